```python
import math
import jax, jax.numpy as jnp
from jax import lax
import numpy as np

D_MODEL = 1024
BATCH = 8
SEQ = 4096
DEPTH = 2

ATTN_HEADS = 8
ATTN_KV_HEADS = 2
HEAD_DIM = 64
WINDOW = 128
ATTN_BLOCK = 128
HGRN_HEADS = 4
HGRN_KEY_DIM = 64
HGRN_VAL_DIM = 64
HGRN_CHUNK = 64
S5_GROUPS = 16
S5_GROUP_CH = 16
S5_STATE = 64
D_FF = 2816
EPS = 1e-6

ATTN_WIDTH = ATTN_HEADS * HEAD_DIM
KV_WIDTH = ATTN_KV_HEADS * HEAD_DIM
HGRN_KEY_WIDTH = HGRN_HEADS * HGRN_KEY_DIM
HGRN_WIDTH = HGRN_HEADS * HGRN_VAL_DIM
S5_WIDTH = S5_GROUPS * S5_GROUP_CH
MIX_WIDTH = ATTN_WIDTH + HGRN_WIDTH + S5_WIDTH
IN_PROJ_WIDTH = ATTN_WIDTH + 2 * KV_WIDTH + 2 * HGRN_KEY_WIDTH + 2 * HGRN_WIDTH + S5_WIDTH

kernel_name = "hymba_swa_hgrn2_s5_macaron"


def rms_norm(x, g):
    xf = x.astype(jnp.float32)
    y = xf * lax.rsqrt(jnp.mean(xf * xf, axis=-1, keepdims=True) + EPS)
    return (y * g.astype(jnp.float32)).astype(x.dtype)


def swiglu(x, w_gate, w_up, w_down):
    return (jax.nn.silu(x @ w_gate) * (x @ w_up)) @ w_down


def alibi_slopes(n_heads):
    return 2.0 ** (-8.0 * (jnp.arange(n_heads, dtype=jnp.float32) + 1.0) / n_heads)


def sliding_window_attention(q, k, v, sinks):
    b_, l_, h_, d_ = q.shape
    hkv = k.shape[2]
    grp = h_ // hkv
    nb = l_ // ATTN_BLOCK
    qb = q.reshape(b_, nb, ATTN_BLOCK, hkv, grp, d_)
    kb = k.reshape(b_, nb, ATTN_BLOCK, hkv, d_)
    vb = v.reshape(b_, nb, ATTN_BLOCK, hkv, d_)

    def with_prev(t):
        prev = jnp.pad(t[:, :-1], ((0, 0), (1, 0), (0, 0), (0, 0), (0, 0)))
        return jnp.concatenate([prev, t], axis=2)

    kk, vv = with_prev(kb), with_prev(vb)
    s = jnp.einsum('bnqhgd,bnkhd->bnhgqk', qb, kk).astype(jnp.float32) * (1.0 / math.sqrt(d_))
    qi = jnp.arange(ATTN_BLOCK)
    ki = jnp.arange(2 * ATTN_BLOCK)
    rel = qi[:, None] + ATTN_BLOCK - ki[None, :]
    in_win = (rel >= 0) & (rel < WINDOW)
    key_pos = jnp.arange(nb)[:, None] * ATTN_BLOCK - ATTN_BLOCK + ki[None, :]
    mask = in_win[None] & (key_pos >= 0)[:, None, :]
    slopes = alibi_slopes(h_).reshape(hkv, grp)
    s = s - slopes[:, :, None, None] * rel.astype(jnp.float32)
    s = jnp.where(mask[None, :, None, None], s, -jnp.inf)
    sink = sinks.astype(jnp.float32).reshape(hkv, grp)[:, :, None, None]
    m = jnp.maximum(jnp.max(s, axis=-1, keepdims=True), sink)
    p = jnp.exp(s - m)
    denom = jnp.sum(p, axis=-1, keepdims=True) + jnp.exp(sink - m)
    p = (p / denom).astype(v.dtype)
    o = jnp.einsum('bnhgqk,bnkhd->bnqhgd', p, vv)
    return o.reshape(b_, l_, h_ * d_)


def hgrn2(q_raw, f_raw, i_raw, g_raw, lb, norm_gain):
    b_, l_, _ = q_raw.shape
    nc = l_ // HGRN_CHUNK
    lbf = lb.astype(jnp.float32)
    zf = f_raw.astype(jnp.float32)
    log_f = jnp.logaddexp(jnp.log(lbf), jnp.log1p(-lbf) + jax.nn.log_sigmoid(zf))
    k = (1.0 - lbf) * jax.nn.sigmoid(-zf)
    q = jax.nn.silu(q_raw)

    def to_chunks(t, d):
        return t.reshape(b_, nc, HGRN_CHUNK, HGRN_HEADS, d).transpose(1, 0, 3, 2, 4)

    qc = to_chunks(q, HGRN_KEY_DIM)
    kc = to_chunks(k, HGRN_KEY_DIM)
    lfc = to_chunks(log_f, HGRN_KEY_DIM)
    vc = to_chunks(i_raw, HGRN_VAL_DIM)
    tri = jnp.tril(jnp.ones((HGRN_CHUNK, HGRN_CHUNK), dtype=bool))

    def step(state, xs):
        qt, kt, vt, lft = xs
        bcum = jnp.cumsum(lft, axis=2)
        inter = jnp.einsum('bhtk,bhkv->bhtv', qt * jnp.exp(bcum), state)
        diff = bcum[:, :, :, None, :] - bcum[:, :, None, :, :]
        decay = jnp.exp(jnp.where(tri[None, None, :, :, None], diff, -jnp.inf))
        att = jnp.einsum('bhtk,bhsk,bhtsk->bhts', qt, kt, decay)
        intra = jnp.einsum('bhts,bhsv->bhtv', att, vt)
        b_last = bcum[:, :, -1, :]
        new_state = state * jnp.exp(b_last)[..., None] + jnp.einsum(
            'bhsk,bhsv->bhkv', kt * jnp.exp(b_last[:, :, None, :] - bcum), vt)
        return new_state, inter + intra

    s0 = jnp.zeros((b_, HGRN_HEADS, HGRN_KEY_DIM, HGRN_VAL_DIM), jnp.float32)
    _, oc = lax.scan(step, s0, (qc, kc, vc, lfc))
    o = oc.transpose(1, 0, 3, 2, 4).reshape(b_, l_, HGRN_HEADS, HGRN_VAL_DIM)
    o = o * lax.rsqrt(jnp.mean(o * o, axis=-1, keepdims=True) + EPS)
    o = o * norm_gain.astype(jnp.float32).reshape(HGRN_HEADS, HGRN_VAL_DIM)
    o = o.reshape(b_, l_, HGRN_WIDTH) * jax.nn.silu(g_raw.astype(jnp.float32))
    return o.astype(q_raw.dtype)


def s5_ssm(u, a_re, a_im, log_dt, b_re, b_im, c_re, c_im, d_skip, glu_w, glu_b):
    b_, l_, _ = u.shape
    f32 = jnp.float32
    uf = u.astype(f32).reshape(b_, l_, S5_GROUPS, S5_GROUP_CH)
    ar, ai = a_re.astype(f32), a_im.astype(f32)
    dt = jnp.exp(log_dt.astype(f32))[:, None]
    mag = jnp.exp(ar * dt)
    abar_re, abar_im = mag * jnp.cos(ai * dt), mag * jnp.sin(ai * dt)
    nr, ni = abar_re - 1.0, abar_im
    den = ar * ar + ai * ai
    z_re = (nr * ar + ni * ai) / den
    z_im = (ni * ar - nr * ai) / den
    br, bi = b_re.astype(f32), b_im.astype(f32)
    bbar_re = z_re[..., None] * br - z_im[..., None] * bi
    bbar_im = z_re[..., None] * bi + z_im[..., None] * br
    bu_re = jnp.einsum('blgc,gpc->blgp', uf, bbar_re)
    bu_im = jnp.einsum('blgc,gpc->blgp', uf, bbar_im)
    a_re_t = jnp.broadcast_to(abar_re, bu_re.shape)
    a_im_t = jnp.broadcast_to(abar_im, bu_im.shape)

    def combine(e1, e2):
        a1r, a1i, b1r, b1i = e1
        a2r, a2i, b2r, b2i = e2
        return (a2r * a1r - a2i * a1i,
                a2r * a1i + a2i * a1r,
                a2r * b1r - a2i * b1i + b2r,
                a2r * b1i + a2i * b1r + b2i)

    _, _, x_re, x_im = lax.associative_scan(combine, (a_re_t, a_im_t, bu_re, bu_im), axis=1)
    y = (jnp.einsum('blgp,gcp->blgc', x_re, c_re.astype(f32))
         - jnp.einsum('blgp,gcp->blgc', x_im, c_im.astype(f32))
         + d_skip.astype(f32) * uf)
    z = jax.nn.gelu(y.reshape(b_, l_, S5_WIDTH))
    out = z * jax.nn.sigmoid(z @ glu_w.astype(f32) + glu_b.astype(f32))
    return out.astype(u.dtype)


def setup_inputs(seed: int = 0) -> dict:
    key = jax.random.key(seed)
    ks = jax.random.split(key, 32)
    f32 = jnp.float32
    nrm = lambda k, shape, scale: jax.random.normal(k, shape, f32) * scale
    gain = lambda k, shape: 1.0 + 0.02 * jax.random.normal(k, shape, f32)
    return {
        "x": jax.random.normal(ks[0], (BATCH, SEQ, D_MODEL), f32),
        "norm_ffn1": gain(ks[1], (DEPTH, D_MODEL)),
        "ffn1_w_gate": nrm(ks[2], (DEPTH, D_MODEL, D_FF), D_MODEL ** -0.5),
        "ffn1_w_up": nrm(ks[3], (DEPTH, D_MODEL, D_FF), D_MODEL ** -0.5),
        "ffn1_w_down": nrm(ks[4], (DEPTH, D_FF, D_MODEL), D_FF ** -0.5),
        "norm_mix": gain(ks[5], (DEPTH, D_MODEL)),
        "w_in": nrm(ks[6], (DEPTH, D_MODEL, IN_PROJ_WIDTH), D_MODEL ** -0.5),
        "attn_sinks": nrm(ks[7], (DEPTH, ATTN_HEADS), 1.0),
        "hgrn_lb_logits": nrm(ks[8], (DEPTH, HGRN_KEY_WIDTH), 1.0),
        "hgrn_norm": gain(ks[9], (DEPTH, HGRN_WIDTH)),
        "s5_a_re": -0.5 + 0.01 * jax.random.normal(ks[10], (DEPTH, S5_GROUPS, S5_STATE), f32),
        "s5_a_im": jnp.pi * jnp.arange(S5_STATE, dtype=f32) + 0.01 * jax.random.normal(ks[11], (DEPTH, S5_GROUPS, S5_STATE), f32),
        "s5_log_dt": jax.random.uniform(ks[12], (DEPTH, S5_GROUPS), f32, math.log(1e-3), math.log(1e-1)),
        "s5_b_re": nrm(ks[13], (DEPTH, S5_GROUPS, S5_STATE, S5_GROUP_CH), (2 * S5_GROUP_CH) ** -0.5),
        "s5_b_im": nrm(ks[14], (DEPTH, S5_GROUPS, S5_STATE, S5_GROUP_CH), (2 * S5_GROUP_CH) ** -0.5),
        "s5_c_re": nrm(ks[15], (DEPTH, S5_GROUPS, S5_GROUP_CH, S5_STATE), (S5_STATE / 2) ** -0.5),
        "s5_c_im": nrm(ks[16], (DEPTH, S5_GROUPS, S5_GROUP_CH, S5_STATE), (S5_STATE / 2) ** -0.5),
        "s5_d": nrm(ks[17], (DEPTH, S5_GROUPS, S5_GROUP_CH), 1.0),
        "s5_glu_w": nrm(ks[18], (DEPTH, S5_WIDTH, S5_WIDTH), S5_WIDTH ** -0.5),
        "s5_glu_b": nrm(ks[19], (DEPTH, S5_WIDTH), 0.01),
        "w_out": nrm(ks[20], (DEPTH, MIX_WIDTH, D_MODEL), MIX_WIDTH ** -0.5),
        "norm_ffn2": gain(ks[21], (DEPTH, D_MODEL)),
        "ffn2_w_gate": nrm(ks[22], (DEPTH, D_MODEL, D_FF), D_MODEL ** -0.5),
        "ffn2_w_up": nrm(ks[23], (DEPTH, D_MODEL, D_FF), D_MODEL ** -0.5),
        "ffn2_w_down": nrm(ks[24], (DEPTH, D_FF, D_MODEL), D_FF ** -0.5),
        "norm_final": gain(ks[25], (D_MODEL,)),
    }


def reference(x, norm_ffn1, ffn1_w_gate, ffn1_w_up, ffn1_w_down, norm_mix, w_in, attn_sinks,
              hgrn_lb_logits, hgrn_norm, s5_a_re, s5_a_im, s5_log_dt, s5_b_re, s5_b_im,
              s5_c_re, s5_c_im, s5_d, s5_glu_w, s5_glu_b, w_out, norm_ffn2, ffn2_w_gate,
              ffn2_w_up, ffn2_w_down, norm_final):
    b_, l_, _ = x.shape
    lbs = jnp.cumsum(jax.nn.softmax(hgrn_lb_logits.astype(jnp.float32), axis=0), axis=0)
    lbs = lbs - lbs[0:1]
    sizes = [ATTN_WIDTH, KV_WIDTH, KV_WIDTH, HGRN_KEY_WIDTH, HGRN_KEY_WIDTH, HGRN_WIDTH, HGRN_WIDTH, S5_WIDTH]
    cuts = [int(c) for c in np.cumsum(sizes)[:-1]]
    for layer in range(DEPTH):
        x = x + 0.5 * swiglu(rms_norm(x, norm_ffn1[layer]), ffn1_w_gate[layer], ffn1_w_up[layer], ffn1_w_down[layer])
        u = rms_norm(x, norm_mix[layer]) @ w_in[layer]
        q_a, k_a, v_a, q_b, f_b, i_b, g_b, u_c = jnp.split(u, cuts, axis=-1)
        y_a = sliding_window_attention(
            q_a.reshape(b_, l_, ATTN_HEADS, HEAD_DIM),
            k_a.reshape(b_, l_, ATTN_KV_HEADS, HEAD_DIM),
            v_a.reshape(b_, l_, ATTN_KV_HEADS, HEAD_DIM),
            attn_sinks[layer])
        y_b = hgrn2(q_b, f_b, i_b, g_b, lbs[layer], hgrn_norm[layer])
        y_c = s5_ssm(u_c, s5_a_re[layer], s5_a_im[layer], s5_log_dt[layer], s5_b_re[layer], s5_b_im[layer],
                     s5_c_re[layer], s5_c_im[layer], s5_d[layer], s5_glu_w[layer], s5_glu_b[layer])
        x = x + jnp.concatenate([y_a, y_b, y_c], axis=-1) @ w_out[layer]
        x = x + 0.5 * swiglu(rms_norm(x, norm_ffn2[layer]), ffn2_w_gate[layer], ffn2_w_up[layer], ffn2_w_down[layer])
    return rms_norm(x, norm_final)
```

```python
import functools
import math

import jax
import jax.numpy as jnp
from jax import lax
from jax.experimental import pallas as pl
from jax.experimental.pallas import tpu as pltpu

D_MODEL = 1024
D_FF = 2816
EPS = 1e-6

ATTN_HEADS = 8
ATTN_KV_HEADS = 2
ATTN_GROUP = ATTN_HEADS // ATTN_KV_HEADS
HEAD_DIM = 64
WINDOW = 128
ATTN_WIDTH = ATTN_HEADS * HEAD_DIM
KV_WIDTH = ATTN_KV_HEADS * HEAD_DIM

HGRN_HEADS = 4
HGRN_DIM = 64
HGRN_WIDTH = HGRN_HEADS * HGRN_DIM
HGRN_CHUNK = 64
HGRN_LEVELS = 6

S5_GROUPS = 16
S5_GROUP_CH = 16
S5_STATE = 64
S5_WIDTH = S5_GROUPS * S5_GROUP_CH
S5_STATES = S5_GROUPS * S5_STATE

IN_PROJ_WIDTH = ATTN_WIDTH + 2 * KV_WIDTH + 4 * HGRN_WIDTH + S5_WIDTH

VMEM_LIMIT_BYTES = 56 * 1024 * 1024

TOKEN_TILE = 512
MASK_VALUE = -1e30

F32 = jnp.float32
BF16 = jnp.bfloat16


def _rms(x, gain):
    return x * lax.rsqrt(jnp.mean(x * x, axis=-1, keepdims=True) + EPS) * gain


def _bdot(a, b):
    return jnp.dot(a.astype(BF16), b.astype(BF16), preferred_element_type=F32)


def _params(n_axes=1):
    return pltpu.CompilerParams(
        dimension_semantics=("arbitrary",) * n_axes,
        vmem_limit_bytes=VMEM_LIMIT_BYTES,
    )


def _full(shape):
    return pl.BlockSpec(shape, lambda i: (0,) * len(shape))


def _ffn_kernel(x_ref, g_ref, wg_ref, wu_ref, wd_ref, gf_ref, o_ref, *, final_norm):
    x = x_ref[...]
    h = _rms(x, g_ref[...]).astype(BF16)
    gate = jnp.dot(h, wg_ref[...], preferred_element_type=F32)
    up = jnp.dot(h, wu_ref[...], preferred_element_type=F32)
    act = (gate * jax.nn.sigmoid(gate) * up).astype(BF16)
    y = x + 0.5 * jnp.dot(act, wd_ref[...], preferred_element_type=F32)
    if final_norm:
        y = _rms(y, gf_ref[...])
    o_ref[...] = y


def _ffn(x, gain, w_gate, w_up, w_down, final_gain, final_norm):
    n = x.shape[0]
    tm = min(TOKEN_TILE, n)
    return pl.pallas_call(
        functools.partial(_ffn_kernel, final_norm=final_norm),
        grid=(n // tm,),
        in_specs=[
            pl.BlockSpec((tm, D_MODEL), lambda i: (i, 0)),
            _full((1, D_MODEL)),
            _full((D_MODEL, D_FF)),
            _full((D_MODEL, D_FF)),
            _full((D_FF, D_MODEL)),
            _full((1, D_MODEL)),
        ],
        out_specs=pl.BlockSpec((tm, D_MODEL), lambda i: (i, 0)),
        out_shape=jax.ShapeDtypeStruct((n, D_MODEL), F32),
        compiler_params=_params(),
        name="ffn",
    )(x, gain, w_gate, w_up, w_down, final_gain)


def _in_proj_kernel(x_ref, g_ref, w_ref, u_ref):
    u_ref[...] = _bdot(_rms(x_ref[...], g_ref[...]), w_ref[...])


def _in_proj(x, gain, w_in):
    n = x.shape[0]
    tm = min(TOKEN_TILE, n)
    return pl.pallas_call(
        _in_proj_kernel,
        grid=(n // tm,),
        in_specs=[
            pl.BlockSpec((tm, D_MODEL), lambda i: (i, 0)),
            _full((1, D_MODEL)),
            _full((D_MODEL, IN_PROJ_WIDTH)),
        ],
        out_specs=pl.BlockSpec((tm, IN_PROJ_WIDTH), lambda i: (i, 0)),
        out_shape=jax.ShapeDtypeStruct((n, IN_PROJ_WIDTH), F32),
        compiler_params=_params(),
        name="in_proj",
    )(x, gain, w_in)


def _out_proj_kernel(x_ref, ya_ref, yb_ref, yc_ref, w_ref, o_ref):
    y = jnp.concatenate([ya_ref[...], yb_ref[...], yc_ref[...]], axis=-1)
    o_ref[...] = x_ref[...] + _bdot(y, w_ref[...])


def _out_proj(x, ya, yb, yc, w_out):
    n = x.shape[0]
    tm = min(TOKEN_TILE, n)
    row = lambda width: pl.BlockSpec((tm, width), lambda i: (i, 0))
    return pl.pallas_call(
        _out_proj_kernel,
        grid=(n // tm,),
        in_specs=[row(D_MODEL), row(ATTN_WIDTH), row(HGRN_WIDTH), row(S5_WIDTH),
                  _full((D_MODEL, D_MODEL))],
        out_specs=row(D_MODEL),
        out_shape=jax.ShapeDtypeStruct((n, D_MODEL), F32),
        compiler_params=_params(),
        name="out_proj",
    )(x, ya, yb, yc, w_out)


def _alibi_slope(head):
    return 2.0 ** (-8.0 * (head + 1.0) / ATTN_HEADS)


def _attn_kernel(sink_ref, q_ref, kv_ref, kvp_ref, o_ref, *, seq_len, tile):
    first = (pl.program_id(0) * tile) % seq_len == 0
    keys = jnp.concatenate([kvp_ref[...], kv_ref[...]], axis=0)
    rows = ATTN_GROUP * WINDOW
    t_idx = lax.broadcasted_iota(jnp.int32, (rows, 2 * WINDOW), 0) % WINDOW
    c_idx = lax.broadcasted_iota(jnp.int32, (rows, 2 * WINDOW), 1)
    rel = t_idx + WINDOW - c_idx
    in_win = (rel >= 0) & (rel < WINDOW)
    relf = rel.astype(F32)
    g_idx = lax.broadcasted_iota(jnp.int32, (rows, 1), 0) // WINDOW
    scale = 1.0 / math.sqrt(HEAD_DIM)
    for blk in range(tile // WINDOW):
        qb = q_ref[blk * WINDOW:(blk + 1) * WINDOW, :]
        kvb = keys[blk * WINDOW:(blk + 2) * WINDOW, :]
        if blk == 0:
            mask = in_win & (c_idx >= jnp.where(first, WINDOW, 0))
        else:
            mask = in_win
        outs = []
        for hk in range(ATTN_KV_HEADS):
            kh = kvb[:, hk * HEAD_DIM:(hk + 1) * HEAD_DIM]
            vh = kvb[:, KV_WIDTH + hk * HEAD_DIM:KV_WIDTH + (hk + 1) * HEAD_DIM]
            heads = [hk * ATTN_GROUP + g for g in range(ATTN_GROUP)]
            qs = jnp.concatenate(
                [qb[:, h * HEAD_DIM:(h + 1) * HEAD_DIM] for h in heads], axis=0)
            s = lax.dot_general(qs.astype(BF16), kh.astype(BF16),
                                (((1,), (1,)), ((), ())),
                                preferred_element_type=F32) * scale
            slope = jnp.zeros((rows, 1), F32)
            sink = jnp.zeros((rows, 1), F32)
            for g, h in enumerate(heads):
                slope = jnp.where(g_idx == g, _alibi_slope(h), slope)
                sink = jnp.where(g_idx == g, sink_ref[h], sink)
            s = s - slope * relf
            s = jnp.where(mask, s, MASK_VALUE)
            m = jnp.maximum(jnp.max(s, axis=-1, keepdims=True), sink)
            p = jnp.exp(s - m)
            denom = jnp.sum(p, axis=-1, keepdims=True) + jnp.exp(sink - m)
            p = p / denom
            o = _bdot(p, vh)
            outs.extend(o[g * WINDOW:(g + 1) * WINDOW, :] for g in range(ATTN_GROUP))
        o_ref[blk * WINDOW:(blk + 1) * WINDOW, :] = jnp.concatenate(outs, axis=-1)


def _attention(u, sinks, seq_len):
    n = u.shape[0]
    tile = min(TOKEN_TILE, seq_len)
    blocks_per_tile = tile // WINDOW
    kv_col = ATTN_WIDTH // (2 * KV_WIDTH)
    return pl.pallas_call(
        functools.partial(_attn_kernel, seq_len=seq_len, tile=tile),
        grid=(n // tile,),
        in_specs=[
            pl.BlockSpec(memory_space=pltpu.SMEM),
            pl.BlockSpec((tile, ATTN_WIDTH), lambda i: (i, 0)),
            pl.BlockSpec((tile, 2 * KV_WIDTH), lambda i: (i, kv_col)),
            pl.BlockSpec((WINDOW, 2 * KV_WIDTH),
                         lambda i: (jnp.maximum(i * blocks_per_tile - 1, 0), kv_col)),
        ],
        out_specs=pl.BlockSpec((tile, ATTN_WIDTH), lambda i: (i, 0)),
        out_shape=jax.ShapeDtypeStruct((n, ATTN_WIDTH), F32),
        compiler_params=_params(),
        name="swa_attention",
    )(sinks, u, u, u)


def _hgrn_exponent_matrix():
    c = HGRN_CHUNK
    r = jnp.arange(c)[:, None]
    j = jnp.arange(c)[None, :]
    blocks = []
    for lvl in range(HGRN_LEVELS):
        m = 1 << lvl
        start = (r // (2 * m)) * (2 * m)
        right = (r // m) % 2 == 1
        in_right = right & (j >= start + m) & (j <= r)
        in_left = (~right) & (j > r) & (j < start + m)
        blocks.append(in_right | in_left)
    blocks.append(j <= r)
    blocks.append(j > r)
    return jnp.concatenate(blocks, axis=0).astype(BF16)


def _split3(x):
    hi = x.astype(BF16)
    r1 = x - hi.astype(F32)
    mid = r1.astype(BF16)
    lo = (r1 - mid.astype(F32)).astype(BF16)
    return hi, mid, lo


def _hgrn_kernel(q_ref, f_ref, i_ref, g_ref, lbl_ref, gain_ref, em_ref, o_ref,
                 state_ref, *, seq_len, tile, layer):
    c = HGRN_CHUNK
    first = (pl.program_id(0) * tile) % seq_len == 0

    @pl.when(first)
    def _():
        state_ref[...] = jnp.zeros_like(state_ref)

    logits = lbl_ref[...]
    e = jnp.exp(logits - jnp.max(logits, axis=0, keepdims=True))
    sm = e / jnp.sum(e, axis=0, keepdims=True)
    lb = jnp.zeros((1, HGRN_WIDTH), F32)
    for l in range(1, layer + 1):
        lb = lb + sm[l:l + 1, :]
    log_lb = jnp.log(lb)
    log_1m_lb = jnp.log1p(-lb)

    r_idx = lax.broadcasted_iota(jnp.int32, (c, 1), 0)
    t_idx = lax.broadcasted_iota(jnp.int32, (c, c), 0)
    s_idx = lax.broadcasted_iota(jnp.int32, (c, c), 1)
    em = em_ref[...]

    def chunk(ci, carry):
        rows = pl.ds(pl.multiple_of(ci * c, c), c)
        z = f_ref[rows, :]
        q = q_ref[rows, :]
        q = q * jax.nn.sigmoid(q)
        v = i_ref[rows, :]
        log_sig = jnp.minimum(z, 0.0) - jnp.log1p(jnp.exp(-jnp.abs(z)))
        bterm = log_1m_lb + log_sig
        hi_ = jnp.maximum(log_lb, bterm)
        lf = hi_ + jnp.log1p(jnp.exp(-jnp.abs(log_lb - bterm)))
        k = (1.0 - lb) * jax.nn.sigmoid(-z)

        parts = _split3(lf)
        expo = sum(jnp.dot(em, p, preferred_element_type=F32) for p in parts)
        decay = jnp.exp(expo)

        outs = []
        for h in range(HGRN_HEADS):
            cols = slice(h * HGRN_DIM, (h + 1) * HGRN_DIM)
            qh, kh, vh = q[:, cols], k[:, cols], v[:, cols]
            att = jnp.where(t_idx == s_idx,
                            jnp.sum(qh * kh, axis=-1, keepdims=True), 0.0)
            for lvl in range(HGRN_LEVELS):
                m = 1 << lvl
                d = decay[lvl * c:(lvl + 1) * c, cols]
                right = (r_idx // m) % 2 == 1
                ql = jnp.where(right, qh * d, 0.0)
                kl = jnp.where(right, 0.0, kh * d)
                a = lax.dot_general(ql.astype(BF16), kl.astype(BF16),
                                    (((1,), (1,)), ((), ())),
                                    preferred_element_type=F32)
                same = (t_idx // (2 * m)) == (s_idx // (2 * m))
                att = att + jnp.where(same, a, 0.0)
            d_in = decay[HGRN_LEVELS * c:(HGRN_LEVELS + 1) * c, cols]
            d_out = decay[(HGRN_LEVELS + 1) * c:(HGRN_LEVELS + 2) * c, cols]
            state_t = state_ref[h]
            o = _bdot(att, vh) + lax.dot_general(
                (qh * d_in).astype(BF16), state_t.astype(BF16),
                (((1,), (1,)), ((), ())), preferred_element_type=F32)
            upd_t = lax.dot_general(vh.astype(BF16), (kh * d_out).astype(BF16),
                                    (((0,), (0,)), ((), ())),
                                    preferred_element_type=F32)
            state_ref[h] = state_t * d_in[c - 1:c, :] + upd_t
            o = o * lax.rsqrt(jnp.mean(o * o, axis=-1, keepdims=True) + EPS)
            outs.append(o)
        o_all = jnp.concatenate(outs, axis=-1) * gain_ref[...]
        gate = g_ref[rows, :]
        o_ref[rows, :] = o_all * (gate * jax.nn.sigmoid(gate))
        return carry

    lax.fori_loop(0, tile // c, chunk, 0)


def _hgrn(u, lb_logits, gain, layer, seq_len):
    n = u.shape[0]
    tile = min(TOKEN_TILE, seq_len)
    depth = lb_logits.shape[0]
    col0 = (ATTN_WIDTH + 2 * KV_WIDTH) // HGRN_WIDTH
    col = lambda j: pl.BlockSpec((tile, HGRN_WIDTH), lambda i: (i, col0 + j))
    em = _hgrn_exponent_matrix()
    return pl.pallas_call(
        functools.partial(_hgrn_kernel, seq_len=seq_len, tile=tile, layer=layer),
        grid=(n // tile,),
        in_specs=[col(0), col(1), col(2), col(3),
                  _full((depth, HGRN_WIDTH)), _full((1, HGRN_WIDTH)),
                  _full(em.shape)],
        out_specs=pl.BlockSpec((tile, HGRN_WIDTH), lambda i: (i, 0)),
        out_shape=jax.ShapeDtypeStruct((n, HGRN_WIDTH), F32),
        scratch_shapes=[pltpu.VMEM((HGRN_HEADS, HGRN_DIM, HGRN_DIM), F32)],
        compiler_params=_params(),
        name="hgrn2",
    )(u, u, u, u, lb_logits, gain, em)


def _s5_prep_kernel(ar_ref, ai_ref, ldt_ref, br_ref, bi_ref,
                    abr_ref, abi_ref, bbr_ref, bbi_ref):
    ar, ai = ar_ref[...], ai_ref[...]
    dt = jnp.exp(ldt_ref[...])
    mag = jnp.exp(ar * dt)
    abar_re = mag * jnp.cos(ai * dt)
    abar_im = mag * jnp.sin(ai * dt)
    nr, ni = abar_re - 1.0, abar_im
    den = ar * ar + ai * ai
    z_re = (nr * ar + ni * ai) / den
    z_im = (ni * ar - nr * ai) / den
    abr_ref[...] = abar_re
    abi_ref[...] = abar_im
    br, bi = br_ref[...], bi_ref[...]
    bbr_ref[...] = z_re[:, None, :] * br - z_im[:, None, :] * bi
    bbi_ref[...] = z_re[:, None, :] * bi + z_im[:, None, :] * br


def _s5_prep(a_re, a_im, log_dt, b_re, b_im):
    gp = jax.ShapeDtypeStruct((S5_GROUPS, S5_STATE), F32)
    gcp = jax.ShapeDtypeStruct((S5_GROUPS, S5_GROUP_CH, S5_STATE), F32)
    return pl.pallas_call(
        _s5_prep_kernel,
        out_shape=(gp, gp, gcp, gcp),
        name="s5_discretize",
    )(a_re, a_im, log_dt[:, None], jnp.swapaxes(b_re, 1, 2), jnp.swapaxes(b_im, 1, 2))


def _block_diag(blocks):
    g, r, c = blocks.shape
    eye = jnp.eye(g, dtype=blocks.dtype)
    return (blocks[:, :, None, :] * eye[:, None, :, None]).reshape(g * r, g * c)


def _s5_kernel(u_ref, a_ref, b_ref, c_ref, d_ref, gw_ref, gb_ref, o_ref,
               x_ref, state_ref, *, batch, steps):
    @pl.when(pl.program_id(0) == 0)
    def _():
        state_ref[...] = jnp.zeros_like(state_ref)

    u = u_ref[...]
    x_ref[...] = _bdot(u, b_ref[...])
    a_re = jnp.broadcast_to(a_ref[0:1, :], (batch, S5_STATES))
    a_im = jnp.broadcast_to(a_ref[1:2, :], (batch, S5_STATES))

    def step(t, carry):
        x_re, x_im = carry
        rows = pl.ds(pl.multiple_of(t * batch, batch), batch)
        bu_re = x_ref[rows, 0:S5_STATES]
        bu_im = x_ref[rows, S5_STATES:2 * S5_STATES]
        n_re = a_re * x_re - a_im * x_im + bu_re
        n_im = a_re * x_im + a_im * x_re + bu_im
        x_ref[rows, 0:S5_STATES] = n_re
        x_ref[rows, S5_STATES:2 * S5_STATES] = n_im
        return n_re, n_im

    x_re, x_im = lax.fori_loop(
        0, steps, step,
        (state_ref[:, 0:S5_STATES], state_ref[:, S5_STATES:2 * S5_STATES]))
    state_ref[:, 0:S5_STATES] = x_re
    state_ref[:, S5_STATES:2 * S5_STATES] = x_im

    y = _bdot(x_ref[...], c_ref[...]) + d_ref[...] * u
    z = jax.nn.gelu(y)
    o_ref[...] = z * jax.nn.sigmoid(_bdot(z, gw_ref[...]) + gb_ref[...])


def _s5(u_tm, abar, b_blk, c_blk, d_skip, glu_w, glu_b, batch, seq_len):
    steps = min(64, seq_len)
    rows = steps * batch
    return pl.pallas_call(
        functools.partial(_s5_kernel, batch=batch, steps=steps),
        grid=(seq_len // steps,),
        in_specs=[
            pl.BlockSpec((rows, S5_WIDTH), lambda i: (i, 0)),
            _full((2, S5_STATES)),
            _full((S5_WIDTH, 2 * S5_STATES)),
            _full((2 * S5_STATES, S5_WIDTH)),
            _full((1, S5_WIDTH)),
            _full((S5_WIDTH, S5_WIDTH)),
            _full((1, S5_WIDTH)),
        ],
        out_specs=pl.BlockSpec((rows, S5_WIDTH), lambda i: (i, 0)),
        out_shape=jax.ShapeDtypeStruct((seq_len * batch, S5_WIDTH), F32),
        scratch_shapes=[pltpu.VMEM((rows, 2 * S5_STATES), F32),
                        pltpu.VMEM((batch, 2 * S5_STATES), F32)],
        compiler_params=_params(),
        name="s5_ssm",
    )(u_tm, abar, b_blk, c_blk, d_skip, glu_w, glu_b)


def _s5_mixer(u, batch, seq_len, a_re, a_im, log_dt, b_re, b_im, c_re, c_im, d_skip,
              glu_w, glu_b):
    abar_re, abar_im, bbar_re, bbar_im = _s5_prep(a_re, a_im, log_dt, b_re, b_im)
    abar = jnp.stack([abar_re.reshape(-1), abar_im.reshape(-1)])
    b_blk = jnp.concatenate([_block_diag(bbar_re), _block_diag(bbar_im)], axis=1)
    c_blk = jnp.concatenate([_block_diag(jnp.swapaxes(c_re, 1, 2)),
                             -_block_diag(jnp.swapaxes(c_im, 1, 2))], axis=0)
    u_c = u[:, IN_PROJ_WIDTH - S5_WIDTH:].reshape(batch, seq_len, S5_WIDTH)
    u_tm = jnp.swapaxes(u_c, 0, 1).reshape(seq_len * batch, S5_WIDTH)
    y_tm = _s5(u_tm, abar, b_blk.astype(BF16), c_blk.astype(BF16),
               d_skip.reshape(1, S5_WIDTH), glu_w.astype(BF16),
               glu_b.reshape(1, S5_WIDTH), batch, seq_len)
    y = jnp.swapaxes(y_tm.reshape(seq_len, batch, S5_WIDTH), 0, 1)
    return y.reshape(batch * seq_len, S5_WIDTH)


def kernel(x, norm_ffn1, ffn1_w_gate, ffn1_w_up, ffn1_w_down, norm_mix, w_in, attn_sinks,
           hgrn_lb_logits, hgrn_norm, s5_a_re, s5_a_im, s5_log_dt, s5_b_re, s5_b_im,
           s5_c_re, s5_c_im, s5_d, s5_glu_w, s5_glu_b, w_out, norm_ffn2, ffn2_w_gate,
           ffn2_w_up, ffn2_w_down, norm_final):
    batch, seq_len, _ = x.shape
    depth = w_in.shape[0]
    h = x.reshape(batch * seq_len, D_MODEL)
    row = lambda v: v.reshape(1, -1)
    final_gain = row(norm_final)
    for layer in range(depth):
        h = _ffn(h, row(norm_ffn1[layer]), ffn1_w_gate[layer].astype(BF16),
                 ffn1_w_up[layer].astype(BF16), ffn1_w_down[layer].astype(BF16),
                 final_gain, False)
        u = _in_proj(h, row(norm_mix[layer]), w_in[layer].astype(BF16))
        y_a = _attention(u, attn_sinks[layer], seq_len)
        y_b = _hgrn(u, hgrn_lb_logits, row(hgrn_norm[layer]), layer, seq_len)
        y_c = _s5_mixer(u, batch, seq_len, s5_a_re[layer], s5_a_im[layer], s5_log_dt[layer],
                        s5_b_re[layer], s5_b_im[layer], s5_c_re[layer], s5_c_im[layer],
                        s5_d[layer], s5_glu_w[layer], s5_glu_b[layer])
        h = _out_proj(h, y_a, y_b, y_c, w_out[layer].astype(BF16))
        h = _ffn(h, row(norm_ffn2[layer]), ffn2_w_gate[layer].astype(BF16),
                 ffn2_w_up[layer].astype(BF16), ffn2_w_down[layer].astype(BF16),
                 final_gain, layer == depth - 1)
    return h.reshape(batch, seq_len, D_MODEL)
```

```python
import functools
import math

import jax
import jax.numpy as jnp
from jax import lax
from jax.experimental import pallas as pl
from jax.experimental.pallas import tpu as pltpu

D_MODEL = 1024
D_FF = 2816
EPS = 1e-6

ATTN_HEADS = 8
ATTN_KV_HEADS = 2
ATTN_GROUP = ATTN_HEADS // ATTN_KV_HEADS
HEAD_DIM = 64
WINDOW = 128
ATTN_WIDTH = ATTN_HEADS * HEAD_DIM
KV_WIDTH = ATTN_KV_HEADS * HEAD_DIM

HGRN_HEADS = 4
HGRN_DIM = 64
HGRN_WIDTH = HGRN_HEADS * HGRN_DIM
HGRN_CHUNK = 64
HGRN_LEVELS = 6

S5_GROUPS = 16
S5_GROUP_CH = 16
S5_STATE = 64
S5_WIDTH = S5_GROUPS * S5_GROUP_CH
S5_STATES = S5_GROUPS * S5_STATE

IN_PROJ_WIDTH = ATTN_WIDTH + 2 * KV_WIDTH + 4 * HGRN_WIDTH + S5_WIDTH

VMEM_LIMIT_BYTES = 56 * 1024 * 1024

TOKEN_TILE = 512
S5_TIME_TILE = 128
MASK_VALUE = -1e30

F32 = jnp.float32
BF16 = jnp.bfloat16


def _rms(x, gain):
    return x * lax.rsqrt(jnp.mean(x * x, axis=-1, keepdims=True) + EPS) * gain


def _bdot(a, b):
    return jnp.dot(a.astype(BF16), b.astype(BF16), preferred_element_type=F32)


def _params(n_axes=1):
    return pltpu.CompilerParams(
        dimension_semantics=("arbitrary",) * n_axes,
        vmem_limit_bytes=VMEM_LIMIT_BYTES,
    )


def _full(shape):
    return pl.BlockSpec(shape, lambda i: (0,) * len(shape))


def _ffn_kernel(x_ref, g_ref, wg_ref, wu_ref, wd_ref, gf_ref, o_ref, *, final_norm):
    x = x_ref[...]
    h = _rms(x, g_ref[...]).astype(BF16)
    gate = jnp.dot(h, wg_ref[...], preferred_element_type=F32)
    up = jnp.dot(h, wu_ref[...], preferred_element_type=F32)
    act = (gate * jax.nn.sigmoid(gate) * up).astype(BF16)
    y = x + 0.5 * jnp.dot(act, wd_ref[...], preferred_element_type=F32)
    if final_norm:
        y = _rms(y, gf_ref[...])
    o_ref[...] = y


def _ffn(x, gain, w_gate, w_up, w_down, final_gain, final_norm):
    n = x.shape[0]
    tm = min(TOKEN_TILE, n)
    return pl.pallas_call(
        functools.partial(_ffn_kernel, final_norm=final_norm),
        grid=(n // tm,),
        in_specs=[
            pl.BlockSpec((tm, D_MODEL), lambda i: (i, 0)),
            _full((1, D_MODEL)),
            _full((D_MODEL, D_FF)),
            _full((D_MODEL, D_FF)),
            _full((D_FF, D_MODEL)),
            _full((1, D_MODEL)),
        ],
        out_specs=pl.BlockSpec((tm, D_MODEL), lambda i: (i, 0)),
        out_shape=jax.ShapeDtypeStruct((n, D_MODEL), F32),
        compiler_params=_params(),
        name="ffn",
    )(x, gain, w_gate, w_up, w_down, final_gain)


def _in_proj_kernel(x_ref, g_ref, w_ref, u_ref):
    u_ref[...] = _bdot(_rms(x_ref[...], g_ref[...]), w_ref[...])


def _in_proj(x, gain, w_in):
    n = x.shape[0]
    tm = min(TOKEN_TILE, n)
    return pl.pallas_call(
        _in_proj_kernel,
        grid=(n // tm,),
        in_specs=[
            pl.BlockSpec((tm, D_MODEL), lambda i: (i, 0)),
            _full((1, D_MODEL)),
            _full((D_MODEL, IN_PROJ_WIDTH)),
        ],
        out_specs=pl.BlockSpec((tm, IN_PROJ_WIDTH), lambda i: (i, 0)),
        out_shape=jax.ShapeDtypeStruct((n, IN_PROJ_WIDTH), F32),
        compiler_params=_params(),
        name="in_proj",
    )(x, gain, w_in)


def _out_proj_kernel(x_ref, ya_ref, yb_ref, yc_ref, w_ref, o_ref):
    y = jnp.concatenate([ya_ref[...], yb_ref[...], yc_ref[...]], axis=-1)
    o_ref[...] = x_ref[...] + _bdot(y, w_ref[...])


def _out_proj(x, ya, yb, yc, w_out):
    n = x.shape[0]
    tm = min(TOKEN_TILE, n)
    row = lambda width: pl.BlockSpec((tm, width), lambda i: (i, 0))
    return pl.pallas_call(
        _out_proj_kernel,
        grid=(n // tm,),
        in_specs=[row(D_MODEL), row(ATTN_WIDTH), row(HGRN_WIDTH), row(S5_WIDTH),
                  _full((D_MODEL, D_MODEL))],
        out_specs=row(D_MODEL),
        out_shape=jax.ShapeDtypeStruct((n, D_MODEL), F32),
        compiler_params=_params(),
        name="out_proj",
    )(x, ya, yb, yc, w_out)


def _alibi_slope(head):
    return 2.0 ** (-8.0 * (head + 1.0) / ATTN_HEADS)


def _attn_kernel(sink_ref, q_ref, kv_ref, kvp_ref, o_ref, bias_ref, *, seq_len, tile):
    first = (pl.program_id(0) * tile) % seq_len == 0
    rows = ATTN_GROUP * WINDOW
    c_idx = lax.broadcasted_iota(jnp.int32, (rows, 2 * WINDOW), 1)
    log2e = math.log2(math.e)
    scale = log2e / math.sqrt(HEAD_DIM)

    @pl.when(pl.program_id(0) == 0)
    def _():
        g_idx = lax.broadcasted_iota(jnp.int32, (rows, 1), 0) // WINDOW
        t_idx = lax.broadcasted_iota(jnp.int32, (rows, 2 * WINDOW), 0) % WINDOW
        rel = t_idx + WINDOW - c_idx
        in_win = (rel >= 0) & (rel < WINDOW)
        relf = rel.astype(F32)
        for hk in range(ATTN_KV_HEADS):
            slope = jnp.zeros((rows, 1), F32)
            sink = jnp.zeros((rows, 1), F32)
            for g in range(ATTN_GROUP):
                head = hk * ATTN_GROUP + g
                slope = jnp.where(g_idx == g, _alibi_slope(head) * log2e, slope)
                sink = jnp.where(g_idx == g, sink_ref[head] * log2e, sink)
            bias = jnp.where(in_win, -(slope * relf), MASK_VALUE)
            bias_ref[hk] = jnp.where(c_idx == 0, sink, bias)

    keys = jnp.concatenate([kvp_ref[...], kv_ref[...]], axis=0)
    keys_t = jnp.transpose(keys[:, 0:KV_WIDTH])
    kcol = lax.broadcasted_iota(jnp.int32, (HEAD_DIM, 2 * WINDOW), 1)
    vrow = lax.broadcasted_iota(jnp.int32, (2 * WINDOW, HEAD_DIM), 0)
    no_prev = (c_idx >= 1) & (c_idx < jnp.where(first, WINDOW, 0))

    for blk in range(tile // WINDOW):
        qb = q_ref[blk * WINDOW:(blk + 1) * WINDOW, :] * scale
        outs = []
        for hk in range(ATTN_KV_HEADS):
            kt = keys_t[hk * HEAD_DIM:(hk + 1) * HEAD_DIM, blk * WINDOW:(blk + 2) * WINDOW]
            kt = jnp.where(kcol == 0, 0.0, kt)
            vh = keys[blk * WINDOW:(blk + 2) * WINDOW,
                      KV_WIDTH + hk * HEAD_DIM:KV_WIDTH + (hk + 1) * HEAD_DIM]
            vh = jnp.where(vrow == 0, 0.0, vh)
            heads = [hk * ATTN_GROUP + g for g in range(ATTN_GROUP)]
            qs = jnp.concatenate(
                [qb[:, h * HEAD_DIM:(h + 1) * HEAD_DIM] for h in heads], axis=0)
            s = _bdot(qs, kt) + bias_ref[hk]
            if blk == 0:
                s = jnp.where(no_prev, MASK_VALUE, s)
            p = jnp.exp2(s - jnp.max(s, axis=-1, keepdims=True))
            denom = jnp.sum(p, axis=-1, keepdims=True)
            o = _bdot(p, vh) * (1.0 / denom)
            outs.extend(o[g * WINDOW:(g + 1) * WINDOW, :] for g in range(ATTN_GROUP))
        o_ref[blk * WINDOW:(blk + 1) * WINDOW, :] = jnp.concatenate(outs, axis=-1)


def _attention(u, sinks, seq_len):
    n = u.shape[0]
    tile = min(TOKEN_TILE, seq_len)
    blocks_per_tile = tile // WINDOW
    kv_col = ATTN_WIDTH // (2 * KV_WIDTH)
    return pl.pallas_call(
        functools.partial(_attn_kernel, seq_len=seq_len, tile=tile),
        grid=(n // tile,),
        in_specs=[
            pl.BlockSpec(memory_space=pltpu.SMEM),
            pl.BlockSpec((tile, ATTN_WIDTH), lambda i: (i, 0)),
            pl.BlockSpec((tile, 2 * KV_WIDTH), lambda i: (i, kv_col)),
            pl.BlockSpec((WINDOW, 2 * KV_WIDTH),
                         lambda i: (jnp.maximum(i * blocks_per_tile - 1, 0), kv_col)),
        ],
        out_specs=pl.BlockSpec((tile, ATTN_WIDTH), lambda i: (i, 0)),
        out_shape=jax.ShapeDtypeStruct((n, ATTN_WIDTH), F32),
        scratch_shapes=[pltpu.VMEM((ATTN_KV_HEADS, ATTN_GROUP * WINDOW, 2 * WINDOW), F32)],
        compiler_params=_params(),
        name="swa_attention",
    )(sinks, u, u, u)


HGRN_FACTORED_DECAY_LIMIT = 80.0


def _hgrn_exponent_matrix():
    c = HGRN_CHUNK
    r = jnp.arange(c)[:, None]
    j = jnp.arange(c)[None, :]
    blocks = []
    for lvl in range(HGRN_LEVELS):
        m = 1 << lvl
        start = (r // (2 * m)) * (2 * m)
        right = (r // m) % 2 == 1
        in_right = right & (j >= start + m) & (j <= r)
        in_left = (~right) & (j > r) & (j < start + m)
        blocks.append(in_right | in_left)
    blocks.append(j <= r)
    blocks.append(j > r)
    return jnp.concatenate(blocks, axis=0).astype(BF16)


def _split3(x):
    hi = x.astype(BF16)
    r1 = x - hi.astype(F32)
    mid = r1.astype(BF16)
    lo = (r1 - mid.astype(F32)).astype(BF16)
    return hi, mid, lo


def _hgrn_kernel(q_ref, f_ref, i_ref, g_ref, lbl_ref, gain_ref, em_ref, o_ref,
                 state_ref, qs_ref, ks_ref, lf_ref, bc_ref, *, seq_len, tile, layer):
    c = HGRN_CHUNK
    first = (pl.program_id(0) * tile) % seq_len == 0

    @pl.when(first)
    def _():
        state_ref[...] = jnp.zeros_like(state_ref)

    logits = lbl_ref[...]
    e = jnp.exp(logits - jnp.max(logits, axis=0, keepdims=True))
    sm = e / jnp.sum(e, axis=0, keepdims=True)
    lb = jnp.zeros((1, HGRN_WIDTH), F32)
    for l in range(1, layer + 1):
        lb = lb + sm[l:l + 1, :]
    log_lb = jnp.log(lb)
    log_1m_lb = jnp.log1p(-lb)

    r_idx = lax.broadcasted_iota(jnp.int32, (c, 1), 0)
    t_idx = lax.broadcasted_iota(jnp.int32, (c, c), 0)
    s_idx = lax.broadcasted_iota(jnp.int32, (c, c), 1)
    lane_head = lax.broadcasted_iota(jnp.int32, (c, HGRN_WIDTH), 1) // HGRN_DIM
    n_chunks = tile // c

    def chunk_rows(ci):
        return pl.ds(pl.multiple_of(ci * c, c), c)

    def prepare(ci, carry):
        rows = chunk_rows(ci)
        z = f_ref[rows, :]
        q = q_ref[rows, :]
        qs_ref[rows, :] = q * jax.nn.sigmoid(q)
        log_sig = jnp.minimum(z, 0.0) - jnp.log1p(jnp.exp(-jnp.abs(z)))
        bterm = log_1m_lb + log_sig
        lf = jnp.maximum(log_lb, bterm) + jnp.log1p(jnp.exp(-jnp.abs(log_lb - bterm)))
        ks_ref[rows, :] = (1.0 - lb) * jax.nn.sigmoid(-z)
        lf_ref[rows, :] = lf
        tri = em_ref[HGRN_LEVELS * c:(HGRN_LEVELS + 1) * c, :]
        bc_ref[rows, :] = sum(jnp.dot(tri, p, preferred_element_type=F32)
                              for p in _split3(lf))
        return carry

    lax.fori_loop(0, n_chunks, prepare, 0, unroll=True)
    strong_decay = jnp.min(bc_ref[...]) < -HGRN_FACTORED_DECAY_LIMIT

    def stack_heads(x):
        return jnp.concatenate(
            [jnp.where(lane_head == h, x, 0.0) for h in range(HGRN_HEADS)],
            axis=0).astype(BF16)

    def factored_chunk(ci, carry):
        rows = chunk_rows(ci)
        q, k, v, bc = qs_ref[rows, :], ks_ref[rows, :], i_ref[rows, :], bc_ref[rows, :]
        b_last = bc[c - 1:c, :]
        qd = (q * jnp.exp(bc)).astype(BF16)
        kb = stack_heads(k * jnp.exp(-bc))
        vb = stack_heads(v)
        att = lax.dot_general(qd, kb, (((1,), (1,)), ((), ())),
                              preferred_element_type=F32)
        col_s = lax.broadcasted_iota(jnp.int32, (c, HGRN_WIDTH), 1) % c
        row_t = lax.broadcasted_iota(jnp.int32, (c, HGRN_WIDTH), 0)
        att = jnp.where(col_s <= row_t, att, 0.0)
        state_t = state_ref[...]
        o = jnp.dot(att.astype(BF16), vb, preferred_element_type=F32)
        o = o + lax.dot_general(qd, state_t.astype(BF16), (((1,), (1,)), ((), ())),
                                preferred_element_type=F32)
        upd = lax.dot_general(v.astype(BF16), (k * jnp.exp(b_last - bc)).astype(BF16),
                              (((0,), (0,)), ((), ())), preferred_element_type=F32)
        rh = lax.broadcasted_iota(jnp.int32, (HGRN_WIDTH, HGRN_WIDTH), 0) // HGRN_DIM
        ch = lax.broadcasted_iota(jnp.int32, (HGRN_WIDTH, HGRN_WIDTH), 1) // HGRN_DIM
        state_ref[...] = state_t * jnp.exp(b_last) + jnp.where(rh == ch, upd, 0.0)
        o_ref[rows, :] = o
        return carry

    def general_chunk(ci, carry):
        rows = chunk_rows(ci)
        q, k, v = qs_ref[rows, :], ks_ref[rows, :], i_ref[rows, :]
        expo = sum(jnp.dot(em_ref[...], p, preferred_element_type=F32)
                   for p in _split3(lf_ref[rows, :]))
        decay = jnp.exp(expo)
        outs = []
        for h in range(HGRN_HEADS):
            cols = slice(h * HGRN_DIM, (h + 1) * HGRN_DIM)
            qh, kh, vh = q[:, cols], k[:, cols], v[:, cols]
            att = jnp.where(t_idx == s_idx,
                            jnp.sum(qh * kh, axis=-1, keepdims=True), 0.0)
            for lvl in range(HGRN_LEVELS):
                m = 1 << lvl
                d = decay[lvl * c:(lvl + 1) * c, cols]
                right = (r_idx // m) % 2 == 1
                ql = jnp.where(right, qh * d, 0.0)
                kl = jnp.where(right, 0.0, kh * d)
                a = lax.dot_general(ql.astype(BF16), kl.astype(BF16),
                                    (((1,), (1,)), ((), ())),
                                    preferred_element_type=F32)
                same = (t_idx // (2 * m)) == (s_idx // (2 * m))
                att = att + jnp.where(same, a, 0.0)
            d_in = decay[HGRN_LEVELS * c:(HGRN_LEVELS + 1) * c, cols]
            d_out = decay[(HGRN_LEVELS + 1) * c:(HGRN_LEVELS + 2) * c, cols]
            state_t = state_ref[cols, cols]
            o = _bdot(att, vh) + lax.dot_general(
                (qh * d_in).astype(BF16), state_t.astype(BF16),
                (((1,), (1,)), ((), ())), preferred_element_type=F32)
            upd_t = lax.dot_general(vh.astype(BF16), (kh * d_out).astype(BF16),
                                    (((0,), (0,)), ((), ())),
                                    preferred_element_type=F32)
            state_ref[cols, cols] = state_t * d_in[c - 1:c, :] + upd_t
            outs.append(o)
        o_ref[rows, :] = jnp.concatenate(outs, axis=-1)
        return carry

    @pl.when(jnp.logical_not(strong_decay))
    def _():
        lax.fori_loop(0, n_chunks, factored_chunk, 0, unroll=True)

    @pl.when(strong_decay)
    def _():
        lax.fori_loop(0, n_chunks, general_chunk, 0)

    o = o_ref[...]
    o2 = o * o
    tile_head = lax.broadcasted_iota(jnp.int32, (tile, HGRN_WIDTH), 1) // HGRN_DIM
    inv = jnp.zeros_like(o)
    for h in range(HGRN_HEADS):
        ms = jnp.sum(jnp.where(tile_head == h, o2, 0.0), axis=-1, keepdims=True)
        inv = jnp.where(tile_head == h, lax.rsqrt(ms * (1.0 / HGRN_DIM) + EPS), inv)
    gate = g_ref[...]
    o_ref[...] = o * inv * gain_ref[...] * (gate * jax.nn.sigmoid(gate))


def _hgrn(u, lb_logits, gain, layer, seq_len):
    n = u.shape[0]
    tile = min(TOKEN_TILE, seq_len)
    depth = lb_logits.shape[0]
    col0 = (ATTN_WIDTH + 2 * KV_WIDTH) // HGRN_WIDTH
    col = lambda j: pl.BlockSpec((tile, HGRN_WIDTH), lambda i: (i, col0 + j))
    em = _hgrn_exponent_matrix()
    tile_f32 = pltpu.VMEM((tile, HGRN_WIDTH), F32)
    return pl.pallas_call(
        functools.partial(_hgrn_kernel, seq_len=seq_len, tile=tile, layer=layer),
        grid=(n // tile,),
        in_specs=[col(0), col(1), col(2), col(3),
                  _full((depth, HGRN_WIDTH)), _full((1, HGRN_WIDTH)),
                  _full(em.shape)],
        out_specs=pl.BlockSpec((tile, HGRN_WIDTH), lambda i: (i, 0)),
        out_shape=jax.ShapeDtypeStruct((n, HGRN_WIDTH), F32),
        scratch_shapes=[pltpu.VMEM((HGRN_WIDTH, HGRN_WIDTH), F32),
                        tile_f32, tile_f32, tile_f32, tile_f32],
        compiler_params=_params(),
        name="hgrn2",
    )(u, u, u, u, lb_logits, gain, em)


def _s5_prep_kernel(ar_ref, ai_ref, ldt_ref, br_ref, bi_ref,
                    abr_ref, abi_ref, bbr_ref, bbi_ref):
    ar, ai = ar_ref[...], ai_ref[...]
    dt = jnp.exp(ldt_ref[...])
    mag = jnp.exp(ar * dt)
    abar_re = mag * jnp.cos(ai * dt)
    abar_im = mag * jnp.sin(ai * dt)
    nr, ni = abar_re - 1.0, abar_im
    den = ar * ar + ai * ai
    z_re = (nr * ar + ni * ai) / den
    z_im = (ni * ar - nr * ai) / den
    abr_ref[...] = abar_re
    abi_ref[...] = abar_im
    br, bi = br_ref[...], bi_ref[...]
    bbr_ref[...] = z_re[:, None, :] * br - z_im[:, None, :] * bi
    bbi_ref[...] = z_re[:, None, :] * bi + z_im[:, None, :] * br


def _s5_prep(a_re, a_im, log_dt, b_re, b_im):
    gp = jax.ShapeDtypeStruct((S5_GROUPS, S5_STATE), F32)
    gcp = jax.ShapeDtypeStruct((S5_GROUPS, S5_GROUP_CH, S5_STATE), F32)
    return pl.pallas_call(
        _s5_prep_kernel,
        out_shape=(gp, gp, gcp, gcp),
        name="s5_discretize",
    )(a_re, a_im, log_dt[:, None], jnp.swapaxes(b_re, 1, 2), jnp.swapaxes(b_im, 1, 2))


def _block_diag(blocks):
    g, r, c = blocks.shape
    eye = jnp.eye(g, dtype=blocks.dtype)
    return (blocks[:, :, None, :] * eye[:, None, :, None]).reshape(g * r, g * c)


def _s5_kernel(u_ref, a_ref, b_ref, c_ref, d_ref, gw_ref, gb_ref, o_ref,
               x_ref, state_ref, *, batch, steps):
    @pl.when(pl.program_id(0) == 0)
    def _():
        state_ref[...] = jnp.zeros_like(state_ref)

    u = u_ref[...]
    x_ref[...] = _bdot(u, b_ref[...])
    a_re = jnp.broadcast_to(a_ref[0:1, :], (batch, S5_STATES))
    a_im = jnp.broadcast_to(a_ref[1:2, :], (batch, S5_STATES))

    def step(t, carry):
        x_re, x_im = carry
        rows = pl.ds(pl.multiple_of(t * batch, batch), batch)
        bu_re = x_ref[rows, 0:S5_STATES]
        bu_im = x_ref[rows, S5_STATES:2 * S5_STATES]
        n_re = a_re * x_re - a_im * x_im + bu_re
        n_im = a_re * x_im + a_im * x_re + bu_im
        x_ref[rows, 0:S5_STATES] = n_re
        x_ref[rows, S5_STATES:2 * S5_STATES] = n_im
        return n_re, n_im

    x_re, x_im = lax.fori_loop(
        0, steps, step,
        (state_ref[:, 0:S5_STATES], state_ref[:, S5_STATES:2 * S5_STATES]),
        unroll=True)
    state_ref[:, 0:S5_STATES] = x_re
    state_ref[:, S5_STATES:2 * S5_STATES] = x_im

    y = _bdot(x_ref[...], c_ref[...]) + d_ref[...] * u
    z = jax.nn.gelu(y)
    o_ref[...] = z * jax.nn.sigmoid(_bdot(z, gw_ref[...]) + gb_ref[...])


def _s5(u_tm, abar, b_blk, c_blk, d_skip, glu_w, glu_b, batch, seq_len):
    steps = min(S5_TIME_TILE, seq_len)
    rows = steps * batch
    return pl.pallas_call(
        functools.partial(_s5_kernel, batch=batch, steps=steps),
        grid=(seq_len // steps,),
        in_specs=[
            pl.BlockSpec((rows, S5_WIDTH), lambda i: (i, 0)),
            _full((2, S5_STATES)),
            _full((S5_WIDTH, 2 * S5_STATES)),
            _full((2 * S5_STATES, S5_WIDTH)),
            _full((1, S5_WIDTH)),
            _full((S5_WIDTH, S5_WIDTH)),
            _full((1, S5_WIDTH)),
        ],
        out_specs=pl.BlockSpec((rows, S5_WIDTH), lambda i: (i, 0)),
        out_shape=jax.ShapeDtypeStruct((seq_len * batch, S5_WIDTH), F32),
        scratch_shapes=[pltpu.VMEM((rows, 2 * S5_STATES), F32),
                        pltpu.VMEM((batch, 2 * S5_STATES), F32)],
        compiler_params=_params(),
        name="s5_ssm",
    )(u_tm, abar, b_blk, c_blk, d_skip, glu_w, glu_b)


def _s5_mixer(u, batch, seq_len, a_re, a_im, log_dt, b_re, b_im, c_re, c_im, d_skip,
              glu_w, glu_b):
    abar_re, abar_im, bbar_re, bbar_im = _s5_prep(a_re, a_im, log_dt, b_re, b_im)
    abar = jnp.stack([abar_re.reshape(-1), abar_im.reshape(-1)])
    b_blk = jnp.concatenate([_block_diag(bbar_re), _block_diag(bbar_im)], axis=1)
    c_blk = jnp.concatenate([_block_diag(jnp.swapaxes(c_re, 1, 2)),
                             -_block_diag(jnp.swapaxes(c_im, 1, 2))], axis=0)
    u_c = u[:, IN_PROJ_WIDTH - S5_WIDTH:].reshape(batch, seq_len, S5_WIDTH)
    u_tm = jnp.swapaxes(u_c, 0, 1).reshape(seq_len * batch, S5_WIDTH)
    y_tm = _s5(u_tm, abar, b_blk.astype(BF16), c_blk.astype(BF16),
               d_skip.reshape(1, S5_WIDTH), glu_w.astype(BF16),
               glu_b.reshape(1, S5_WIDTH), batch, seq_len)
    y = jnp.swapaxes(y_tm.reshape(seq_len, batch, S5_WIDTH), 0, 1)
    return y.reshape(batch * seq_len, S5_WIDTH)


def kernel(x, norm_ffn1, ffn1_w_gate, ffn1_w_up, ffn1_w_down, norm_mix, w_in, attn_sinks,
           hgrn_lb_logits, hgrn_norm, s5_a_re, s5_a_im, s5_log_dt, s5_b_re, s5_b_im,
           s5_c_re, s5_c_im, s5_d, s5_glu_w, s5_glu_b, w_out, norm_ffn2, ffn2_w_gate,
           ffn2_w_up, ffn2_w_down, norm_final):
    batch, seq_len, _ = x.shape
    depth = w_in.shape[0]
    h = x.reshape(batch * seq_len, D_MODEL)
    row = lambda v: v.reshape(1, -1)
    final_gain = row(norm_final)
    for layer in range(depth):
        h = _ffn(h, row(norm_ffn1[layer]), ffn1_w_gate[layer].astype(BF16),
                 ffn1_w_up[layer].astype(BF16), ffn1_w_down[layer].astype(BF16),
                 final_gain, False)
        u = _in_proj(h, row(norm_mix[layer]), w_in[layer].astype(BF16))
        y_a = _attention(u, attn_sinks[layer], seq_len)
        y_b = _hgrn(u, hgrn_lb_logits, row(hgrn_norm[layer]), layer, seq_len)
        y_c = _s5_mixer(u, batch, seq_len, s5_a_re[layer], s5_a_im[layer], s5_log_dt[layer],
                        s5_b_re[layer], s5_b_im[layer], s5_c_re[layer], s5_c_im[layer],
                        s5_d[layer], s5_glu_w[layer], s5_glu_b[layer])
        h = _out_proj(h, y_a, y_b, y_c, w_out[layer].astype(BF16))
        h = _ffn(h, row(norm_ffn2[layer]), ffn2_w_gate[layer].astype(BF16),
                 ffn2_w_up[layer].astype(BF16), ffn2_w_down[layer].astype(BF16),
                 final_gain, layer == depth - 1)
    return h.reshape(batch, seq_len, D_MODEL)
```

```python
import functools
import math

import jax
import jax.numpy as jnp
from jax import lax
from jax.experimental import pallas as pl
from jax.experimental.pallas import tpu as pltpu

D_MODEL = 1024
D_FF = 2816
EPS = 1e-6

ATTN_HEADS = 8
ATTN_KV_HEADS = 2
ATTN_GROUP = ATTN_HEADS // ATTN_KV_HEADS
HEAD_DIM = 64
WINDOW = 128
ATTN_WIDTH = ATTN_HEADS * HEAD_DIM
KV_WIDTH = ATTN_KV_HEADS * HEAD_DIM

HGRN_HEADS = 4
HGRN_DIM = 64
HGRN_WIDTH = HGRN_HEADS * HGRN_DIM
HGRN_CHUNK = 64
HGRN_LEVELS = 6

S5_GROUPS = 16
S5_GROUP_CH = 16
S5_STATE = 64
S5_WIDTH = S5_GROUPS * S5_GROUP_CH
S5_STATES = S5_GROUPS * S5_STATE

IN_PROJ_WIDTH = ATTN_WIDTH + 2 * KV_WIDTH + 4 * HGRN_WIDTH + S5_WIDTH
MIX_AB_WIDTH = IN_PROJ_WIDTH - S5_WIDTH

VMEM_LIMIT_BYTES = 56 * 1024 * 1024

TOKEN_TILE = 512
S5_TIME_TILE = 128
MASK_VALUE = -1e30

F32 = jnp.float32
BF16 = jnp.bfloat16


def _rms(x, gain):
    return x * lax.rsqrt(jnp.mean(x * x, axis=-1, keepdims=True) + EPS) * gain


def _bdot(a, b):
    return jnp.dot(a.astype(BF16), b.astype(BF16), preferred_element_type=F32)


def _params(n_axes=1):
    return pltpu.CompilerParams(
        dimension_semantics=("arbitrary",) * n_axes,
        vmem_limit_bytes=VMEM_LIMIT_BYTES,
    )


def _full(shape):
    return pl.BlockSpec(shape, lambda i: (0,) * len(shape))


WEIGHT_LOAD_STEPS = 16


def _token_tile(step):
    return jnp.maximum(step - WEIGHT_LOAD_STEPS, 0)


def _weight_chunk_spec(w, layer):
    _, rows, cols = w.shape
    return pl.BlockSpec(
        (None, rows // WEIGHT_LOAD_STEPS, cols),
        lambda i: (layer, jnp.minimum(i, WEIGHT_LOAD_STEPS - 1), 0))


def _keep_weight_chunk(step, w_ref, w_s):
    chunk = w_ref.shape[0]
    w_s[pl.ds(pl.multiple_of(step * chunk, chunk), chunk), :] = w_ref[...].astype(BF16)


def _time_major_spec(tile, seq_len):
    tiles_per_seq = seq_len // tile
    return pl.BlockSpec(
        (tile, S5_WIDTH),
        lambda i: (_token_tile(i) % tiles_per_seq, _token_tile(i) // tiles_per_seq))


def _ffn_kernel(*refs, mix, final_norm):
    if mix:
        (x_ref, ya_ref, yb_ref, yc_ref, wo_ref, g_ref, wg_ref, wu_ref, wd_ref, gf_ref,
         o_ref, wo_s, wg_s, wu_s, wd_s) = refs
        streamed = [(wo_ref, wo_s), (wg_ref, wg_s), (wu_ref, wu_s), (wd_ref, wd_s)]
    else:
        x_ref, g_ref, wg_ref, wu_ref, wd_ref, gf_ref, o_ref, wg_s, wu_s, wd_s = refs
        streamed = [(wg_ref, wg_s), (wu_ref, wu_s), (wd_ref, wd_s)]
    step = pl.program_id(0)

    @pl.when(step < WEIGHT_LOAD_STEPS)
    def _():
        for w_ref, w_s in streamed:
            _keep_weight_chunk(step, w_ref, w_s)

    @pl.when(step >= WEIGHT_LOAD_STEPS)
    def _():
        x = x_ref[...]
        if mix:
            y = jnp.concatenate([ya_ref[...], yb_ref[...], yc_ref[...]], axis=-1)
            x = x + _bdot(y, wo_s[...])
        h = _rms(x, g_ref[...]).astype(BF16)
        gate = jnp.dot(h, wg_s[...], preferred_element_type=F32)
        up = jnp.dot(h, wu_s[...], preferred_element_type=F32)
        act = (gate * jax.nn.sigmoid(gate) * up).astype(BF16)
        y = x + 0.5 * jnp.dot(act, wd_s[...], preferred_element_type=F32)
        if final_norm:
            y = _rms(y, gf_ref[...])
        o_ref[...] = y


def _ffn(x, mixer_out, gain, w_gate, w_up, w_down, final_gain, final_norm, layer, seq_len):
    n = x.shape[0]
    tm = min(TOKEN_TILE, seq_len)
    row = lambda width: pl.BlockSpec((tm, width), lambda i: (_token_tile(i), 0))
    vec = _full((1, D_MODEL))
    ffn_weights = [w_gate, w_up, w_down]
    if mixer_out is None:
        in_specs, args, streamed = [row(D_MODEL)], [x], ffn_weights
    else:
        y_a, y_b, y_c, w_out = mixer_out
        in_specs = [row(D_MODEL), row(ATTN_WIDTH), row(HGRN_WIDTH),
                    _time_major_spec(tm, seq_len), _weight_chunk_spec(w_out, layer)]
        args = [x, y_a, y_b, y_c, w_out]
        streamed = [w_out] + ffn_weights
    in_specs += [vec] + [_weight_chunk_spec(w, layer) for w in ffn_weights] + [vec]
    return pl.pallas_call(
        functools.partial(_ffn_kernel, mix=mixer_out is not None, final_norm=final_norm),
        grid=(WEIGHT_LOAD_STEPS + n // tm,),
        in_specs=in_specs,
        out_specs=row(D_MODEL),
        out_shape=jax.ShapeDtypeStruct((n, D_MODEL), F32),
        scratch_shapes=[pltpu.VMEM(w.shape[1:], BF16) for w in streamed],
        compiler_params=_params(),
        name="ffn",
    )(*args, gain, *ffn_weights, final_gain)


def _in_proj_kernel(x_ref, g_ref, w_ref, u_ref, uc_ref, w_s):
    step = pl.program_id(0)

    @pl.when(step < WEIGHT_LOAD_STEPS)
    def _():
        _keep_weight_chunk(step, w_ref, w_s)

    @pl.when(step >= WEIGHT_LOAD_STEPS)
    def _():
        u = jnp.dot(_rms(x_ref[...], g_ref[...]).astype(BF16), w_s[...],
                    preferred_element_type=F32)
        u_ref[...] = u[:, :MIX_AB_WIDTH]
        uc_ref[...] = u[:, MIX_AB_WIDTH:]


def _in_proj(x, gain, w_in, layer, batch, seq_len):
    n = x.shape[0]
    tm = min(TOKEN_TILE, seq_len)
    row = lambda width: pl.BlockSpec((tm, width), lambda i: (_token_tile(i), 0))
    return pl.pallas_call(
        _in_proj_kernel,
        grid=(WEIGHT_LOAD_STEPS + n // tm,),
        in_specs=[row(D_MODEL), _full((1, D_MODEL)), _weight_chunk_spec(w_in, layer)],
        out_specs=[row(MIX_AB_WIDTH), _time_major_spec(tm, seq_len)],
        out_shape=[jax.ShapeDtypeStruct((n, MIX_AB_WIDTH), F32),
                   jax.ShapeDtypeStruct((seq_len, batch * S5_WIDTH), F32)],
        scratch_shapes=[pltpu.VMEM(w_in.shape[1:], BF16)],
        compiler_params=_params(),
        name="in_proj",
    )(x, gain, w_in)


def _alibi_slope(head):
    return 2.0 ** (-8.0 * (head + 1.0) / ATTN_HEADS)


def _attn_kernel(sink_ref, q_ref, kv_ref, kvp_ref, o_ref, bias_ref, *, seq_len, tile):
    first = (pl.program_id(0) * tile) % seq_len == 0
    rows = ATTN_GROUP * WINDOW
    c_idx = lax.broadcasted_iota(jnp.int32, (rows, 2 * WINDOW), 1)
    log2e = math.log2(math.e)
    scale = log2e / math.sqrt(HEAD_DIM)

    @pl.when(pl.program_id(0) == 0)
    def _():
        g_idx = lax.broadcasted_iota(jnp.int32, (rows, 1), 0) // WINDOW
        t_idx = lax.broadcasted_iota(jnp.int32, (rows, 2 * WINDOW), 0) % WINDOW
        rel = t_idx + WINDOW - c_idx
        in_win = (rel >= 0) & (rel < WINDOW)
        relf = rel.astype(F32)
        for hk in range(ATTN_KV_HEADS):
            slope = jnp.zeros((rows, 1), F32)
            sink = jnp.zeros((rows, 1), F32)
            for g in range(ATTN_GROUP):
                head = hk * ATTN_GROUP + g
                slope = jnp.where(g_idx == g, _alibi_slope(head) * log2e, slope)
                sink = jnp.where(g_idx == g, sink_ref[head] * log2e, sink)
            bias = jnp.where(in_win, -(slope * relf), MASK_VALUE)
            bias_ref[hk] = jnp.where(c_idx == 0, sink, bias)

    keys = jnp.concatenate([kvp_ref[...], kv_ref[...]], axis=0)
    keys_t = jnp.transpose(keys[:, 0:KV_WIDTH])
    kcol = lax.broadcasted_iota(jnp.int32, (HEAD_DIM, 2 * WINDOW), 1)
    vrow = lax.broadcasted_iota(jnp.int32, (2 * WINDOW, HEAD_DIM), 0)
    no_prev = (c_idx >= 1) & (c_idx < jnp.where(first, WINDOW, 0))

    for blk in range(tile // WINDOW):
        qb = q_ref[blk * WINDOW:(blk + 1) * WINDOW, :] * scale
        outs = []
        for hk in range(ATTN_KV_HEADS):
            kt = keys_t[hk * HEAD_DIM:(hk + 1) * HEAD_DIM, blk * WINDOW:(blk + 2) * WINDOW]
            kt = jnp.where(kcol == 0, 0.0, kt)
            vh = keys[blk * WINDOW:(blk + 2) * WINDOW,
                      KV_WIDTH + hk * HEAD_DIM:KV_WIDTH + (hk + 1) * HEAD_DIM]
            vh = jnp.where(vrow == 0, 0.0, vh)
            heads = [hk * ATTN_GROUP + g for g in range(ATTN_GROUP)]
            qs = jnp.concatenate(
                [qb[:, h * HEAD_DIM:(h + 1) * HEAD_DIM] for h in heads], axis=0)
            s = _bdot(qs, kt) + bias_ref[hk]
            if blk == 0:
                s = jnp.where(no_prev, MASK_VALUE, s)
            p = jnp.exp2(s - jnp.max(s, axis=-1, keepdims=True))
            denom = jnp.sum(p, axis=-1, keepdims=True)
            o = _bdot(p, vh) * (1.0 / denom)
            outs.extend(o[g * WINDOW:(g + 1) * WINDOW, :] for g in range(ATTN_GROUP))
        o_ref[blk * WINDOW:(blk + 1) * WINDOW, :] = jnp.concatenate(outs, axis=-1)


def _attention(u, sinks, seq_len):
    n = u.shape[0]
    tile = min(TOKEN_TILE, seq_len)
    blocks_per_tile = tile // WINDOW
    kv_col = ATTN_WIDTH // (2 * KV_WIDTH)
    return pl.pallas_call(
        functools.partial(_attn_kernel, seq_len=seq_len, tile=tile),
        grid=(n // tile,),
        in_specs=[
            pl.BlockSpec(memory_space=pltpu.SMEM),
            pl.BlockSpec((tile, ATTN_WIDTH), lambda i: (i, 0)),
            pl.BlockSpec((tile, 2 * KV_WIDTH), lambda i: (i, kv_col)),
            pl.BlockSpec((WINDOW, 2 * KV_WIDTH),
                         lambda i: (jnp.maximum(i * blocks_per_tile - 1, 0), kv_col)),
        ],
        out_specs=pl.BlockSpec((tile, ATTN_WIDTH), lambda i: (i, 0)),
        out_shape=jax.ShapeDtypeStruct((n, ATTN_WIDTH), F32),
        scratch_shapes=[pltpu.VMEM((ATTN_KV_HEADS, ATTN_GROUP * WINDOW, 2 * WINDOW), F32)],
        compiler_params=_params(),
        name="swa_attention",
    )(sinks, u, u, u)


HGRN_FACTORED_DECAY_LIMIT = 80.0


def _hgrn_exponent_matrix():
    c = HGRN_CHUNK
    r = jnp.arange(c)[:, None]
    j = jnp.arange(c)[None, :]
    blocks = []
    for lvl in range(HGRN_LEVELS):
        m = 1 << lvl
        start = (r // (2 * m)) * (2 * m)
        right = (r // m) % 2 == 1
        in_right = right & (j >= start + m) & (j <= r)
        in_left = (~right) & (j > r) & (j < start + m)
        blocks.append(in_right | in_left)
    blocks.append(j <= r)
    blocks.append(j > r)
    return jnp.concatenate(blocks, axis=0).astype(BF16)


def _split3(x):
    hi = x.astype(BF16)
    r1 = x - hi.astype(F32)
    mid = r1.astype(BF16)
    lo = (r1 - mid.astype(F32)).astype(BF16)
    return hi, mid, lo


def _hgrn_kernel(q_ref, f_ref, i_ref, g_ref, lbl_ref, gain_ref, em_ref, o_ref,
                 state_ref, qs_ref, ks_ref, lf_ref, bc_ref, *, seq_len, tile, layer):
    c = HGRN_CHUNK
    first = (pl.program_id(0) * tile) % seq_len == 0

    @pl.when(first)
    def _():
        state_ref[...] = jnp.zeros_like(state_ref)

    logits = lbl_ref[...]
    e = jnp.exp(logits - jnp.max(logits, axis=0, keepdims=True))
    sm = e / jnp.sum(e, axis=0, keepdims=True)
    lb = jnp.zeros((1, HGRN_WIDTH), F32)
    for l in range(1, layer + 1):
        lb = lb + sm[l:l + 1, :]
    log_lb = jnp.log(lb)
    log_1m_lb = jnp.log1p(-lb)

    r_idx = lax.broadcasted_iota(jnp.int32, (c, 1), 0)
    t_idx = lax.broadcasted_iota(jnp.int32, (c, c), 0)
    s_idx = lax.broadcasted_iota(jnp.int32, (c, c), 1)
    lane_head = lax.broadcasted_iota(jnp.int32, (c, HGRN_WIDTH), 1) // HGRN_DIM
    n_chunks = tile // c

    def chunk_rows(ci):
        return pl.ds(pl.multiple_of(ci * c, c), c)

    def prepare(ci, carry):
        rows = chunk_rows(ci)
        z = f_ref[rows, :]
        q = q_ref[rows, :]
        qs_ref[rows, :] = q * jax.nn.sigmoid(q)
        log_sig = jnp.minimum(z, 0.0) - jnp.log1p(jnp.exp(-jnp.abs(z)))
        bterm = log_1m_lb + log_sig
        lf = jnp.maximum(log_lb, bterm) + jnp.log1p(jnp.exp(-jnp.abs(log_lb - bterm)))
        ks_ref[rows, :] = (1.0 - lb) * jax.nn.sigmoid(-z)
        lf_ref[rows, :] = lf
        tri = em_ref[HGRN_LEVELS * c:(HGRN_LEVELS + 1) * c, :]
        bc_ref[rows, :] = sum(jnp.dot(tri, p, preferred_element_type=F32)
                              for p in _split3(lf))
        return carry

    lax.fori_loop(0, n_chunks, prepare, 0, unroll=True)
    strong_decay = jnp.min(bc_ref[...]) < -HGRN_FACTORED_DECAY_LIMIT

    def stack_heads(x):
        return jnp.concatenate(
            [jnp.where(lane_head == h, x, 0.0) for h in range(HGRN_HEADS)],
            axis=0).astype(BF16)

    def factored_chunk(ci, carry):
        rows = chunk_rows(ci)
        q, k, v, bc = qs_ref[rows, :], ks_ref[rows, :], i_ref[rows, :], bc_ref[rows, :]
        b_last = bc[c - 1:c, :]
        qd = (q * jnp.exp(bc)).astype(BF16)
        kb = stack_heads(k * jnp.exp(-bc))
        vb = stack_heads(v)
        att = lax.dot_general(qd, kb, (((1,), (1,)), ((), ())),
                              preferred_element_type=F32)
        col_s = lax.broadcasted_iota(jnp.int32, (c, HGRN_WIDTH), 1) % c
        row_t = lax.broadcasted_iota(jnp.int32, (c, HGRN_WIDTH), 0)
        att = jnp.where(col_s <= row_t, att, 0.0)
        state_t = state_ref[...]
        o = jnp.dot(att.astype(BF16), vb, preferred_element_type=F32)
        o = o + lax.dot_general(qd, state_t.astype(BF16), (((1,), (1,)), ((), ())),
                                preferred_element_type=F32)
        upd = lax.dot_general(v.astype(BF16), (k * jnp.exp(b_last - bc)).astype(BF16),
                              (((0,), (0,)), ((), ())), preferred_element_type=F32)
        rh = lax.broadcasted_iota(jnp.int32, (HGRN_WIDTH, HGRN_WIDTH), 0) // HGRN_DIM
        ch = lax.broadcasted_iota(jnp.int32, (HGRN_WIDTH, HGRN_WIDTH), 1) // HGRN_DIM
        state_ref[...] = state_t * jnp.exp(b_last) + jnp.where(rh == ch, upd, 0.0)
        o_ref[rows, :] = o
        return carry

    def general_chunk(ci, carry):
        rows = chunk_rows(ci)
        q, k, v = qs_ref[rows, :], ks_ref[rows, :], i_ref[rows, :]
        expo = sum(jnp.dot(em_ref[...], p, preferred_element_type=F32)
                   for p in _split3(lf_ref[rows, :]))
        decay = jnp.exp(expo)
        outs = []
        for h in range(HGRN_HEADS):
            cols = slice(h * HGRN_DIM, (h + 1) * HGRN_DIM)
            qh, kh, vh = q[:, cols], k[:, cols], v[:, cols]
            att = jnp.where(t_idx == s_idx,
                            jnp.sum(qh * kh, axis=-1, keepdims=True), 0.0)
            for lvl in range(HGRN_LEVELS):
                m = 1 << lvl
                d = decay[lvl * c:(lvl + 1) * c, cols]
                right = (r_idx // m) % 2 == 1
                ql = jnp.where(right, qh * d, 0.0)
                kl = jnp.where(right, 0.0, kh * d)
                a = lax.dot_general(ql.astype(BF16), kl.astype(BF16),
                                    (((1,), (1,)), ((), ())),
                                    preferred_element_type=F32)
                same = (t_idx // (2 * m)) == (s_idx // (2 * m))
                att = att + jnp.where(same, a, 0.0)
            d_in = decay[HGRN_LEVELS * c:(HGRN_LEVELS + 1) * c, cols]
            d_out = decay[(HGRN_LEVELS + 1) * c:(HGRN_LEVELS + 2) * c, cols]
            state_t = state_ref[cols, cols]
            o = _bdot(att, vh) + lax.dot_general(
                (qh * d_in).astype(BF16), state_t.astype(BF16),
                (((1,), (1,)), ((), ())), preferred_element_type=F32)
            upd_t = lax.dot_general(vh.astype(BF16), (kh * d_out).astype(BF16),
                                    (((0,), (0,)), ((), ())),
                                    preferred_element_type=F32)
            state_ref[cols, cols] = state_t * d_in[c - 1:c, :] + upd_t
            outs.append(o)
        o_ref[rows, :] = jnp.concatenate(outs, axis=-1)
        return carry

    @pl.when(jnp.logical_not(strong_decay))
    def _():
        lax.fori_loop(0, n_chunks, factored_chunk, 0, unroll=True)

    @pl.when(strong_decay)
    def _():
        lax.fori_loop(0, n_chunks, general_chunk, 0)

    o = o_ref[...]
    o2 = o * o
    tile_head = lax.broadcasted_iota(jnp.int32, (tile, HGRN_WIDTH), 1) // HGRN_DIM
    inv = jnp.zeros_like(o)
    for h in range(HGRN_HEADS):
        ms = jnp.sum(jnp.where(tile_head == h, o2, 0.0), axis=-1, keepdims=True)
        inv = jnp.where(tile_head == h, lax.rsqrt(ms * (1.0 / HGRN_DIM) + EPS), inv)
    gate = g_ref[...]
    o_ref[...] = o * inv * gain_ref[...] * (gate * jax.nn.sigmoid(gate))


def _hgrn(u, lb_logits, gain, layer, seq_len):
    n = u.shape[0]
    tile = min(TOKEN_TILE, seq_len)
    depth = lb_logits.shape[0]
    col0 = (ATTN_WIDTH + 2 * KV_WIDTH) // HGRN_WIDTH
    col = lambda j: pl.BlockSpec((tile, HGRN_WIDTH), lambda i: (i, col0 + j))
    em = _hgrn_exponent_matrix()
    tile_f32 = pltpu.VMEM((tile, HGRN_WIDTH), F32)
    return pl.pallas_call(
        functools.partial(_hgrn_kernel, seq_len=seq_len, tile=tile, layer=layer),
        grid=(n // tile,),
        in_specs=[col(0), col(1), col(2), col(3),
                  _full((depth, HGRN_WIDTH)), _full((1, HGRN_WIDTH)),
                  _full(em.shape)],
        out_specs=pl.BlockSpec((tile, HGRN_WIDTH), lambda i: (i, 0)),
        out_shape=jax.ShapeDtypeStruct((n, HGRN_WIDTH), F32),
        scratch_shapes=[pltpu.VMEM((HGRN_WIDTH, HGRN_WIDTH), F32),
                        tile_f32, tile_f32, tile_f32, tile_f32],
        compiler_params=_params(),
        name="hgrn2",
    )(u, u, u, u, lb_logits, gain, em)


def _s5_prep_kernel(ar_ref, ai_ref, ldt_ref, br_ref, bi_ref,
                    abr_ref, abi_ref, bbr_ref, bbi_ref):
    ar, ai = ar_ref[...], ai_ref[...]
    dt = jnp.exp(ldt_ref[...])
    mag = jnp.exp(ar * dt)
    abar_re = mag * jnp.cos(ai * dt)
    abar_im = mag * jnp.sin(ai * dt)
    nr, ni = abar_re - 1.0, abar_im
    den = ar * ar + ai * ai
    z_re = (nr * ar + ni * ai) / den
    z_im = (ni * ar - nr * ai) / den
    abr_ref[...] = abar_re
    abi_ref[...] = abar_im
    br, bi = br_ref[...], bi_ref[...]
    bbr_ref[...] = z_re[:, None, :] * br - z_im[:, None, :] * bi
    bbi_ref[...] = z_re[:, None, :] * bi + z_im[:, None, :] * br


def _s5_prep(a_re, a_im, log_dt, b_re, b_im):
    gp = jax.ShapeDtypeStruct((S5_GROUPS, S5_STATE), F32)
    gcp = jax.ShapeDtypeStruct((S5_GROUPS, S5_GROUP_CH, S5_STATE), F32)
    return pl.pallas_call(
        _s5_prep_kernel,
        out_shape=(gp, gp, gcp, gcp),
        name="s5_discretize",
    )(a_re, a_im, log_dt[:, None], jnp.swapaxes(b_re, 1, 2), jnp.swapaxes(b_im, 1, 2))


def _block_diag(blocks):
    g, r, c = blocks.shape
    eye = jnp.eye(g, dtype=blocks.dtype)
    return (blocks[:, :, None, :] * eye[:, None, :, None]).reshape(g * r, g * c)


def _s5_kernel(u_ref, a_ref, b_ref, c_ref, d_ref, gw_ref, gb_ref, o_ref,
               x_ref, state_ref, *, batch, steps):
    @pl.when(pl.program_id(0) == 0)
    def _():
        state_ref[...] = jnp.zeros_like(state_ref)

    u = u_ref[...]
    x_ref[...] = _bdot(u, b_ref[...])
    a_re = jnp.broadcast_to(a_ref[0:1, :], (batch, S5_STATES))
    a_im = jnp.broadcast_to(a_ref[1:2, :], (batch, S5_STATES))

    def step(t, carry):
        x_re, x_im = carry
        rows = pl.ds(pl.multiple_of(t * batch, batch), batch)
        bu_re = x_ref[rows, 0:S5_STATES]
        bu_im = x_ref[rows, S5_STATES:2 * S5_STATES]
        n_re = a_re * x_re - a_im * x_im + bu_re
        n_im = a_re * x_im + a_im * x_re + bu_im
        x_ref[rows, 0:S5_STATES] = n_re
        x_ref[rows, S5_STATES:2 * S5_STATES] = n_im
        return n_re, n_im

    x_re, x_im = lax.fori_loop(
        0, steps, step,
        (state_ref[:, 0:S5_STATES], state_ref[:, S5_STATES:2 * S5_STATES]),
        unroll=True)
    state_ref[:, 0:S5_STATES] = x_re
    state_ref[:, S5_STATES:2 * S5_STATES] = x_im

    y = _bdot(x_ref[...], c_ref[...]) + d_ref[...] * u
    z = jax.nn.gelu(y)
    o_ref[...] = z * jax.nn.sigmoid(_bdot(z, gw_ref[...]) + gb_ref[...])


def _s5(u_tm, abar, b_blk, c_blk, d_skip, glu_w, glu_b, batch, seq_len):
    steps = min(S5_TIME_TILE, seq_len)
    rows = steps * batch
    return pl.pallas_call(
        functools.partial(_s5_kernel, batch=batch, steps=steps),
        grid=(seq_len // steps,),
        in_specs=[
            pl.BlockSpec((rows, S5_WIDTH), lambda i: (i, 0)),
            _full((2, S5_STATES)),
            _full((S5_WIDTH, 2 * S5_STATES)),
            _full((2 * S5_STATES, S5_WIDTH)),
            _full((1, S5_WIDTH)),
            _full((S5_WIDTH, S5_WIDTH)),
            _full((1, S5_WIDTH)),
        ],
        out_specs=pl.BlockSpec((rows, S5_WIDTH), lambda i: (i, 0)),
        out_shape=jax.ShapeDtypeStruct((seq_len * batch, S5_WIDTH), F32),
        scratch_shapes=[pltpu.VMEM((rows, 2 * S5_STATES), F32),
                        pltpu.VMEM((batch, 2 * S5_STATES), F32)],
        compiler_params=_params(),
        name="s5_ssm",
    )(u_tm, abar, b_blk, c_blk, d_skip, glu_w, glu_b)


def _s5_mixer(u_c, batch, seq_len, a_re, a_im, log_dt, b_re, b_im, c_re, c_im, d_skip,
              glu_w, glu_b):
    abar_re, abar_im, bbar_re, bbar_im = _s5_prep(a_re, a_im, log_dt, b_re, b_im)
    abar = jnp.stack([abar_re.reshape(-1), abar_im.reshape(-1)])
    b_blk = jnp.concatenate([_block_diag(bbar_re), _block_diag(bbar_im)], axis=1)
    c_blk = jnp.concatenate([_block_diag(jnp.swapaxes(c_re, 1, 2)),
                             -_block_diag(jnp.swapaxes(c_im, 1, 2))], axis=0)
    y_tm = _s5(u_c.reshape(seq_len * batch, S5_WIDTH), abar, b_blk.astype(BF16),
               c_blk.astype(BF16), d_skip.reshape(1, S5_WIDTH), glu_w.astype(BF16),
               glu_b.reshape(1, S5_WIDTH), batch, seq_len)
    return y_tm.reshape(seq_len, batch * S5_WIDTH)


def kernel(x, norm_ffn1, ffn1_w_gate, ffn1_w_up, ffn1_w_down, norm_mix, w_in, attn_sinks,
           hgrn_lb_logits, hgrn_norm, s5_a_re, s5_a_im, s5_log_dt, s5_b_re, s5_b_im,
           s5_c_re, s5_c_im, s5_d, s5_glu_w, s5_glu_b, w_out, norm_ffn2, ffn2_w_gate,
           ffn2_w_up, ffn2_w_down, norm_final):
    batch, seq_len, _ = x.shape
    depth = w_in.shape[0]
    h = x.reshape(batch * seq_len, D_MODEL)
    row = lambda v: v.reshape(1, -1)
    final_gain = row(norm_final)
    for layer in range(depth):
        h = _ffn(h, None, row(norm_ffn1[layer]), ffn1_w_gate, ffn1_w_up, ffn1_w_down,
                 final_gain, False, layer, seq_len)
        u, u_c = _in_proj(h, row(norm_mix[layer]), w_in, layer, batch, seq_len)
        y_a = _attention(u, attn_sinks[layer], seq_len)
        y_b = _hgrn(u, hgrn_lb_logits, row(hgrn_norm[layer]), layer, seq_len)
        y_c = _s5_mixer(u_c, batch, seq_len, s5_a_re[layer], s5_a_im[layer],
                        s5_log_dt[layer], s5_b_re[layer], s5_b_im[layer], s5_c_re[layer],
                        s5_c_im[layer], s5_d[layer], s5_glu_w[layer], s5_glu_b[layer])
        h = _ffn(h, (y_a, y_b, y_c, w_out), row(norm_ffn2[layer]), ffn2_w_gate, ffn2_w_up,
                 ffn2_w_down, final_gain, layer == depth - 1, layer, seq_len)
    return h.reshape(batch, seq_len, D_MODEL)
```

```python
import functools
import math

import jax
import jax.numpy as jnp
from jax import lax
from jax.experimental import pallas as pl
from jax.experimental.pallas import tpu as pltpu

D_MODEL = 1024
D_FF = 2816
EPS = 1e-6

ATTN_HEADS = 8
ATTN_KV_HEADS = 2
ATTN_GROUP = ATTN_HEADS // ATTN_KV_HEADS
HEAD_DIM = 64
WINDOW = 128
ATTN_WIDTH = ATTN_HEADS * HEAD_DIM
KV_WIDTH = ATTN_KV_HEADS * HEAD_DIM

HGRN_HEADS = 4
HGRN_DIM = 64
HGRN_WIDTH = HGRN_HEADS * HGRN_DIM
HGRN_CHUNK = 64
HGRN_LEVELS = 6

S5_GROUPS = 16
S5_GROUP_CH = 16
S5_STATE = 64
S5_WIDTH = S5_GROUPS * S5_GROUP_CH
S5_STATES = S5_GROUPS * S5_STATE

IN_PROJ_WIDTH = ATTN_WIDTH + 2 * KV_WIDTH + 4 * HGRN_WIDTH + S5_WIDTH

VMEM_LIMIT_BYTES = 56 * 1024 * 1024

TOKEN_TILE = 512
S5_TIME_TILE = 128
MASK_VALUE = -1e30

F32 = jnp.float32
BF16 = jnp.bfloat16


def _rms(x, gain):
    return x * lax.rsqrt(jnp.mean(x * x, axis=-1, keepdims=True) + EPS) * gain


def _bdot(a, b):
    return jnp.dot(a.astype(BF16), b.astype(BF16), preferred_element_type=F32)


def _params(n_axes=1):
    return pltpu.CompilerParams(
        dimension_semantics=("arbitrary",) * n_axes,
        vmem_limit_bytes=VMEM_LIMIT_BYTES,
    )


def _full(shape):
    return pl.BlockSpec(shape, lambda i: (0,) * len(shape))


WEIGHT_LOAD_STEPS = 16


def _token_tile(step):
    return jnp.maximum(step - WEIGHT_LOAD_STEPS, 0)


def _weight_chunk_spec(w, layer):
    _, rows, cols = w.shape
    return pl.BlockSpec(
        (None, rows // WEIGHT_LOAD_STEPS, cols),
        lambda i: (layer, jnp.minimum(i, WEIGHT_LOAD_STEPS - 1), 0))


def _keep_weight_chunk(step, w_ref, w_s):
    chunk = w_ref.shape[0]
    w_s[pl.ds(pl.multiple_of(step * chunk, chunk), chunk), :] = w_ref[...].astype(BF16)


def _ffn_kernel(*refs, mix, final_norm):
    if mix:
        (x_ref, ya_ref, yb_ref, yc_ref, wo_ref, g_ref, wg_ref, wu_ref, wd_ref, gf_ref,
         o_ref, wo_s, wg_s, wu_s, wd_s) = refs
        streamed = [(wo_ref, wo_s), (wg_ref, wg_s), (wu_ref, wu_s), (wd_ref, wd_s)]
    else:
        x_ref, g_ref, wg_ref, wu_ref, wd_ref, gf_ref, o_ref, wg_s, wu_s, wd_s = refs
        streamed = [(wg_ref, wg_s), (wu_ref, wu_s), (wd_ref, wd_s)]
    step = pl.program_id(0)

    @pl.when(step < WEIGHT_LOAD_STEPS)
    def _():
        for w_ref, w_s in streamed:
            _keep_weight_chunk(step, w_ref, w_s)

    @pl.when(step >= WEIGHT_LOAD_STEPS)
    def _():
        x = x_ref[...]
        if mix:
            y = jnp.concatenate([ya_ref[...], yb_ref[...], yc_ref[...]], axis=-1)
            x = x + _bdot(y, wo_s[...])
        h = _rms(x, g_ref[...]).astype(BF16)
        gate = jnp.dot(h, wg_s[...], preferred_element_type=F32)
        up = jnp.dot(h, wu_s[...], preferred_element_type=F32)
        act = (gate * jax.nn.sigmoid(gate) * up).astype(BF16)
        y = x + 0.5 * jnp.dot(act, wd_s[...], preferred_element_type=F32)
        if final_norm:
            y = _rms(y, gf_ref[...])
        o_ref[...] = y


def _ffn(x, mixer_out, gain, w_gate, w_up, w_down, final_gain, final_norm, layer, seq_len):
    n = x.shape[0]
    tm = min(TOKEN_TILE, seq_len)
    row = lambda width: pl.BlockSpec((tm, width), lambda i: (_token_tile(i), 0))
    vec = _full((1, D_MODEL))
    ffn_weights = [w_gate, w_up, w_down]
    if mixer_out is None:
        in_specs, args, streamed = [row(D_MODEL)], [x], ffn_weights
    else:
        y_a, y_b, y_c, w_out = mixer_out
        in_specs = [row(D_MODEL), row(ATTN_WIDTH), row(HGRN_WIDTH),
                    row(S5_WIDTH), _weight_chunk_spec(w_out, layer)]
        args = [x, y_a, y_b, y_c, w_out]
        streamed = [w_out] + ffn_weights
    in_specs += [vec] + [_weight_chunk_spec(w, layer) for w in ffn_weights] + [vec]
    return pl.pallas_call(
        functools.partial(_ffn_kernel, mix=mixer_out is not None, final_norm=final_norm),
        grid=(WEIGHT_LOAD_STEPS + n // tm,),
        in_specs=in_specs,
        out_specs=row(D_MODEL),
        out_shape=jax.ShapeDtypeStruct((n, D_MODEL), F32),
        scratch_shapes=[pltpu.VMEM(w.shape[1:], BF16) for w in streamed],
        compiler_params=_params(),
        name="ffn",
    )(*args, gain, *ffn_weights, final_gain)


def _in_proj_kernel(x_ref, g_ref, w_ref, u_ref, w_s):
    step = pl.program_id(0)

    @pl.when(step < WEIGHT_LOAD_STEPS)
    def _():
        _keep_weight_chunk(step, w_ref, w_s)

    @pl.when(step >= WEIGHT_LOAD_STEPS)
    def _():
        u_ref[...] = jnp.dot(_rms(x_ref[...], g_ref[...]).astype(BF16), w_s[...],
                             preferred_element_type=F32)


def _in_proj(x, gain, w_in, layer, seq_len):
    n = x.shape[0]
    tm = min(TOKEN_TILE, seq_len)
    row = lambda width: pl.BlockSpec((tm, width), lambda i: (_token_tile(i), 0))
    return pl.pallas_call(
        _in_proj_kernel,
        grid=(WEIGHT_LOAD_STEPS + n // tm,),
        in_specs=[row(D_MODEL), _full((1, D_MODEL)), _weight_chunk_spec(w_in, layer)],
        out_specs=row(IN_PROJ_WIDTH),
        out_shape=jax.ShapeDtypeStruct((n, IN_PROJ_WIDTH), F32),
        scratch_shapes=[pltpu.VMEM(w_in.shape[1:], BF16)],
        compiler_params=_params(),
        name="in_proj",
    )(x, gain, w_in)


def _alibi_slope(head):
    return 2.0 ** (-8.0 * (head + 1.0) / ATTN_HEADS)


def _attn_kernel(sink_ref, q_ref, kv_ref, kvp_ref, o_ref, bias_ref, *, seq_len, tile):
    first = (pl.program_id(0) * tile) % seq_len == 0
    cols = ATTN_GROUP * WINDOW
    c_idx = lax.broadcasted_iota(jnp.int32, (2 * WINDOW, cols), 0)
    log2e = math.log2(math.e)
    scale = log2e / math.sqrt(HEAD_DIM)

    @pl.when(pl.program_id(0) == 0)
    def _():
        g_idx = lax.broadcasted_iota(jnp.int32, (1, cols), 1) // WINDOW
        t_idx = lax.broadcasted_iota(jnp.int32, (2 * WINDOW, cols), 1) % WINDOW
        rel = t_idx + WINDOW - c_idx
        in_win = (rel >= 0) & (rel < WINDOW)
        relf = rel.astype(F32)
        for hk in range(ATTN_KV_HEADS):
            slope = jnp.zeros((1, cols), F32)
            sink = jnp.zeros((1, cols), F32)
            for g in range(ATTN_GROUP):
                head = hk * ATTN_GROUP + g
                slope = jnp.where(g_idx == g, _alibi_slope(head) * log2e, slope)
                sink = jnp.where(g_idx == g, sink_ref[head] * log2e, sink)
            bias = jnp.where(in_win, -(slope * relf), MASK_VALUE)
            bias_ref[hk] = jnp.where(c_idx == 0, sink, bias)

    keys = jnp.concatenate([kvp_ref[...], kv_ref[...]], axis=0)
    k_all = keys[:, 0:KV_WIDTH]
    v_t = jnp.transpose(keys[:, KV_WIDTH:2 * KV_WIDTH])
    q_t = jnp.transpose(q_ref[...] * scale).astype(BF16)
    krow = lax.broadcasted_iota(jnp.int32, (2 * WINDOW, HEAD_DIM), 0)
    vcol = lax.broadcasted_iota(jnp.int32, (HEAD_DIM, 2 * WINDOW), 1)
    ones_rows = jnp.ones((8, 2 * WINDOW), F32)
    no_prev = (c_idx >= 1) & (c_idx < jnp.where(first, WINDOW, 0))

    n_blocks = tile // WINDOW
    out_t = [[None] * n_blocks for _ in range(ATTN_HEADS)]
    units = [(blk, hk) for blk in range(n_blocks) for hk in range(ATTN_KV_HEADS)]

    def scores(blk, hk):
        slots = slice(blk * WINDOW, (blk + 2) * WINDOW)
        dims = slice(hk * HEAD_DIM, (hk + 1) * HEAD_DIM)
        kh = jnp.where(krow == 0, 0.0, k_all[slots, dims]).astype(BF16)
        qs = jnp.concatenate(
            [q_t[h * HEAD_DIM:(h + 1) * HEAD_DIM, blk * WINDOW:(blk + 1) * WINDOW]
             for h in range(hk * ATTN_GROUP, (hk + 1) * ATTN_GROUP)], axis=1)
        s = jnp.dot(kh, qs, preferred_element_type=F32) + bias_ref[hk]
        if blk == 0:
            s = jnp.where(no_prev, MASK_VALUE, s)
        return s

    def weighted_values(blk, hk, s):
        slots = slice(blk * WINDOW, (blk + 2) * WINDOW)
        dims = slice(hk * HEAD_DIM, (hk + 1) * HEAD_DIM)
        vh = jnp.where(vcol == 0, 0.0, v_t[dims, slots])
        vh = jnp.concatenate([vh, ones_rows], axis=0).astype(BF16)
        p = jnp.exp2(s - jnp.max(s, axis=0, keepdims=True)).astype(BF16)
        o = jnp.dot(vh, p, preferred_element_type=F32)
        o = o[:HEAD_DIM] * (1.0 / o[HEAD_DIM:HEAD_DIM + 1])
        for g in range(ATTN_GROUP):
            out_t[hk * ATTN_GROUP + g][blk] = o[:, g * WINDOW:(g + 1) * WINDOW]

    s = scores(*units[0])
    for i, unit in enumerate(units):
        s_next = scores(*units[i + 1]) if i + 1 < len(units) else None
        weighted_values(*unit, s)
        s = s_next
    out_t = jnp.concatenate([jnp.concatenate(blocks, axis=1) for blocks in out_t], axis=0)
    o_ref[...] = jnp.transpose(out_t)


def _attention(u, sinks, seq_len):
    n = u.shape[0]
    tile = min(TOKEN_TILE, seq_len)
    blocks_per_tile = tile // WINDOW
    kv_col = ATTN_WIDTH // (2 * KV_WIDTH)
    return pl.pallas_call(
        functools.partial(_attn_kernel, seq_len=seq_len, tile=tile),
        grid=(n // tile,),
        in_specs=[
            pl.BlockSpec(memory_space=pltpu.SMEM),
            pl.BlockSpec((tile, ATTN_WIDTH), lambda i: (i, 0)),
            pl.BlockSpec((tile, 2 * KV_WIDTH), lambda i: (i, kv_col)),
            pl.BlockSpec((WINDOW, 2 * KV_WIDTH),
                         lambda i: (jnp.maximum(i * blocks_per_tile - 1, 0), kv_col)),
        ],
        out_specs=pl.BlockSpec((tile, ATTN_WIDTH), lambda i: (i, 0)),
        out_shape=jax.ShapeDtypeStruct((n, ATTN_WIDTH), F32),
        scratch_shapes=[pltpu.VMEM((ATTN_KV_HEADS, 2 * WINDOW, ATTN_GROUP * WINDOW), F32)],
        compiler_params=_params(),
        name="swa_attention",
    )(sinks, u, u, u)


HGRN_FACTORED_DECAY_LIMIT = 80.0


def _hgrn_exponent_matrix():
    c = HGRN_CHUNK
    r = jnp.arange(c)[:, None]
    j = jnp.arange(c)[None, :]
    blocks = []
    for lvl in range(HGRN_LEVELS):
        m = 1 << lvl
        start = (r // (2 * m)) * (2 * m)
        right = (r // m) % 2 == 1
        in_right = right & (j >= start + m) & (j <= r)
        in_left = (~right) & (j > r) & (j < start + m)
        blocks.append(in_right | in_left)
    blocks.append(j <= r)
    blocks.append(j > r)
    return jnp.concatenate(blocks, axis=0).astype(BF16)


def _split3(x):
    hi = x.astype(BF16)
    r1 = x - hi.astype(F32)
    mid = r1.astype(BF16)
    lo = (r1 - mid.astype(F32)).astype(BF16)
    return hi, mid, lo


def _hgrn_kernel(q_ref, f_ref, i_ref, g_ref, lbl_ref, gain_ref, em_ref, hm_ref, o_ref,
                 state_ref, qs_ref, ks_ref, lf_ref, bc_ref, *, seq_len, tile, layer):
    c = HGRN_CHUNK
    first = (pl.program_id(0) * tile) % seq_len == 0

    @pl.when(first)
    def _():
        state_ref[...] = jnp.zeros_like(state_ref)

    logits = lbl_ref[...]
    e = jnp.exp(logits - jnp.max(logits, axis=0, keepdims=True))
    sm = e / jnp.sum(e, axis=0, keepdims=True)
    lb = jnp.zeros((1, HGRN_WIDTH), F32)
    for l in range(1, layer + 1):
        lb = lb + sm[l:l + 1, :]

    r_idx = lax.broadcasted_iota(jnp.int32, (c, 1), 0)
    t_idx = lax.broadcasted_iota(jnp.int32, (c, c), 0)
    s_idx = lax.broadcasted_iota(jnp.int32, (c, c), 1)
    lane_head = lax.broadcasted_iota(jnp.int32, (c, HGRN_WIDTH), 1) // HGRN_DIM
    n_chunks = tile // c

    def chunk_rows(ci):
        return pl.ds(pl.multiple_of(ci * c, c), c)

    def prepare(ci, carry):
        rows = chunk_rows(ci)
        z = f_ref[rows, :]
        q = q_ref[rows, :]
        qs_ref[rows, :] = q * jax.nn.sigmoid(q)
        e = jnp.exp(-jnp.abs(z))
        r = 1.0 + e
        inv = 1.0 / r
        pos = z >= 0.0
        sig = jnp.where(pos, 1.0, e) * inv
        ks_ref[rows, :] = (1.0 - lb) * (jnp.where(pos, e, 1.0) * inv)
        lf = jnp.where(lb > 0.0, jnp.log(lb + (1.0 - lb) * sig),
                       jnp.minimum(z, 0.0) - jnp.log(r))
        lf_ref[rows, :] = lf
        tri = em_ref[HGRN_LEVELS * c:(HGRN_LEVELS + 1) * c, :]
        bc_ref[rows, :] = sum(jnp.dot(tri, p, preferred_element_type=F32)
                              for p in _split3(lf))
        return carry

    lax.fori_loop(0, n_chunks, prepare, 0, unroll=True)
    strong_decay = jnp.min(bc_ref[...]) < -HGRN_FACTORED_DECAY_LIMIT

    def stack_heads(x):
        return jnp.concatenate(
            [jnp.where(lane_head == h, x, 0.0) for h in range(HGRN_HEADS)],
            axis=0).astype(BF16)

    def chunk_scores(ci):
        rows = pl.ds(ci * c, c)
        bc = bc_ref[rows, :]
        qd = (qs_ref[rows, :] * jnp.exp(bc)).astype(BF16)
        kb = stack_heads(ks_ref[rows, :] * jnp.exp(-bc))
        att = lax.dot_general(qd, kb, (((1,), (1,)), ((), ())),
                              preferred_element_type=F32)
        col_s = lax.broadcasted_iota(jnp.int32, (c, HGRN_WIDTH), 1) % c
        row_t = lax.broadcasted_iota(jnp.int32, (c, HGRN_WIDTH), 0)
        return qd, jnp.where(col_s <= row_t, att, 0.0).astype(BF16)

    def chunk_output(ci, qd, att, state_t):
        rows = pl.ds(ci * c, c)
        k, v, bc = ks_ref[rows, :], i_ref[rows, :], bc_ref[rows, :]
        b_last = bc[c - 1:c, :]
        o = jnp.dot(att, stack_heads(v), preferred_element_type=F32)
        o = o + lax.dot_general(qd, state_t.astype(BF16), (((1,), (1,)), ((), ())),
                                preferred_element_type=F32)
        upd = lax.dot_general(v.astype(BF16), (k * jnp.exp(b_last - bc)).astype(BF16),
                              (((0,), (0,)), ((), ())), preferred_element_type=F32)
        rh = lax.broadcasted_iota(jnp.int32, (HGRN_WIDTH, HGRN_WIDTH), 0) // HGRN_DIM
        ch = lax.broadcasted_iota(jnp.int32, (HGRN_WIDTH, HGRN_WIDTH), 1) // HGRN_DIM
        o_ref[rows, :] = o
        return state_t * jnp.exp(b_last) + jnp.where(rh == ch, upd, 0.0)

    def factored_chunks():
        state_t = state_ref[...]
        scores = chunk_scores(0)
        for ci in range(n_chunks):
            scores_next = chunk_scores(ci + 1) if ci + 1 < n_chunks else None
            state_t = chunk_output(ci, *scores, state_t)
            scores = scores_next
        state_ref[...] = state_t

    def general_chunk(ci, carry):
        rows = chunk_rows(ci)
        q, k, v = qs_ref[rows, :], ks_ref[rows, :], i_ref[rows, :]
        expo = sum(jnp.dot(em_ref[...], p, preferred_element_type=F32)
                   for p in _split3(lf_ref[rows, :]))
        decay = jnp.exp(expo)
        outs = []
        for h in range(HGRN_HEADS):
            cols = slice(h * HGRN_DIM, (h + 1) * HGRN_DIM)
            qh, kh, vh = q[:, cols], k[:, cols], v[:, cols]
            att = jnp.where(t_idx == s_idx,
                            jnp.sum(qh * kh, axis=-1, keepdims=True), 0.0)
            for lvl in range(HGRN_LEVELS):
                m = 1 << lvl
                d = decay[lvl * c:(lvl + 1) * c, cols]
                right = (r_idx // m) % 2 == 1
                ql = jnp.where(right, qh * d, 0.0)
                kl = jnp.where(right, 0.0, kh * d)
                a = lax.dot_general(ql.astype(BF16), kl.astype(BF16),
                                    (((1,), (1,)), ((), ())),
                                    preferred_element_type=F32)
                same = (t_idx // (2 * m)) == (s_idx // (2 * m))
                att = att + jnp.where(same, a, 0.0)
            d_in = decay[HGRN_LEVELS * c:(HGRN_LEVELS + 1) * c, cols]
            d_out = decay[(HGRN_LEVELS + 1) * c:(HGRN_LEVELS + 2) * c, cols]
            state_t = state_ref[cols, cols]
            o = _bdot(att, vh) + lax.dot_general(
                (qh * d_in).astype(BF16), state_t.astype(BF16),
                (((1,), (1,)), ((), ())), preferred_element_type=F32)
            upd_t = lax.dot_general(vh.astype(BF16), (kh * d_out).astype(BF16),
                                    (((0,), (0,)), ((), ())),
                                    preferred_element_type=F32)
            state_ref[cols, cols] = state_t * d_in[c - 1:c, :] + upd_t
            outs.append(o)
        o_ref[rows, :] = jnp.concatenate(outs, axis=-1)
        return carry

    @pl.when(jnp.logical_not(strong_decay))
    def _():
        factored_chunks()

    @pl.when(strong_decay)
    def _():
        lax.fori_loop(0, n_chunks, general_chunk, 0)

    o = o_ref[...]
    sq = sum(jnp.dot(p, hm_ref[...], preferred_element_type=F32) for p in _split3(o * o))
    inv = lax.rsqrt(sq * (1.0 / HGRN_DIM) + EPS)
    gate = g_ref[...]
    o_ref[...] = o * inv * gain_ref[...] * (gate * jax.nn.sigmoid(gate))


def _hgrn(u, lb_logits, gain, layer, seq_len):
    n = u.shape[0]
    tile = min(TOKEN_TILE, seq_len)
    depth = lb_logits.shape[0]
    col0 = (ATTN_WIDTH + 2 * KV_WIDTH) // HGRN_WIDTH
    col = lambda j: pl.BlockSpec((tile, HGRN_WIDTH), lambda i: (i, col0 + j))
    em = _hgrn_exponent_matrix()
    channel_head = jnp.arange(HGRN_WIDTH) // HGRN_DIM
    head_mask = (channel_head[:, None] == channel_head[None, :]).astype(BF16)
    tile_f32 = pltpu.VMEM((tile, HGRN_WIDTH), F32)
    return pl.pallas_call(
        functools.partial(_hgrn_kernel, seq_len=seq_len, tile=tile, layer=layer),
        grid=(n // tile,),
        in_specs=[col(0), col(1), col(2), col(3),
                  _full((depth, HGRN_WIDTH)), _full((1, HGRN_WIDTH)),
                  _full(em.shape), _full(head_mask.shape)],
        out_specs=pl.BlockSpec((tile, HGRN_WIDTH), lambda i: (i, 0)),
        out_shape=jax.ShapeDtypeStruct((n, HGRN_WIDTH), F32),
        scratch_shapes=[pltpu.VMEM((HGRN_WIDTH, HGRN_WIDTH), F32),
                        tile_f32, tile_f32, tile_f32, tile_f32],
        compiler_params=_params(),
        name="hgrn2",
    )(u, u, u, u, lb_logits, gain, em, head_mask)


def _s5_prep_kernel(ar_ref, ai_ref, ldt_ref, br_ref, bi_ref,
                    abr_ref, abi_ref, bbr_ref, bbi_ref):
    ar, ai = ar_ref[...], ai_ref[...]
    dt = jnp.exp(ldt_ref[...])
    mag = jnp.exp(ar * dt)
    abar_re = mag * jnp.cos(ai * dt)
    abar_im = mag * jnp.sin(ai * dt)
    nr, ni = abar_re - 1.0, abar_im
    den = ar * ar + ai * ai
    z_re = (nr * ar + ni * ai) / den
    z_im = (ni * ar - nr * ai) / den
    abr_ref[...] = abar_re
    abi_ref[...] = abar_im
    br, bi = br_ref[...], bi_ref[...]
    bbr_ref[...] = z_re[:, None, :] * br - z_im[:, None, :] * bi
    bbi_ref[...] = z_re[:, None, :] * bi + z_im[:, None, :] * br


def _s5_prep(a_re, a_im, log_dt, b_re, b_im):
    gp = jax.ShapeDtypeStruct((S5_GROUPS, S5_STATE), F32)
    gcp = jax.ShapeDtypeStruct((S5_GROUPS, S5_GROUP_CH, S5_STATE), F32)
    return pl.pallas_call(
        _s5_prep_kernel,
        out_shape=(gp, gp, gcp, gcp),
        name="s5_discretize",
    )(a_re, a_im, log_dt[:, None], jnp.swapaxes(b_re, 1, 2), jnp.swapaxes(b_im, 1, 2))


def _block_diag(blocks):
    g, r, c = blocks.shape
    eye = jnp.eye(g, dtype=blocks.dtype)
    return (blocks[:, :, None, :] * eye[:, None, :, None]).reshape(g * r, g * c)


def _s5_kernel(u_ref, a_ref, b_ref, c_ref, d_ref, gw_ref, gb_ref, o_ref,
               x_ref, state_ref, *, batch, steps):
    @pl.when(pl.program_id(0) == 0)
    def _():
        state_ref[...] = jnp.zeros_like(state_ref)

    u = jnp.swapaxes(u_ref[...], 0, 1).reshape(steps * batch, S5_WIDTH)
    a_re = jnp.broadcast_to(a_ref[0:1, :], (batch, S5_STATES))
    a_im = jnp.broadcast_to(a_ref[1:2, :], (batch, S5_STATES))
    x_ref[...] = _bdot(u, b_ref[...])

    x_re, x_im = state_ref[:, 0:S5_STATES], state_ref[:, S5_STATES:2 * S5_STATES]
    for t in range(steps):
        rows = slice(t * batch, (t + 1) * batch)
        bu_re = x_ref[rows, 0:S5_STATES]
        bu_im = x_ref[rows, S5_STATES:2 * S5_STATES]
        x_re, x_im = (a_re * x_re - a_im * x_im + bu_re,
                      a_re * x_im + a_im * x_re + bu_im)
        x_ref[rows, 0:S5_STATES] = x_re
        x_ref[rows, S5_STATES:2 * S5_STATES] = x_im
    state_ref[:, 0:S5_STATES] = x_re
    state_ref[:, S5_STATES:2 * S5_STATES] = x_im

    y = _bdot(x_ref[...], c_ref[...]) + d_ref[...] * u
    z = jax.nn.gelu(y)
    out = z * jax.nn.sigmoid(_bdot(z, gw_ref[...]) + gb_ref[...])
    o_ref[...] = jnp.swapaxes(out.reshape(steps, batch, S5_WIDTH), 0, 1)


def _s5(u, abar, b_blk, c_blk, d_skip, glu_w, glu_b, batch, seq_len):
    steps = min(S5_TIME_TILE, seq_len)
    rows = steps * batch
    u_col = IN_PROJ_WIDTH // S5_WIDTH - 1
    return pl.pallas_call(
        functools.partial(_s5_kernel, batch=batch, steps=steps),
        grid=(seq_len // steps,),
        in_specs=[
            pl.BlockSpec((batch, steps, S5_WIDTH), lambda i: (0, i, u_col)),
            _full((2, S5_STATES)),
            _full((S5_WIDTH, 2 * S5_STATES)),
            _full((2 * S5_STATES, S5_WIDTH)),
            _full((1, S5_WIDTH)),
            _full((S5_WIDTH, S5_WIDTH)),
            _full((1, S5_WIDTH)),
        ],
        out_specs=pl.BlockSpec((batch, steps, S5_WIDTH), lambda i: (0, i, 0)),
        out_shape=jax.ShapeDtypeStruct((batch, seq_len, S5_WIDTH), F32),
        scratch_shapes=[pltpu.VMEM((rows, 2 * S5_STATES), F32),
                        pltpu.VMEM((batch, 2 * S5_STATES), F32)],
        compiler_params=_params(),
        name="s5_ssm",
    )(u, abar, b_blk, c_blk, d_skip, glu_w, glu_b)


def _s5_mixer(u, batch, seq_len, a_re, a_im, log_dt, b_re, b_im, c_re, c_im, d_skip,
              glu_w, glu_b):
    abar_re, abar_im, bbar_re, bbar_im = _s5_prep(a_re, a_im, log_dt, b_re, b_im)
    abar = jnp.stack([abar_re.reshape(-1), abar_im.reshape(-1)])
    b_blk = jnp.concatenate([_block_diag(bbar_re), _block_diag(bbar_im)], axis=1)
    c_blk = jnp.concatenate([_block_diag(jnp.swapaxes(c_re, 1, 2)),
                             -_block_diag(jnp.swapaxes(c_im, 1, 2))], axis=0)
    y = _s5(u.reshape(batch, seq_len, IN_PROJ_WIDTH), abar, b_blk.astype(BF16),
            c_blk.astype(BF16), d_skip.reshape(1, S5_WIDTH), glu_w.astype(BF16),
            glu_b.reshape(1, S5_WIDTH), batch, seq_len)
    return y.reshape(batch * seq_len, S5_WIDTH)


def kernel(x, norm_ffn1, ffn1_w_gate, ffn1_w_up, ffn1_w_down, norm_mix, w_in, attn_sinks,
           hgrn_lb_logits, hgrn_norm, s5_a_re, s5_a_im, s5_log_dt, s5_b_re, s5_b_im,
           s5_c_re, s5_c_im, s5_d, s5_glu_w, s5_glu_b, w_out, norm_ffn2, ffn2_w_gate,
           ffn2_w_up, ffn2_w_down, norm_final):
    batch, seq_len, _ = x.shape
    depth = w_in.shape[0]
    h = x.reshape(batch * seq_len, D_MODEL)
    row = lambda v: v.reshape(1, -1)
    final_gain = row(norm_final)
    for layer in range(depth):
        h = _ffn(h, None, row(norm_ffn1[layer]), ffn1_w_gate, ffn1_w_up, ffn1_w_down,
                 final_gain, False, layer, seq_len)
        u = _in_proj(h, row(norm_mix[layer]), w_in, layer, seq_len)
        y_a = _attention(u, attn_sinks[layer], seq_len)
        y_b = _hgrn(u, hgrn_lb_logits, row(hgrn_norm[layer]), layer, seq_len)
        y_c = _s5_mixer(u, batch, seq_len, s5_a_re[layer], s5_a_im[layer],
                        s5_log_dt[layer], s5_b_re[layer], s5_b_im[layer], s5_c_re[layer],
                        s5_c_im[layer], s5_d[layer], s5_glu_w[layer], s5_glu_b[layer])
        h = _ffn(h, (y_a, y_b, y_c, w_out), row(norm_ffn2[layer]), ffn2_w_gate, ffn2_w_up,
                 ffn2_w_down, final_gain, layer == depth - 1, layer, seq_len)
    return h.reshape(batch, seq_len, D_MODEL)
```

```python
import functools
import math

import jax
import jax.numpy as jnp
from jax import lax
from jax.experimental import pallas as pl
from jax.experimental.pallas import tpu as pltpu

D_MODEL = 1024
D_FF = 2816
EPS = 1e-6

ATTN_HEADS = 8
ATTN_KV_HEADS = 2
ATTN_GROUP = ATTN_HEADS // ATTN_KV_HEADS
HEAD_DIM = 64
WINDOW = 128
ATTN_WIDTH = ATTN_HEADS * HEAD_DIM
KV_WIDTH = ATTN_KV_HEADS * HEAD_DIM

HGRN_HEADS = 4
HGRN_DIM = 64
HGRN_WIDTH = HGRN_HEADS * HGRN_DIM
HGRN_CHUNK = 64
HGRN_LEVELS = 6

S5_GROUPS = 16
S5_GROUP_CH = 16
S5_STATE = 64
S5_WIDTH = S5_GROUPS * S5_GROUP_CH
S5_STATES = S5_GROUPS * S5_STATE

IN_PROJ_WIDTH = ATTN_WIDTH + 2 * KV_WIDTH + 4 * HGRN_WIDTH + S5_WIDTH

VMEM_LIMIT_BYTES = 56 * 1024 * 1024

TOKEN_TILE = 512
S5_TIME_TILE = 128
MASK_VALUE = -1e30

F32 = jnp.float32
BF16 = jnp.bfloat16


def _rms(x, gain):
    return x * lax.rsqrt(jnp.mean(x * x, axis=-1, keepdims=True) + EPS) * gain


def _bdot(a, b):
    return jnp.dot(a.astype(BF16), b.astype(BF16), preferred_element_type=F32)


def _params(n_axes=1):
    return pltpu.CompilerParams(
        dimension_semantics=("arbitrary",) * n_axes,
        vmem_limit_bytes=VMEM_LIMIT_BYTES,
    )


def _full(shape):
    return pl.BlockSpec(shape, lambda i: (0,) * len(shape))


WEIGHT_LOAD_STEPS = 16


def _token_tile(step):
    return jnp.maximum(step - WEIGHT_LOAD_STEPS, 0)


def _weight_chunk_spec(w, layer):
    _, rows, cols = w.shape
    return pl.BlockSpec(
        (None, rows // WEIGHT_LOAD_STEPS, cols),
        lambda i: (layer, jnp.minimum(i, WEIGHT_LOAD_STEPS - 1), 0))


def _keep_weight_chunk(step, w_ref, w_s):
    chunk = w_ref.shape[0]
    w_s[pl.ds(pl.multiple_of(step * chunk, chunk), chunk), :] = w_ref[...].astype(BF16)


def _ffn_kernel(*refs, mix, final_norm):
    if mix:
        (x_ref, ya_ref, yb_ref, yc_ref, wo_ref, g_ref, wg_ref, wu_ref, wd_ref, gf_ref,
         o_ref, wo_s, wg_s, wu_s, wd_s) = refs
        streamed = [(wo_ref, wo_s), (wg_ref, wg_s), (wu_ref, wu_s), (wd_ref, wd_s)]
    else:
        x_ref, g_ref, wg_ref, wu_ref, wd_ref, gf_ref, o_ref, wg_s, wu_s, wd_s = refs
        streamed = [(wg_ref, wg_s), (wu_ref, wu_s), (wd_ref, wd_s)]
    step = pl.program_id(0)

    @pl.when(step < WEIGHT_LOAD_STEPS)
    def _():
        for w_ref, w_s in streamed:
            _keep_weight_chunk(step, w_ref, w_s)

    @pl.when(step >= WEIGHT_LOAD_STEPS)
    def _():
        x = x_ref[...]
        if mix:
            y = jnp.concatenate([ya_ref[...], yb_ref[...], yc_ref[...]], axis=-1)
            x = x + _bdot(y, wo_s[...])
        h = _rms(x, g_ref[...]).astype(BF16)
        gate = jnp.dot(h, wg_s[...], preferred_element_type=F32)
        up = jnp.dot(h, wu_s[...], preferred_element_type=F32)
        act = (gate * jax.nn.sigmoid(gate) * up).astype(BF16)
        y = x + 0.5 * jnp.dot(act, wd_s[...], preferred_element_type=F32)
        if final_norm:
            y = _rms(y, gf_ref[...])
        o_ref[...] = y


def _ffn(x, mixer_out, gain, w_gate, w_up, w_down, final_gain, final_norm, layer, seq_len):
    n = x.shape[0]
    tm = min(TOKEN_TILE, seq_len)
    row = lambda width: pl.BlockSpec((tm, width), lambda i: (_token_tile(i), 0))
    vec = _full((1, D_MODEL))
    ffn_weights = [w_gate, w_up, w_down]
    if mixer_out is None:
        in_specs, args, streamed = [row(D_MODEL)], [x], ffn_weights
    else:
        y_a, y_b, y_c, w_out = mixer_out
        in_specs = [row(D_MODEL), row(ATTN_WIDTH), row(HGRN_WIDTH),
                    row(S5_WIDTH), _weight_chunk_spec(w_out, layer)]
        args = [x, y_a, y_b, y_c, w_out]
        streamed = [w_out] + ffn_weights
    in_specs += [vec] + [_weight_chunk_spec(w, layer) for w in ffn_weights] + [vec]
    return pl.pallas_call(
        functools.partial(_ffn_kernel, mix=mixer_out is not None, final_norm=final_norm),
        grid=(WEIGHT_LOAD_STEPS + n // tm,),
        in_specs=in_specs,
        out_specs=row(D_MODEL),
        out_shape=jax.ShapeDtypeStruct((n, D_MODEL), F32),
        scratch_shapes=[pltpu.VMEM(w.shape[1:], BF16) for w in streamed],
        compiler_params=_params(),
        name="ffn",
    )(*args, gain, *ffn_weights, final_gain)


def _in_proj_kernel(x_ref, g_ref, w_ref, u_ref, w_s):
    step = pl.program_id(0)

    @pl.when(step < WEIGHT_LOAD_STEPS)
    def _():
        _keep_weight_chunk(step, w_ref, w_s)

    @pl.when(step >= WEIGHT_LOAD_STEPS)
    def _():
        u_ref[...] = jnp.dot(_rms(x_ref[...], g_ref[...]).astype(BF16), w_s[...],
                             preferred_element_type=F32)


def _in_proj(x, gain, w_in, layer, seq_len):
    n = x.shape[0]
    tm = min(TOKEN_TILE, seq_len)
    row = lambda width: pl.BlockSpec((tm, width), lambda i: (_token_tile(i), 0))
    return pl.pallas_call(
        _in_proj_kernel,
        grid=(WEIGHT_LOAD_STEPS + n // tm,),
        in_specs=[row(D_MODEL), _full((1, D_MODEL)), _weight_chunk_spec(w_in, layer)],
        out_specs=row(IN_PROJ_WIDTH),
        out_shape=jax.ShapeDtypeStruct((n, IN_PROJ_WIDTH), F32),
        scratch_shapes=[pltpu.VMEM(w_in.shape[1:], BF16)],
        compiler_params=_params(),
        name="in_proj",
    )(x, gain, w_in)


def _alibi_slope(head):
    return 2.0 ** (-8.0 * (head + 1.0) / ATTN_HEADS)


def _attn_kernel(sink_ref, q_ref, kv_ref, kvp_ref, o_ref, bias_ref, *, seq_len, tile):
    first = (pl.program_id(0) * tile) % seq_len == 0
    cols = ATTN_GROUP * WINDOW
    c_idx = lax.broadcasted_iota(jnp.int32, (2 * WINDOW, cols), 0)
    log2e = math.log2(math.e)
    scale = log2e / math.sqrt(HEAD_DIM)

    @pl.when(pl.program_id(0) == 0)
    def _():
        g_idx = lax.broadcasted_iota(jnp.int32, (1, cols), 1) // WINDOW
        t_idx = lax.broadcasted_iota(jnp.int32, (2 * WINDOW, cols), 1) % WINDOW
        rel = t_idx + WINDOW - c_idx
        in_win = (rel >= 0) & (rel < WINDOW)
        relf = rel.astype(F32)
        for hk in range(ATTN_KV_HEADS):
            slope = jnp.zeros((1, cols), F32)
            sink = jnp.zeros((1, cols), F32)
            for g in range(ATTN_GROUP):
                head = hk * ATTN_GROUP + g
                slope = jnp.where(g_idx == g, _alibi_slope(head) * log2e, slope)
                sink = jnp.where(g_idx == g, sink_ref[head] * log2e, sink)
            bias = jnp.where(in_win, -(slope * relf), MASK_VALUE)
            bias_ref[hk] = jnp.where(c_idx == 0, sink, bias)

    keys = jnp.concatenate([kvp_ref[...], kv_ref[...]], axis=0)
    k_all = keys[:, 0:KV_WIDTH]
    v_t = jnp.transpose(keys[:, KV_WIDTH:2 * KV_WIDTH])
    krow = lax.broadcasted_iota(jnp.int32, (2 * WINDOW, HEAD_DIM), 0)
    vcol = lax.broadcasted_iota(jnp.int32, (HEAD_DIM, 2 * WINDOW), 1)
    ones_rows = jnp.ones((8, 2 * WINDOW), F32)
    no_prev = (c_idx >= 1) & (c_idx < jnp.where(first, WINDOW, 0))

    n_blocks = tile // WINDOW
    units = [(blk, hk) for blk in range(n_blocks) for hk in range(ATTN_KV_HEADS)]
    q_t = [jnp.transpose(q_ref[blk * WINDOW:(blk + 1) * WINDOW, :] * scale).astype(BF16)
           for blk in range(n_blocks)]
    out_t = [[None] * ATTN_HEADS for _ in range(n_blocks)]

    def scores(blk, hk):
        slots = slice(blk * WINDOW, (blk + 2) * WINDOW)
        dims = slice(hk * HEAD_DIM, (hk + 1) * HEAD_DIM)
        kh = jnp.where(krow == 0, 0.0, k_all[slots, dims]).astype(BF16)
        qs = jnp.concatenate(
            [q_t[blk][h * HEAD_DIM:(h + 1) * HEAD_DIM, :]
             for h in range(hk * ATTN_GROUP, (hk + 1) * ATTN_GROUP)], axis=1)
        s = jnp.dot(kh, qs, preferred_element_type=F32) + bias_ref[hk]
        if blk == 0:
            s = jnp.where(no_prev, MASK_VALUE, s)
        return s

    def weighted_values(blk, hk, s):
        slots = slice(blk * WINDOW, (blk + 2) * WINDOW)
        dims = slice(hk * HEAD_DIM, (hk + 1) * HEAD_DIM)
        vh = jnp.where(vcol == 0, 0.0, v_t[dims, slots])
        vh = jnp.concatenate([vh, ones_rows], axis=0).astype(BF16)
        p = jnp.exp2(s - jnp.max(s, axis=0, keepdims=True)).astype(BF16)
        o = jnp.dot(vh, p, preferred_element_type=F32)
        o = o[:HEAD_DIM] * (1.0 / o[HEAD_DIM:HEAD_DIM + 1])
        for g in range(ATTN_GROUP):
            out_t[blk][hk * ATTN_GROUP + g] = o[:, g * WINDOW:(g + 1) * WINDOW]
        if hk == ATTN_KV_HEADS - 1:
            o_ref[blk * WINDOW:(blk + 1) * WINDOW, :] = jnp.transpose(
                jnp.concatenate(out_t[blk], axis=0))

    s = scores(*units[0])
    for i, unit in enumerate(units):
        s_next = scores(*units[i + 1]) if i + 1 < len(units) else None
        weighted_values(*unit, s)
        s = s_next


def _attention(u, sinks, seq_len):
    n = u.shape[0]
    tile = min(TOKEN_TILE, seq_len)
    blocks_per_tile = tile // WINDOW
    kv_col = ATTN_WIDTH // (2 * KV_WIDTH)
    return pl.pallas_call(
        functools.partial(_attn_kernel, seq_len=seq_len, tile=tile),
        grid=(n // tile,),
        in_specs=[
            pl.BlockSpec(memory_space=pltpu.SMEM),
            pl.BlockSpec((tile, ATTN_WIDTH), lambda i: (i, 0)),
            pl.BlockSpec((tile, 2 * KV_WIDTH), lambda i: (i, kv_col)),
            pl.BlockSpec((WINDOW, 2 * KV_WIDTH),
                         lambda i: (jnp.maximum(i * blocks_per_tile - 1, 0), kv_col)),
        ],
        out_specs=pl.BlockSpec((tile, ATTN_WIDTH), lambda i: (i, 0)),
        out_shape=jax.ShapeDtypeStruct((n, ATTN_WIDTH), F32),
        scratch_shapes=[pltpu.VMEM((ATTN_KV_HEADS, 2 * WINDOW, ATTN_GROUP * WINDOW), F32)],
        compiler_params=_params(),
        name="swa_attention",
    )(sinks, u, u, u)


HGRN_FACTORED_DECAY_LIMIT = 80.0


def _hgrn_exponent_matrix():
    c = HGRN_CHUNK
    r = jnp.arange(c)[:, None]
    j = jnp.arange(c)[None, :]
    blocks = []
    for lvl in range(HGRN_LEVELS):
        m = 1 << lvl
        start = (r // (2 * m)) * (2 * m)
        right = (r // m) % 2 == 1
        in_right = right & (j >= start + m) & (j <= r)
        in_left = (~right) & (j > r) & (j < start + m)
        blocks.append(in_right | in_left)
    blocks.append(j <= r)
    blocks.append(j > r)
    return jnp.concatenate(blocks, axis=0).astype(BF16)


def _split3(x):
    hi = x.astype(BF16)
    r1 = x - hi.astype(F32)
    mid = r1.astype(BF16)
    lo = (r1 - mid.astype(F32)).astype(BF16)
    return hi, mid, lo


def _hgrn_kernel(q_ref, f_ref, i_ref, g_ref, lbl_ref, gain_ref, em_ref, hm_ref, o_ref,
                 state_ref, qs_ref, ks_ref, lf_ref, bc_ref, *, seq_len, tile, layer):
    c = HGRN_CHUNK
    first = (pl.program_id(0) * tile) % seq_len == 0

    @pl.when(first)
    def _():
        state_ref[...] = jnp.zeros_like(state_ref)

    logits = lbl_ref[...]
    e = jnp.exp(logits - jnp.max(logits, axis=0, keepdims=True))
    sm = e / jnp.sum(e, axis=0, keepdims=True)
    lb = jnp.zeros((1, HGRN_WIDTH), F32)
    for l in range(1, layer + 1):
        lb = lb + sm[l:l + 1, :]

    r_idx = lax.broadcasted_iota(jnp.int32, (c, 1), 0)
    t_idx = lax.broadcasted_iota(jnp.int32, (c, c), 0)
    s_idx = lax.broadcasted_iota(jnp.int32, (c, c), 1)
    lane_head = lax.broadcasted_iota(jnp.int32, (c, HGRN_WIDTH), 1) // HGRN_DIM
    n_chunks = tile // c

    def chunk_rows(ci):
        return pl.ds(pl.multiple_of(ci * c, c), c)

    def prepare(ci, carry):
        rows = chunk_rows(ci)
        z = f_ref[rows, :]
        q = q_ref[rows, :]
        qs_ref[rows, :] = q * jax.nn.sigmoid(q)
        e = jnp.exp(-jnp.abs(z))
        r = 1.0 + e
        inv = 1.0 / r
        pos = z >= 0.0
        sig = jnp.where(pos, 1.0, e) * inv
        ks_ref[rows, :] = (1.0 - lb) * (jnp.where(pos, e, 1.0) * inv)
        lf = jnp.where(lb > 0.0, jnp.log(lb + (1.0 - lb) * sig),
                       jnp.minimum(z, 0.0) - jnp.log(r))
        lf_ref[rows, :] = lf
        tri = em_ref[HGRN_LEVELS * c:(HGRN_LEVELS + 1) * c, :]
        bc_ref[rows, :] = sum(jnp.dot(tri, p, preferred_element_type=F32)
                              for p in _split3(lf))
        return carry

    lax.fori_loop(0, n_chunks, prepare, 0, unroll=True)
    strong_decay = jnp.min(bc_ref[...]) < -HGRN_FACTORED_DECAY_LIMIT

    def stack_heads(x):
        return jnp.concatenate(
            [jnp.where(lane_head == h, x, 0.0) for h in range(HGRN_HEADS)],
            axis=0).astype(BF16)

    def chunk_scores(ci):
        rows = pl.ds(ci * c, c)
        bc = bc_ref[rows, :]
        qd = (qs_ref[rows, :] * jnp.exp(bc)).astype(BF16)
        kb = stack_heads(ks_ref[rows, :] * jnp.exp(-bc))
        att = lax.dot_general(qd, kb, (((1,), (1,)), ((), ())),
                              preferred_element_type=F32)
        col_s = lax.broadcasted_iota(jnp.int32, (c, HGRN_WIDTH), 1) % c
        row_t = lax.broadcasted_iota(jnp.int32, (c, HGRN_WIDTH), 0)
        return qd, jnp.where(col_s <= row_t, att, 0.0).astype(BF16)

    def chunk_output(ci, qd, att, state_t):
        rows = pl.ds(ci * c, c)
        k, v, bc = ks_ref[rows, :], i_ref[rows, :], bc_ref[rows, :]
        b_last = bc[c - 1:c, :]
        o = jnp.dot(att, stack_heads(v), preferred_element_type=F32)
        o = o + lax.dot_general(qd, state_t.astype(BF16), (((1,), (1,)), ((), ())),
                                preferred_element_type=F32)
        upd = lax.dot_general(v.astype(BF16), (k * jnp.exp(b_last - bc)).astype(BF16),
                              (((0,), (0,)), ((), ())), preferred_element_type=F32)
        rh = lax.broadcasted_iota(jnp.int32, (HGRN_WIDTH, HGRN_WIDTH), 0) // HGRN_DIM
        ch = lax.broadcasted_iota(jnp.int32, (HGRN_WIDTH, HGRN_WIDTH), 1) // HGRN_DIM
        o_ref[rows, :] = o
        return state_t * jnp.exp(b_last) + jnp.where(rh == ch, upd, 0.0)

    def factored_chunks():
        state_t = state_ref[...]
        scores = chunk_scores(0)
        for ci in range(n_chunks):
            scores_next = chunk_scores(ci + 1) if ci + 1 < n_chunks else None
            state_t = chunk_output(ci, *scores, state_t)
            scores = scores_next
        state_ref[...] = state_t

    def general_chunk(ci, carry):
        rows = chunk_rows(ci)
        q, k, v = qs_ref[rows, :], ks_ref[rows, :], i_ref[rows, :]
        expo = sum(jnp.dot(em_ref[...], p, preferred_element_type=F32)
                   for p in _split3(lf_ref[rows, :]))
        decay = jnp.exp(expo)
        outs = []
        for h in range(HGRN_HEADS):
            cols = slice(h * HGRN_DIM, (h + 1) * HGRN_DIM)
            qh, kh, vh = q[:, cols], k[:, cols], v[:, cols]
            att = jnp.where(t_idx == s_idx,
                            jnp.sum(qh * kh, axis=-1, keepdims=True), 0.0)
            for lvl in range(HGRN_LEVELS):
                m = 1 << lvl
                d = decay[lvl * c:(lvl + 1) * c, cols]
                right = (r_idx // m) % 2 == 1
                ql = jnp.where(right, qh * d, 0.0)
                kl = jnp.where(right, 0.0, kh * d)
                a = lax.dot_general(ql.astype(BF16), kl.astype(BF16),
                                    (((1,), (1,)), ((), ())),
                                    preferred_element_type=F32)
                same = (t_idx // (2 * m)) == (s_idx // (2 * m))
                att = att + jnp.where(same, a, 0.0)
            d_in = decay[HGRN_LEVELS * c:(HGRN_LEVELS + 1) * c, cols]
            d_out = decay[(HGRN_LEVELS + 1) * c:(HGRN_LEVELS + 2) * c, cols]
            state_t = state_ref[cols, cols]
            o = _bdot(att, vh) + lax.dot_general(
                (qh * d_in).astype(BF16), state_t.astype(BF16),
                (((1,), (1,)), ((), ())), preferred_element_type=F32)
            upd_t = lax.dot_general(vh.astype(BF16), (kh * d_out).astype(BF16),
                                    (((0,), (0,)), ((), ())),
                                    preferred_element_type=F32)
            state_ref[cols, cols] = state_t * d_in[c - 1:c, :] + upd_t
            outs.append(o)
        o_ref[rows, :] = jnp.concatenate(outs, axis=-1)
        return carry

    @pl.when(jnp.logical_not(strong_decay))
    def _():
        factored_chunks()

    @pl.when(strong_decay)
    def _():
        lax.fori_loop(0, n_chunks, general_chunk, 0)

    o = o_ref[...]
    sq = sum(jnp.dot(p, hm_ref[...], preferred_element_type=F32) for p in _split3(o * o))
    inv = lax.rsqrt(sq * (1.0 / HGRN_DIM) + EPS)
    gate = g_ref[...]
    o_ref[...] = o * inv * gain_ref[...] * (gate * jax.nn.sigmoid(gate))


def _hgrn(u, lb_logits, gain, layer, seq_len):
    n = u.shape[0]
    tile = min(TOKEN_TILE, seq_len)
    depth = lb_logits.shape[0]
    col0 = (ATTN_WIDTH + 2 * KV_WIDTH) // HGRN_WIDTH
    col = lambda j: pl.BlockSpec((tile, HGRN_WIDTH), lambda i: (i, col0 + j))
    em = _hgrn_exponent_matrix()
    channel_head = jnp.arange(HGRN_WIDTH) // HGRN_DIM
    head_mask = (channel_head[:, None] == channel_head[None, :]).astype(BF16)
    tile_f32 = pltpu.VMEM((tile, HGRN_WIDTH), F32)
    return pl.pallas_call(
        functools.partial(_hgrn_kernel, seq_len=seq_len, tile=tile, layer=layer),
        grid=(n // tile,),
        in_specs=[col(0), col(1), col(2), col(3),
                  _full((depth, HGRN_WIDTH)), _full((1, HGRN_WIDTH)),
                  _full(em.shape), _full(head_mask.shape)],
        out_specs=pl.BlockSpec((tile, HGRN_WIDTH), lambda i: (i, 0)),
        out_shape=jax.ShapeDtypeStruct((n, HGRN_WIDTH), F32),
        scratch_shapes=[pltpu.VMEM((HGRN_WIDTH, HGRN_WIDTH), F32),
                        tile_f32, tile_f32, tile_f32, tile_f32],
        compiler_params=_params(),
        name="hgrn2",
    )(u, u, u, u, lb_logits, gain, em, head_mask)


def _s5_prep_kernel(ar_ref, ai_ref, ldt_ref, br_ref, bi_ref,
                    abr_ref, abi_ref, bbr_ref, bbi_ref):
    ar, ai = ar_ref[...], ai_ref[...]
    dt = jnp.exp(ldt_ref[...])
    mag = jnp.exp(ar * dt)
    abar_re = mag * jnp.cos(ai * dt)
    abar_im = mag * jnp.sin(ai * dt)
    nr, ni = abar_re - 1.0, abar_im
    den = ar * ar + ai * ai
    z_re = (nr * ar + ni * ai) / den
    z_im = (ni * ar - nr * ai) / den
    abr_ref[...] = abar_re
    abi_ref[...] = abar_im
    br, bi = br_ref[...], bi_ref[...]
    bbr_ref[...] = z_re[:, None, :] * br - z_im[:, None, :] * bi
    bbi_ref[...] = z_re[:, None, :] * bi + z_im[:, None, :] * br


def _s5_prep(a_re, a_im, log_dt, b_re, b_im):
    gp = jax.ShapeDtypeStruct((S5_GROUPS, S5_STATE), F32)
    gcp = jax.ShapeDtypeStruct((S5_GROUPS, S5_GROUP_CH, S5_STATE), F32)
    return pl.pallas_call(
        _s5_prep_kernel,
        out_shape=(gp, gp, gcp, gcp),
        name="s5_discretize",
    )(a_re, a_im, log_dt[:, None], jnp.swapaxes(b_re, 1, 2), jnp.swapaxes(b_im, 1, 2))


def _block_diag(blocks):
    g, r, c = blocks.shape
    eye = jnp.eye(g, dtype=blocks.dtype)
    return (blocks[:, :, None, :] * eye[:, None, :, None]).reshape(g * r, g * c)


def _s5_kernel(u_ref, a_ref, b_ref, c_ref, d_ref, gw_ref, gb_ref, o_ref,
               xa_ref, xb_ref, ua_ref, ub_ref, state_ref, *, batch, steps):
    step = pl.program_id(0)

    @pl.when(step == 0)
    def _():
        state_ref[...] = jnp.zeros_like(state_ref)
        xb_ref[...] = jnp.zeros_like(xb_ref)
        ub_ref[...] = jnp.zeros_like(ub_ref)

    a_re = jnp.broadcast_to(a_ref[0:1, :], (batch, S5_STATES))
    a_im = jnp.broadcast_to(a_ref[1:2, :], (batch, S5_STATES))

    def pipeline_step(x_new, u_new, x_cur, u_cur):
        u = jnp.swapaxes(u_ref[...], 0, 1).reshape(steps * batch, S5_WIDTH)
        u_new[...] = u
        x_new[...] = _bdot(u, b_ref[...])

        x_re, x_im = state_ref[:, 0:S5_STATES], state_ref[:, S5_STATES:2 * S5_STATES]
        for t in range(steps):
            rows = slice(t * batch, (t + 1) * batch)
            bu_re = x_cur[rows, 0:S5_STATES]
            bu_im = x_cur[rows, S5_STATES:2 * S5_STATES]
            x_re, x_im = (a_re * x_re - a_im * x_im + bu_re,
                          a_re * x_im + a_im * x_re + bu_im)
            x_cur[rows, 0:S5_STATES] = x_re
            x_cur[rows, S5_STATES:2 * S5_STATES] = x_im
        state_ref[:, 0:S5_STATES] = x_re
        state_ref[:, S5_STATES:2 * S5_STATES] = x_im

        y = _bdot(x_cur[...], c_ref[...]) + d_ref[...] * u_cur[...]
        z = jax.nn.gelu(y)
        out = z * jax.nn.sigmoid(_bdot(z, gw_ref[...]) + gb_ref[...])
        o_ref[...] = jnp.swapaxes(out.reshape(steps, batch, S5_WIDTH), 0, 1)

    @pl.when(step % 2 == 0)
    def _():
        pipeline_step(xa_ref, ua_ref, xb_ref, ub_ref)

    @pl.when(step % 2 == 1)
    def _():
        pipeline_step(xb_ref, ub_ref, xa_ref, ua_ref)


def _s5(u, abar, b_blk, c_blk, d_skip, glu_w, glu_b, batch, seq_len):
    steps = min(S5_TIME_TILE, seq_len)
    rows = steps * batch
    n_tiles = seq_len // steps
    u_col = IN_PROJ_WIDTH // S5_WIDTH - 1
    x_tile = pltpu.VMEM((rows, 2 * S5_STATES), F32)
    u_tile = pltpu.VMEM((rows, S5_WIDTH), F32)
    return pl.pallas_call(
        functools.partial(_s5_kernel, batch=batch, steps=steps),
        grid=(n_tiles + 1,),
        in_specs=[
            pl.BlockSpec((batch, steps, S5_WIDTH),
                         lambda i: (0, jnp.minimum(i, n_tiles - 1), u_col)),
            _full((2, S5_STATES)),
            _full((S5_WIDTH, 2 * S5_STATES)),
            _full((2 * S5_STATES, S5_WIDTH)),
            _full((1, S5_WIDTH)),
            _full((S5_WIDTH, S5_WIDTH)),
            _full((1, S5_WIDTH)),
        ],
        out_specs=pl.BlockSpec((batch, steps, S5_WIDTH),
                               lambda i: (0, jnp.maximum(i - 1, 0), 0)),
        out_shape=jax.ShapeDtypeStruct((batch, seq_len, S5_WIDTH), F32),
        scratch_shapes=[x_tile, x_tile, u_tile, u_tile,
                        pltpu.VMEM((batch, 2 * S5_STATES), F32)],
        compiler_params=_params(),
        name="s5_ssm",
    )(u, abar, b_blk, c_blk, d_skip, glu_w, glu_b)


def _s5_mixer(u, batch, seq_len, a_re, a_im, log_dt, b_re, b_im, c_re, c_im, d_skip,
              glu_w, glu_b):
    abar_re, abar_im, bbar_re, bbar_im = _s5_prep(a_re, a_im, log_dt, b_re, b_im)
    abar = jnp.stack([abar_re.reshape(-1), abar_im.reshape(-1)])
    b_blk = jnp.concatenate([_block_diag(bbar_re), _block_diag(bbar_im)], axis=1)
    c_blk = jnp.concatenate([_block_diag(jnp.swapaxes(c_re, 1, 2)),
                             -_block_diag(jnp.swapaxes(c_im, 1, 2))], axis=0)
    y = _s5(u.reshape(batch, seq_len, IN_PROJ_WIDTH), abar, b_blk.astype(BF16),
            c_blk.astype(BF16), d_skip.reshape(1, S5_WIDTH), glu_w.astype(BF16),
            glu_b.reshape(1, S5_WIDTH), batch, seq_len)
    return y.reshape(batch * seq_len, S5_WIDTH)


def kernel(x, norm_ffn1, ffn1_w_gate, ffn1_w_up, ffn1_w_down, norm_mix, w_in, attn_sinks,
           hgrn_lb_logits, hgrn_norm, s5_a_re, s5_a_im, s5_log_dt, s5_b_re, s5_b_im,
           s5_c_re, s5_c_im, s5_d, s5_glu_w, s5_glu_b, w_out, norm_ffn2, ffn2_w_gate,
           ffn2_w_up, ffn2_w_down, norm_final):
    batch, seq_len, _ = x.shape
    depth = w_in.shape[0]
    h = x.reshape(batch * seq_len, D_MODEL)
    row = lambda v: v.reshape(1, -1)
    final_gain = row(norm_final)
    for layer in range(depth):
        h = _ffn(h, None, row(norm_ffn1[layer]), ffn1_w_gate, ffn1_w_up, ffn1_w_down,
                 final_gain, False, layer, seq_len)
        u = _in_proj(h, row(norm_mix[layer]), w_in, layer, seq_len)
        y_a = _attention(u, attn_sinks[layer], seq_len)
        y_b = _hgrn(u, hgrn_lb_logits, row(hgrn_norm[layer]), layer, seq_len)
        y_c = _s5_mixer(u, batch, seq_len, s5_a_re[layer], s5_a_im[layer],
                        s5_log_dt[layer], s5_b_re[layer], s5_b_im[layer], s5_c_re[layer],
                        s5_c_im[layer], s5_d[layer], s5_glu_w[layer], s5_glu_b[layer])
        h = _ffn(h, (y_a, y_b, y_c, w_out), row(norm_ffn2[layer]), ffn2_w_gate, ffn2_w_up,
                 ffn2_w_down, final_gain, layer == depth - 1, layer, seq_len)
    return h.reshape(batch, seq_len, D_MODEL)
```

```python
import functools
import math

import jax
import jax.numpy as jnp
from jax import lax
from jax.experimental import pallas as pl
from jax.experimental.pallas import tpu as pltpu

D_MODEL = 1024
D_FF = 2816
EPS = 1e-6

ATTN_HEADS = 8
ATTN_KV_HEADS = 2
ATTN_GROUP = ATTN_HEADS // ATTN_KV_HEADS
HEAD_DIM = 64
WINDOW = 128
ATTN_WIDTH = ATTN_HEADS * HEAD_DIM
KV_WIDTH = ATTN_KV_HEADS * HEAD_DIM

HGRN_HEADS = 4
HGRN_DIM = 64
HGRN_WIDTH = HGRN_HEADS * HGRN_DIM
HGRN_CHUNK = 64
HGRN_LEVELS = 6

S5_GROUPS = 16
S5_GROUP_CH = 16
S5_STATE = 64
S5_WIDTH = S5_GROUPS * S5_GROUP_CH
S5_STATES = S5_GROUPS * S5_STATE

IN_PROJ_WIDTH = ATTN_WIDTH + 2 * KV_WIDTH + 4 * HGRN_WIDTH + S5_WIDTH
_F_B_START = ATTN_WIDTH + 2 * KV_WIDTH + HGRN_WIDTH
_U_C_START = IN_PROJ_WIDTH - S5_WIDTH
MIX_BF16_COLUMNS = ((0, _F_B_START), (_F_B_START + HGRN_WIDTH, _U_C_START))
MIX_F32_COLUMNS = ((_F_B_START, _F_B_START + HGRN_WIDTH), (_U_C_START, IN_PROJ_WIDTH))
MIX_BF16_WIDTH = sum(b - a for a, b in MIX_BF16_COLUMNS)
MIX_F32_WIDTH = sum(b - a for a, b in MIX_F32_COLUMNS)

VMEM_LIMIT_BYTES = 56 * 1024 * 1024

TOKEN_TILE = 512
S5_TIME_TILE = 128
MASK_VALUE = -1e30

F32 = jnp.float32
BF16 = jnp.bfloat16


def _rms(x, gain):
    return x * lax.rsqrt(jnp.mean(x * x, axis=-1, keepdims=True) + EPS) * gain


def _bdot(a, b):
    return jnp.dot(a.astype(BF16), b.astype(BF16), preferred_element_type=F32)


def _params(n_axes=1):
    return pltpu.CompilerParams(
        dimension_semantics=("arbitrary",) * n_axes,
        vmem_limit_bytes=VMEM_LIMIT_BYTES,
    )


def _full(shape):
    return pl.BlockSpec(shape, lambda i: (0,) * len(shape))


WEIGHT_LOAD_STEPS = 16


def _token_tile(step):
    return jnp.maximum(step - WEIGHT_LOAD_STEPS, 0)


def _weight_chunk_spec(w, layer):
    _, rows, cols = w.shape
    return pl.BlockSpec(
        (None, rows // WEIGHT_LOAD_STEPS, cols),
        lambda i: (layer, jnp.minimum(i, WEIGHT_LOAD_STEPS - 1), 0))


def _keep_weight_chunk(step, w_ref, w_s):
    chunk = w_ref.shape[0]
    w_s[pl.ds(pl.multiple_of(step * chunk, chunk), chunk), :] = w_ref[...].astype(BF16)


def _ffn_kernel(*refs, mix, project, final_norm):
    refs = list(refs)
    take = lambda n: [refs.pop(0) for _ in range(n)]
    (x_ref,) = take(1)
    if mix:
        ya_ref, yb_ref, yc_ref, wo_ref = take(4)
    g_ref, wg_ref, wu_ref, wd_ref, gf_ref = take(5)
    if project:
        gm_ref, wi_ref = take(2)
    (o_ref,) = take(1)
    if project:
        ub_ref, uf_ref = take(2)
    if mix:
        (wo_s,) = take(1)
    wg_s, wu_s, wd_s = take(3)
    streamed = [(wg_ref, wg_s), (wu_ref, wu_s), (wd_ref, wd_s)]
    if mix:
        streamed.append((wo_ref, wo_s))
    if project:
        (wi_s,) = take(1)
        streamed.append((wi_ref, wi_s))
    step = pl.program_id(0)

    @pl.when(step < WEIGHT_LOAD_STEPS)
    def _():
        for w_ref, w_s in streamed:
            _keep_weight_chunk(step, w_ref, w_s)

    @pl.when(step >= WEIGHT_LOAD_STEPS)
    def _():
        x = x_ref[...]
        if mix:
            y = jnp.concatenate([ya_ref[...], yb_ref[...], yc_ref[...]], axis=-1)
            x = x + _bdot(y, wo_s[...])
        h = _rms(x, g_ref[...]).astype(BF16)
        gate = jnp.dot(h, wg_s[...], preferred_element_type=F32)
        up = jnp.dot(h, wu_s[...], preferred_element_type=F32)
        act = (gate * jax.nn.sigmoid(gate) * up).astype(BF16)
        y = x + 0.5 * jnp.dot(act, wd_s[...], preferred_element_type=F32)
        if final_norm:
            y = _rms(y, gf_ref[...])
        o_ref[...] = y
        if project:
            u = jnp.dot(_rms(y, gm_ref[...]).astype(BF16), wi_s[...],
                        preferred_element_type=F32)
            for dst, columns in ((ub_ref, MIX_BF16_COLUMNS), (uf_ref, MIX_F32_COLUMNS)):
                dst[...] = jnp.concatenate([u[:, a:b] for a, b in columns],
                                           axis=-1).astype(dst.dtype)


def _ffn(x, mixer_out, in_proj, gain, w_gate, w_up, w_down, final_gain, final_norm,
         layer, seq_len):
    n = x.shape[0]
    tm = min(TOKEN_TILE, seq_len)
    row = lambda width: pl.BlockSpec((tm, width), lambda i: (_token_tile(i), 0))
    vec = _full((1, D_MODEL))
    chunk = lambda w: _weight_chunk_spec(w, layer)
    ffn_weights = [w_gate, w_up, w_down]
    in_specs, args = [row(D_MODEL)], [x]
    out_specs = [row(D_MODEL)]
    out_shape = [jax.ShapeDtypeStruct((n, D_MODEL), F32)]
    scratch = list(ffn_weights)
    if mixer_out is not None:
        y_a, y_b, y_c, w_out = mixer_out
        in_specs += [row(ATTN_WIDTH), row(HGRN_WIDTH), row(S5_WIDTH), chunk(w_out)]
        args += [y_a, y_b, y_c, w_out]
        scratch.insert(0, w_out)
    in_specs += [vec] + [chunk(w) for w in ffn_weights] + [vec]
    args += [gain, *ffn_weights, final_gain]
    if in_proj is not None:
        mix_gain, w_in = in_proj
        in_specs += [vec, chunk(w_in)]
        args += [mix_gain, w_in]
        scratch.append(w_in)
        out_specs += [row(MIX_BF16_WIDTH), row(MIX_F32_WIDTH)]
        out_shape += [jax.ShapeDtypeStruct((n, MIX_BF16_WIDTH), BF16),
                      jax.ShapeDtypeStruct((n, MIX_F32_WIDTH), F32)]
    return pl.pallas_call(
        functools.partial(_ffn_kernel, mix=mixer_out is not None,
                          project=in_proj is not None, final_norm=final_norm),
        grid=(WEIGHT_LOAD_STEPS + n // tm,),
        in_specs=in_specs,
        out_specs=out_specs,
        out_shape=out_shape,
        scratch_shapes=[pltpu.VMEM(w.shape[1:], BF16) for w in scratch],
        compiler_params=_params(),
        name="ffn",
    )(*args)


def _alibi_slope(head):
    return 2.0 ** (-8.0 * (head + 1.0) / ATTN_HEADS)


def _attn_kernel(sink_ref, q_ref, kv_ref, kvp_ref, o_ref, bias_ref, *, seq_len, tile):
    first = (pl.program_id(0) * tile) % seq_len == 0
    cols = ATTN_GROUP * WINDOW
    c_idx = lax.broadcasted_iota(jnp.int32, (2 * WINDOW, cols), 0)
    log2e = math.log2(math.e)
    scale = log2e / math.sqrt(HEAD_DIM)

    @pl.when(pl.program_id(0) == 0)
    def _():
        g_idx = lax.broadcasted_iota(jnp.int32, (1, cols), 1) // WINDOW
        t_idx = lax.broadcasted_iota(jnp.int32, (2 * WINDOW, cols), 1) % WINDOW
        rel = t_idx + WINDOW - c_idx
        in_win = (rel >= 0) & (rel < WINDOW)
        relf = rel.astype(F32)
        for hk in range(ATTN_KV_HEADS):
            slope = jnp.zeros((1, cols), F32)
            sink = jnp.zeros((1, cols), F32)
            for g in range(ATTN_GROUP):
                head = hk * ATTN_GROUP + g
                slope = jnp.where(g_idx == g, _alibi_slope(head) * log2e, slope)
                sink = jnp.where(g_idx == g, sink_ref[head] * log2e, sink)
            bias = jnp.where(in_win, -(slope * relf), MASK_VALUE)
            bias_ref[hk] = jnp.where(c_idx == 0, sink, bias)

    keys = jnp.concatenate([kvp_ref[...], kv_ref[...]], axis=0).astype(F32)
    k_all = keys[:, 0:KV_WIDTH]
    v_t = jnp.transpose(keys[:, KV_WIDTH:2 * KV_WIDTH])
    krow = lax.broadcasted_iota(jnp.int32, (2 * WINDOW, HEAD_DIM), 0)
    vcol = lax.broadcasted_iota(jnp.int32, (HEAD_DIM, 2 * WINDOW), 1)
    ones_rows = jnp.ones((8, 2 * WINDOW), F32)
    no_prev = (c_idx >= 1) & (c_idx < jnp.where(first, WINDOW, 0))

    n_blocks = tile // WINDOW
    units = [(blk, hk) for blk in range(n_blocks) for hk in range(ATTN_KV_HEADS)]
    q_t = [jnp.transpose(q_ref[blk * WINDOW:(blk + 1) * WINDOW, :].astype(F32) * scale)
           .astype(BF16) for blk in range(n_blocks)]
    out_t = [[None] * ATTN_HEADS for _ in range(n_blocks)]

    def scores(blk, hk):
        slots = slice(blk * WINDOW, (blk + 2) * WINDOW)
        dims = slice(hk * HEAD_DIM, (hk + 1) * HEAD_DIM)
        kh = jnp.where(krow == 0, 0.0, k_all[slots, dims]).astype(BF16)
        qs = jnp.concatenate(
            [q_t[blk][h * HEAD_DIM:(h + 1) * HEAD_DIM, :]
             for h in range(hk * ATTN_GROUP, (hk + 1) * ATTN_GROUP)], axis=1)
        s = jnp.dot(kh, qs, preferred_element_type=F32) + bias_ref[hk]
        if blk == 0:
            s = jnp.where(no_prev, MASK_VALUE, s)
        return s

    def weighted_values(blk, hk, s):
        slots = slice(blk * WINDOW, (blk + 2) * WINDOW)
        dims = slice(hk * HEAD_DIM, (hk + 1) * HEAD_DIM)
        vh = jnp.where(vcol == 0, 0.0, v_t[dims, slots])
        vh = jnp.concatenate([vh, ones_rows], axis=0).astype(BF16)
        p = jnp.exp2(s - jnp.max(s, axis=0, keepdims=True)).astype(BF16)
        o = jnp.dot(vh, p, preferred_element_type=F32)
        o = o[:HEAD_DIM] * (1.0 / o[HEAD_DIM:HEAD_DIM + 1])
        for g in range(ATTN_GROUP):
            out_t[blk][hk * ATTN_GROUP + g] = o[:, g * WINDOW:(g + 1) * WINDOW]
        if hk == ATTN_KV_HEADS - 1:
            o_ref[blk * WINDOW:(blk + 1) * WINDOW, :] = jnp.transpose(
                jnp.concatenate(out_t[blk], axis=0))

    s = scores(*units[0])
    for i, unit in enumerate(units):
        s_next = scores(*units[i + 1]) if i + 1 < len(units) else None
        weighted_values(*unit, s)
        s = s_next


def _attention(u, sinks, seq_len):
    n = u.shape[0]
    tile = min(TOKEN_TILE, seq_len)
    blocks_per_tile = tile // WINDOW
    kv_col = ATTN_WIDTH // (2 * KV_WIDTH)
    return pl.pallas_call(
        functools.partial(_attn_kernel, seq_len=seq_len, tile=tile),
        grid=(n // tile,),
        in_specs=[
            pl.BlockSpec(memory_space=pltpu.SMEM),
            pl.BlockSpec((tile, ATTN_WIDTH), lambda i: (i, 0)),
            pl.BlockSpec((tile, 2 * KV_WIDTH), lambda i: (i, kv_col)),
            pl.BlockSpec((WINDOW, 2 * KV_WIDTH),
                         lambda i: (jnp.maximum(i * blocks_per_tile - 1, 0), kv_col)),
        ],
        out_specs=pl.BlockSpec((tile, ATTN_WIDTH), lambda i: (i, 0)),
        out_shape=jax.ShapeDtypeStruct((n, ATTN_WIDTH), F32),
        scratch_shapes=[pltpu.VMEM((ATTN_KV_HEADS, 2 * WINDOW, ATTN_GROUP * WINDOW), F32)],
        compiler_params=_params(),
        name="swa_attention",
    )(sinks, u, u, u)


HGRN_FACTORED_DECAY_LIMIT = 80.0


def _hgrn_exponent_matrix():
    c = HGRN_CHUNK
    r = jnp.arange(c)[:, None]
    j = jnp.arange(c)[None, :]
    blocks = []
    for lvl in range(HGRN_LEVELS):
        m = 1 << lvl
        start = (r // (2 * m)) * (2 * m)
        right = (r // m) % 2 == 1
        in_right = right & (j >= start + m) & (j <= r)
        in_left = (~right) & (j > r) & (j < start + m)
        blocks.append(in_right | in_left)
    blocks.append(j <= r)
    blocks.append(j > r)
    return jnp.concatenate(blocks, axis=0).astype(BF16)


def _split3(x):
    hi = x.astype(BF16)
    r1 = x - hi.astype(F32)
    mid = r1.astype(BF16)
    lo = (r1 - mid.astype(F32)).astype(BF16)
    return hi, mid, lo


def _hgrn_kernel(q_ref, f_ref, i_ref, g_ref, lbl_ref, gain_ref, em_ref, hm_ref, o_ref,
                 state_ref, qs_ref, ks_ref, lf_ref, bc_ref, *, seq_len, tile, layer):
    c = HGRN_CHUNK
    first = (pl.program_id(0) * tile) % seq_len == 0

    @pl.when(first)
    def _():
        state_ref[...] = jnp.zeros_like(state_ref)

    logits = lbl_ref[...]
    e = jnp.exp(logits - jnp.max(logits, axis=0, keepdims=True))
    sm = e / jnp.sum(e, axis=0, keepdims=True)
    lb = jnp.zeros((1, HGRN_WIDTH), F32)
    for l in range(1, layer + 1):
        lb = lb + sm[l:l + 1, :]

    r_idx = lax.broadcasted_iota(jnp.int32, (c, 1), 0)
    t_idx = lax.broadcasted_iota(jnp.int32, (c, c), 0)
    s_idx = lax.broadcasted_iota(jnp.int32, (c, c), 1)
    lane_head = lax.broadcasted_iota(jnp.int32, (c, HGRN_WIDTH), 1) // HGRN_DIM
    n_chunks = tile // c

    def chunk_rows(ci):
        return pl.ds(pl.multiple_of(ci * c, c), c)

    def prepare(ci, carry):
        rows = chunk_rows(ci)
        z = f_ref[rows, :]
        q = q_ref[rows, :].astype(F32)
        qs_ref[rows, :] = q * jax.nn.sigmoid(q)
        e = jnp.exp(-jnp.abs(z))
        r = 1.0 + e
        inv = 1.0 / r
        pos = z >= 0.0
        sig = jnp.where(pos, 1.0, e) * inv
        ks_ref[rows, :] = (1.0 - lb) * (jnp.where(pos, e, 1.0) * inv)
        lf = jnp.where(lb > 0.0, jnp.log(lb + (1.0 - lb) * sig),
                       jnp.minimum(z, 0.0) - jnp.log(r))
        lf_ref[rows, :] = lf
        tri = em_ref[HGRN_LEVELS * c:(HGRN_LEVELS + 1) * c, :]
        bc_ref[rows, :] = sum(jnp.dot(tri, p, preferred_element_type=F32)
                              for p in _split3(lf))
        return carry

    lax.fori_loop(0, n_chunks, prepare, 0, unroll=True)
    strong_decay = jnp.min(bc_ref[...]) < -HGRN_FACTORED_DECAY_LIMIT

    def stack_heads(x):
        return jnp.concatenate(
            [jnp.where(lane_head == h, x, 0.0) for h in range(HGRN_HEADS)],
            axis=0).astype(BF16)

    def chunk_scores(ci):
        rows = pl.ds(ci * c, c)
        bc = bc_ref[rows, :]
        qd = (qs_ref[rows, :] * jnp.exp(bc)).astype(BF16)
        kb = stack_heads(ks_ref[rows, :] * jnp.exp(-bc))
        att = lax.dot_general(qd, kb, (((1,), (1,)), ((), ())),
                              preferred_element_type=F32)
        col_s = lax.broadcasted_iota(jnp.int32, (c, HGRN_WIDTH), 1) % c
        row_t = lax.broadcasted_iota(jnp.int32, (c, HGRN_WIDTH), 0)
        return qd, jnp.where(col_s <= row_t, att, 0.0).astype(BF16)

    def chunk_output(ci, qd, att, state_t):
        rows = pl.ds(ci * c, c)
        k, v, bc = ks_ref[rows, :], i_ref[rows, :].astype(F32), bc_ref[rows, :]
        b_last = bc[c - 1:c, :]
        o = jnp.dot(att, stack_heads(v), preferred_element_type=F32)
        o = o + lax.dot_general(qd, state_t.astype(BF16), (((1,), (1,)), ((), ())),
                                preferred_element_type=F32)
        upd = lax.dot_general(v.astype(BF16), (k * jnp.exp(b_last - bc)).astype(BF16),
                              (((0,), (0,)), ((), ())), preferred_element_type=F32)
        rh = lax.broadcasted_iota(jnp.int32, (HGRN_WIDTH, HGRN_WIDTH), 0) // HGRN_DIM
        ch = lax.broadcasted_iota(jnp.int32, (HGRN_WIDTH, HGRN_WIDTH), 1) // HGRN_DIM
        o_ref[rows, :] = o
        return state_t * jnp.exp(b_last) + jnp.where(rh == ch, upd, 0.0)

    def factored_chunks():
        state_t = state_ref[...]
        scores = chunk_scores(0)
        for ci in range(n_chunks):
            scores_next = chunk_scores(ci + 1) if ci + 1 < n_chunks else None
            state_t = chunk_output(ci, *scores, state_t)
            scores = scores_next
        state_ref[...] = state_t

    def general_chunk(ci, carry):
        rows = chunk_rows(ci)
        q, k, v = qs_ref[rows, :], ks_ref[rows, :], i_ref[rows, :].astype(F32)
        expo = sum(jnp.dot(em_ref[...], p, preferred_element_type=F32)
                   for p in _split3(lf_ref[rows, :]))
        decay = jnp.exp(expo)
        outs = []
        for h in range(HGRN_HEADS):
            cols = slice(h * HGRN_DIM, (h + 1) * HGRN_DIM)
            qh, kh, vh = q[:, cols], k[:, cols], v[:, cols]
            att = jnp.where(t_idx == s_idx,
                            jnp.sum(qh * kh, axis=-1, keepdims=True), 0.0)
            for lvl in range(HGRN_LEVELS):
                m = 1 << lvl
                d = decay[lvl * c:(lvl + 1) * c, cols]
                right = (r_idx // m) % 2 == 1
                ql = jnp.where(right, qh * d, 0.0)
                kl = jnp.where(right, 0.0, kh * d)
                a = lax.dot_general(ql.astype(BF16), kl.astype(BF16),
                                    (((1,), (1,)), ((), ())),
                                    preferred_element_type=F32)
                same = (t_idx // (2 * m)) == (s_idx // (2 * m))
                att = att + jnp.where(same, a, 0.0)
            d_in = decay[HGRN_LEVELS * c:(HGRN_LEVELS + 1) * c, cols]
            d_out = decay[(HGRN_LEVELS + 1) * c:(HGRN_LEVELS + 2) * c, cols]
            state_t = state_ref[cols, cols]
            o = _bdot(att, vh) + lax.dot_general(
                (qh * d_in).astype(BF16), state_t.astype(BF16),
                (((1,), (1,)), ((), ())), preferred_element_type=F32)
            upd_t = lax.dot_general(vh.astype(BF16), (kh * d_out).astype(BF16),
                                    (((0,), (0,)), ((), ())),
                                    preferred_element_type=F32)
            state_ref[cols, cols] = state_t * d_in[c - 1:c, :] + upd_t
            outs.append(o)
        o_ref[rows, :] = jnp.concatenate(outs, axis=-1)
        return carry

    @pl.when(jnp.logical_not(strong_decay))
    def _():
        factored_chunks()

    @pl.when(strong_decay)
    def _():
        lax.fori_loop(0, n_chunks, general_chunk, 0)

    o = o_ref[...]
    sq = sum(jnp.dot(p, hm_ref[...], preferred_element_type=F32) for p in _split3(o * o))
    inv = lax.rsqrt(sq * (1.0 / HGRN_DIM) + EPS)
    gate = g_ref[...].astype(F32)
    o_ref[...] = o * inv * gain_ref[...] * (gate * jax.nn.sigmoid(gate))


def _hgrn(u_bf16, u_f32, lb_logits, gain, layer, seq_len):
    n = u_bf16.shape[0]
    tile = min(TOKEN_TILE, seq_len)
    depth = lb_logits.shape[0]
    col = lambda j: pl.BlockSpec((tile, HGRN_WIDTH), lambda i: (i, j))
    q_col = (ATTN_WIDTH + 2 * KV_WIDTH) // HGRN_WIDTH
    em = _hgrn_exponent_matrix()
    channel_head = jnp.arange(HGRN_WIDTH) // HGRN_DIM
    head_mask = (channel_head[:, None] == channel_head[None, :]).astype(BF16)
    tile_f32 = pltpu.VMEM((tile, HGRN_WIDTH), F32)
    return pl.pallas_call(
        functools.partial(_hgrn_kernel, seq_len=seq_len, tile=tile, layer=layer),
        grid=(n // tile,),
        in_specs=[col(q_col), col(0), col(q_col + 1), col(q_col + 2),
                  _full((depth, HGRN_WIDTH)), _full((1, HGRN_WIDTH)),
                  _full(em.shape), _full(head_mask.shape)],
        out_specs=pl.BlockSpec((tile, HGRN_WIDTH), lambda i: (i, 0)),
        out_shape=jax.ShapeDtypeStruct((n, HGRN_WIDTH), F32),
        scratch_shapes=[pltpu.VMEM((HGRN_WIDTH, HGRN_WIDTH), F32),
                        tile_f32, tile_f32, tile_f32, tile_f32],
        compiler_params=_params(),
        name="hgrn2",
    )(u_bf16, u_f32, u_bf16, u_bf16, lb_logits, gain, em, head_mask)


def _s5_prep_kernel(ar_ref, ai_ref, ldt_ref, br_ref, bi_ref,
                    abr_ref, abi_ref, bbr_ref, bbi_ref):
    ar, ai = ar_ref[...], ai_ref[...]
    dt = jnp.exp(ldt_ref[...])
    mag = jnp.exp(ar * dt)
    abar_re = mag * jnp.cos(ai * dt)
    abar_im = mag * jnp.sin(ai * dt)
    nr, ni = abar_re - 1.0, abar_im
    den = ar * ar + ai * ai
    z_re = (nr * ar + ni * ai) / den
    z_im = (ni * ar - nr * ai) / den
    abr_ref[...] = abar_re
    abi_ref[...] = abar_im
    br, bi = br_ref[...], bi_ref[...]
    bbr_ref[...] = z_re[:, None, :] * br - z_im[:, None, :] * bi
    bbi_ref[...] = z_re[:, None, :] * bi + z_im[:, None, :] * br


def _s5_prep(a_re, a_im, log_dt, b_re, b_im):
    gp = jax.ShapeDtypeStruct((S5_GROUPS, S5_STATE), F32)
    gcp = jax.ShapeDtypeStruct((S5_GROUPS, S5_GROUP_CH, S5_STATE), F32)
    return pl.pallas_call(
        _s5_prep_kernel,
        out_shape=(gp, gp, gcp, gcp),
        name="s5_discretize",
    )(a_re, a_im, log_dt[:, None], jnp.swapaxes(b_re, 1, 2), jnp.swapaxes(b_im, 1, 2))


def _block_diag(blocks):
    g, r, c = blocks.shape
    eye = jnp.eye(g, dtype=blocks.dtype)
    return (blocks[:, :, None, :] * eye[:, None, :, None]).reshape(g * r, g * c)


def _s5_kernel(u_ref, a_ref, b_ref, c_ref, d_ref, gw_ref, gb_ref, o_ref,
               xa_ref, xb_ref, ua_ref, ub_ref, state_ref, *, batch, steps):
    step = pl.program_id(0)

    @pl.when(step == 0)
    def _():
        state_ref[...] = jnp.zeros_like(state_ref)
        xb_ref[...] = jnp.zeros_like(xb_ref)
        ub_ref[...] = jnp.zeros_like(ub_ref)

    a_re = jnp.broadcast_to(a_ref[0:1, :], (batch, S5_STATES))
    a_im = jnp.broadcast_to(a_ref[1:2, :], (batch, S5_STATES))

    def pipeline_step(x_new, u_new, x_cur, u_cur):
        u = jnp.swapaxes(u_ref[...], 0, 1).reshape(steps * batch, S5_WIDTH)
        u_new[...] = u
        x_new[...] = _bdot(u, b_ref[...])

        x_re, x_im = state_ref[:, 0:S5_STATES], state_ref[:, S5_STATES:2 * S5_STATES]
        for t in range(steps):
            rows = slice(t * batch, (t + 1) * batch)
            bu_re = x_cur[rows, 0:S5_STATES]
            bu_im = x_cur[rows, S5_STATES:2 * S5_STATES]
            x_re, x_im = (a_re * x_re - a_im * x_im + bu_re,
                          a_re * x_im + a_im * x_re + bu_im)
            x_cur[rows, 0:S5_STATES] = x_re
            x_cur[rows, S5_STATES:2 * S5_STATES] = x_im
        state_ref[:, 0:S5_STATES] = x_re
        state_ref[:, S5_STATES:2 * S5_STATES] = x_im

        y = _bdot(x_cur[...], c_ref[...]) + d_ref[...] * u_cur[...]
        z = jax.nn.gelu(y)
        out = z * jax.nn.sigmoid(_bdot(z, gw_ref[...]) + gb_ref[...])
        o_ref[...] = jnp.swapaxes(out.reshape(steps, batch, S5_WIDTH), 0, 1)

    @pl.when(step % 2 == 0)
    def _():
        pipeline_step(xa_ref, ua_ref, xb_ref, ub_ref)

    @pl.when(step % 2 == 1)
    def _():
        pipeline_step(xb_ref, ub_ref, xa_ref, ua_ref)


def _s5(u, abar, b_blk, c_blk, d_skip, glu_w, glu_b, batch, seq_len):
    steps = min(S5_TIME_TILE, seq_len)
    rows = steps * batch
    n_tiles = seq_len // steps
    u_col = MIX_F32_WIDTH // S5_WIDTH - 1
    x_tile = pltpu.VMEM((rows, 2 * S5_STATES), F32)
    u_tile = pltpu.VMEM((rows, S5_WIDTH), F32)
    return pl.pallas_call(
        functools.partial(_s5_kernel, batch=batch, steps=steps),
        grid=(n_tiles + 1,),
        in_specs=[
            pl.BlockSpec((batch, steps, S5_WIDTH),
                         lambda i: (0, jnp.minimum(i, n_tiles - 1), u_col)),
            _full((2, S5_STATES)),
            _full((S5_WIDTH, 2 * S5_STATES)),
            _full((2 * S5_STATES, S5_WIDTH)),
            _full((1, S5_WIDTH)),
            _full((S5_WIDTH, S5_WIDTH)),
            _full((1, S5_WIDTH)),
        ],
        out_specs=pl.BlockSpec((batch, steps, S5_WIDTH),
                               lambda i: (0, jnp.maximum(i - 1, 0), 0)),
        out_shape=jax.ShapeDtypeStruct((batch, seq_len, S5_WIDTH), F32),
        scratch_shapes=[x_tile, x_tile, u_tile, u_tile,
                        pltpu.VMEM((batch, 2 * S5_STATES), F32)],
        compiler_params=_params(),
        name="s5_ssm",
    )(u, abar, b_blk, c_blk, d_skip, glu_w, glu_b)


def _s5_mixer(u, batch, seq_len, a_re, a_im, log_dt, b_re, b_im, c_re, c_im, d_skip,
              glu_w, glu_b):
    abar_re, abar_im, bbar_re, bbar_im = _s5_prep(a_re, a_im, log_dt, b_re, b_im)
    abar = jnp.stack([abar_re.reshape(-1), abar_im.reshape(-1)])
    b_blk = jnp.concatenate([_block_diag(bbar_re), _block_diag(bbar_im)], axis=1)
    c_blk = jnp.concatenate([_block_diag(jnp.swapaxes(c_re, 1, 2)),
                             -_block_diag(jnp.swapaxes(c_im, 1, 2))], axis=0)
    y = _s5(u.reshape(batch, seq_len, MIX_F32_WIDTH), abar, b_blk.astype(BF16),
            c_blk.astype(BF16), d_skip.reshape(1, S5_WIDTH), glu_w.astype(BF16),
            glu_b.reshape(1, S5_WIDTH), batch, seq_len)
    return y.reshape(batch * seq_len, S5_WIDTH)


def kernel(x, norm_ffn1, ffn1_w_gate, ffn1_w_up, ffn1_w_down, norm_mix, w_in, attn_sinks,
           hgrn_lb_logits, hgrn_norm, s5_a_re, s5_a_im, s5_log_dt, s5_b_re, s5_b_im,
           s5_c_re, s5_c_im, s5_d, s5_glu_w, s5_glu_b, w_out, norm_ffn2, ffn2_w_gate,
           ffn2_w_up, ffn2_w_down, norm_final):
    batch, seq_len, _ = x.shape
    depth = w_in.shape[0]
    h = x.reshape(batch * seq_len, D_MODEL)
    row = lambda v: v.reshape(1, -1)
    final_gain = row(norm_final)
    for layer in range(depth):
        h, u_bf16, u_f32 = _ffn(
            h, None, (row(norm_mix[layer]), w_in), row(norm_ffn1[layer]), ffn1_w_gate,
            ffn1_w_up, ffn1_w_down, final_gain, False, layer, seq_len)
        y_a = _attention(u_bf16, attn_sinks[layer], seq_len)
        y_b = _hgrn(u_bf16, u_f32, hgrn_lb_logits, row(hgrn_norm[layer]), layer, seq_len)
        y_c = _s5_mixer(u_f32, batch, seq_len, s5_a_re[layer], s5_a_im[layer],
                        s5_log_dt[layer], s5_b_re[layer], s5_b_im[layer], s5_c_re[layer],
                        s5_c_im[layer], s5_d[layer], s5_glu_w[layer], s5_glu_b[layer])
        (h,) = _ffn(h, (y_a, y_b, y_c, w_out), None, row(norm_ffn2[layer]), ffn2_w_gate,
                    ffn2_w_up, ffn2_w_down, final_gain, layer == depth - 1, layer, seq_len)
    return h.reshape(batch, seq_len, D_MODEL)
```

```python
import functools
import math

import jax
import jax.numpy as jnp
from jax import lax
from jax.experimental import pallas as pl
from jax.experimental.pallas import tpu as pltpu

D_MODEL = 1024
D_FF = 2816
EPS = 1e-6

ATTN_HEADS = 8
ATTN_KV_HEADS = 2
ATTN_GROUP = ATTN_HEADS // ATTN_KV_HEADS
HEAD_DIM = 64
WINDOW = 128
ATTN_WIDTH = ATTN_HEADS * HEAD_DIM
KV_WIDTH = ATTN_KV_HEADS * HEAD_DIM

HGRN_HEADS = 4
HGRN_DIM = 64
HGRN_WIDTH = HGRN_HEADS * HGRN_DIM
HGRN_CHUNK = 64
HGRN_LEVELS = 6

S5_GROUPS = 16
S5_GROUP_CH = 16
S5_STATE = 64
S5_WIDTH = S5_GROUPS * S5_GROUP_CH
S5_STATES = S5_GROUPS * S5_STATE

IN_PROJ_WIDTH = ATTN_WIDTH + 2 * KV_WIDTH + 4 * HGRN_WIDTH + S5_WIDTH
_F_B_START = ATTN_WIDTH + 2 * KV_WIDTH + HGRN_WIDTH
_U_C_START = IN_PROJ_WIDTH - S5_WIDTH
MIX_BF16_COLUMNS = ((0, _F_B_START), (_F_B_START + HGRN_WIDTH, _U_C_START))
MIX_F32_COLUMNS = ((_F_B_START, _F_B_START + HGRN_WIDTH), (_U_C_START, IN_PROJ_WIDTH))
MIX_BF16_WIDTH = sum(b - a for a, b in MIX_BF16_COLUMNS)
MIX_F32_WIDTH = sum(b - a for a, b in MIX_F32_COLUMNS)

VMEM_LIMIT_BYTES = 56 * 1024 * 1024

TOKEN_TILE = 512
S5_TIME_TILE = 128
MASK_VALUE = -1e30

F32 = jnp.float32
BF16 = jnp.bfloat16


def _rms(x, gain):
    return x * lax.rsqrt(jnp.mean(x * x, axis=-1, keepdims=True) + EPS) * gain


def _bdot(a, b):
    return jnp.dot(a.astype(BF16), b.astype(BF16), preferred_element_type=F32)


def _params(n_axes=1):
    return pltpu.CompilerParams(
        dimension_semantics=("arbitrary",) * n_axes,
        vmem_limit_bytes=VMEM_LIMIT_BYTES,
    )


def _full(shape):
    return pl.BlockSpec(shape, lambda i: (0,) * len(shape))


WEIGHT_LOAD_STEPS = 8


def _token_tile(step):
    return jnp.maximum(step - WEIGHT_LOAD_STEPS, 0)


def _weight_chunk_spec(w, layer):
    _, rows, cols = w.shape
    return pl.BlockSpec(
        (None, rows // WEIGHT_LOAD_STEPS, cols),
        lambda i: (layer, jnp.minimum(i, WEIGHT_LOAD_STEPS - 1), 0))


def _keep_weight_chunk(step, w_ref, w_s):
    chunk = w_ref.shape[0]
    w_s[pl.ds(pl.multiple_of(step * chunk, chunk), chunk), :] = w_ref[...].astype(BF16)


def _ffn_kernel(*refs, mix, project, final_norm):
    refs = list(refs)
    take = lambda n: [refs.pop(0) for _ in range(n)]
    (x_ref,) = take(1)
    if mix:
        ya_ref, yb_ref, yc_ref, wo_ref = take(4)
    g_ref, wg_ref, wu_ref, wd_ref, gf_ref = take(5)
    if project:
        gm_ref, wi_ref = take(2)
    (o_ref,) = take(1)
    if project:
        ub_ref, uf_ref = take(2)
    if mix:
        (wo_s,) = take(1)
    wg_s, wu_s, wd_s = take(3)
    streamed = [(wg_ref, wg_s), (wu_ref, wu_s), (wd_ref, wd_s)]
    if mix:
        streamed.append((wo_ref, wo_s))
    if project:
        (wi_s,) = take(1)
        streamed.append((wi_ref, wi_s))
    step = pl.program_id(0)

    @pl.when(step < WEIGHT_LOAD_STEPS)
    def _():
        for w_ref, w_s in streamed:
            _keep_weight_chunk(step, w_ref, w_s)

    @pl.when(step >= WEIGHT_LOAD_STEPS)
    def _():
        x = x_ref[...]
        if mix:
            y = jnp.concatenate([ya_ref[...], yb_ref[...], yc_ref[...]], axis=-1)
            x = x + _bdot(y, wo_s[...])
        h = _rms(x, g_ref[...]).astype(BF16)
        gate = jnp.dot(h, wg_s[...], preferred_element_type=F32)
        up = jnp.dot(h, wu_s[...], preferred_element_type=F32)
        act = (gate * jax.nn.sigmoid(gate) * up).astype(BF16)
        y = x + 0.5 * jnp.dot(act, wd_s[...], preferred_element_type=F32)
        if final_norm:
            y = _rms(y, gf_ref[...])
        o_ref[...] = y
        if project:
            u = jnp.dot(_rms(y, gm_ref[...]).astype(BF16), wi_s[...],
                        preferred_element_type=F32)
            for dst, columns in ((ub_ref, MIX_BF16_COLUMNS), (uf_ref, MIX_F32_COLUMNS)):
                dst[...] = jnp.concatenate([u[:, a:b] for a, b in columns],
                                           axis=-1).astype(dst.dtype)


def _ffn(x, mixer_out, in_proj, gain, w_gate, w_up, w_down, final_gain, final_norm,
         layer, seq_len):
    n = x.shape[0]
    tm = min(TOKEN_TILE, seq_len)
    row = lambda width: pl.BlockSpec((tm, width), lambda i: (_token_tile(i), 0))
    vec = _full((1, D_MODEL))
    chunk = lambda w: _weight_chunk_spec(w, layer)
    ffn_weights = [w_gate, w_up, w_down]
    in_specs, args = [row(D_MODEL)], [x]
    out_specs = [row(D_MODEL)]
    out_shape = [jax.ShapeDtypeStruct((n, D_MODEL), F32)]
    scratch = list(ffn_weights)
    if mixer_out is not None:
        y_a, y_b, y_c, w_out = mixer_out
        in_specs += [row(ATTN_WIDTH), row(HGRN_WIDTH), row(S5_WIDTH), chunk(w_out)]
        args += [y_a, y_b, y_c, w_out]
        scratch.insert(0, w_out)
    in_specs += [vec] + [chunk(w) for w in ffn_weights] + [vec]
    args += [gain, *ffn_weights, final_gain]
    if in_proj is not None:
        mix_gain, w_in = in_proj
        in_specs += [vec, chunk(w_in)]
        args += [mix_gain, w_in]
        scratch.append(w_in)
        out_specs += [row(MIX_BF16_WIDTH), row(MIX_F32_WIDTH)]
        out_shape += [jax.ShapeDtypeStruct((n, MIX_BF16_WIDTH), BF16),
                      jax.ShapeDtypeStruct((n, MIX_F32_WIDTH), F32)]
    return pl.pallas_call(
        functools.partial(_ffn_kernel, mix=mixer_out is not None,
                          project=in_proj is not None, final_norm=final_norm),
        grid=(WEIGHT_LOAD_STEPS + n // tm,),
        in_specs=in_specs,
        out_specs=out_specs,
        out_shape=out_shape,
        scratch_shapes=[pltpu.VMEM(w.shape[1:], BF16) for w in scratch],
        compiler_params=_params(),
        name="ffn",
    )(*args)


def _alibi_slope(head):
    return 2.0 ** (-8.0 * (head + 1.0) / ATTN_HEADS)


def _attn_kernel(sink_ref, q_ref, kv_ref, kvp_ref, o_ref, bias_ref, *, seq_len, tile):
    first = (pl.program_id(0) * tile) % seq_len == 0
    cols = ATTN_GROUP * WINDOW
    c_idx = lax.broadcasted_iota(jnp.int32, (2 * WINDOW, cols), 0)
    log2e = math.log2(math.e)
    scale = log2e / math.sqrt(HEAD_DIM)

    @pl.when(pl.program_id(0) == 0)
    def _():
        g_idx = lax.broadcasted_iota(jnp.int32, (1, cols), 1) // WINDOW
        t_idx = lax.broadcasted_iota(jnp.int32, (2 * WINDOW, cols), 1) % WINDOW
        rel = t_idx + WINDOW - c_idx
        in_win = (rel >= 0) & (rel < WINDOW)
        relf = rel.astype(F32)
        for hk in range(ATTN_KV_HEADS):
            slope = jnp.zeros((1, cols), F32)
            sink = jnp.zeros((1, cols), F32)
            for g in range(ATTN_GROUP):
                head = hk * ATTN_GROUP + g
                slope = jnp.where(g_idx == g, _alibi_slope(head) * log2e, slope)
                sink = jnp.where(g_idx == g, sink_ref[head] * log2e, sink)
            bias = jnp.where(in_win, -(slope * relf), MASK_VALUE)
            bias_ref[hk] = jnp.where(c_idx == 0, sink, bias)

    keys = jnp.concatenate([kvp_ref[...], kv_ref[...]], axis=0).astype(F32)
    k_all = keys[:, 0:KV_WIDTH]
    v_t = jnp.transpose(keys[:, KV_WIDTH:2 * KV_WIDTH])
    krow = lax.broadcasted_iota(jnp.int32, (2 * WINDOW, HEAD_DIM), 0)
    vcol = lax.broadcasted_iota(jnp.int32, (HEAD_DIM, 2 * WINDOW), 1)
    ones_rows = jnp.ones((8, 2 * WINDOW), F32)
    no_prev = (c_idx >= 1) & (c_idx < jnp.where(first, WINDOW, 0))

    n_blocks = tile // WINDOW
    units = [(blk, hk) for blk in range(n_blocks) for hk in range(ATTN_KV_HEADS)]
    q_t = [jnp.transpose(q_ref[blk * WINDOW:(blk + 1) * WINDOW, :].astype(F32) * scale)
           .astype(BF16) for blk in range(n_blocks)]
    out_t = [[None] * ATTN_HEADS for _ in range(n_blocks)]

    def scores(blk, hk):
        slots = slice(blk * WINDOW, (blk + 2) * WINDOW)
        dims = slice(hk * HEAD_DIM, (hk + 1) * HEAD_DIM)
        kh = jnp.where(krow == 0, 0.0, k_all[slots, dims]).astype(BF16)
        qs = jnp.concatenate(
            [q_t[blk][h * HEAD_DIM:(h + 1) * HEAD_DIM, :]
             for h in range(hk * ATTN_GROUP, (hk + 1) * ATTN_GROUP)], axis=1)
        s = jnp.dot(kh, qs, preferred_element_type=F32) + bias_ref[hk]
        if blk == 0:
            s = jnp.where(no_prev, MASK_VALUE, s)
        return s

    def weighted_values(blk, hk, s):
        slots = slice(blk * WINDOW, (blk + 2) * WINDOW)
        dims = slice(hk * HEAD_DIM, (hk + 1) * HEAD_DIM)
        vh = jnp.where(vcol == 0, 0.0, v_t[dims, slots])
        vh = jnp.concatenate([vh, ones_rows], axis=0).astype(BF16)
        p = jnp.exp2(s - jnp.max(s, axis=0, keepdims=True)).astype(BF16)
        o = jnp.dot(vh, p, preferred_element_type=F32)
        o = o[:HEAD_DIM] * (1.0 / o[HEAD_DIM:HEAD_DIM + 1])
        for g in range(ATTN_GROUP):
            out_t[blk][hk * ATTN_GROUP + g] = o[:, g * WINDOW:(g + 1) * WINDOW]
        if hk == ATTN_KV_HEADS - 1:
            o_ref[blk * WINDOW:(blk + 1) * WINDOW, :] = jnp.transpose(
                jnp.concatenate(out_t[blk], axis=0))

    s = scores(*units[0])
    for i, unit in enumerate(units):
        s_next = scores(*units[i + 1]) if i + 1 < len(units) else None
        weighted_values(*unit, s)
        s = s_next


def _attention(u, sinks, seq_len):
    n = u.shape[0]
    tile = min(TOKEN_TILE, seq_len)
    blocks_per_tile = tile // WINDOW
    kv_col = ATTN_WIDTH // (2 * KV_WIDTH)
    return pl.pallas_call(
        functools.partial(_attn_kernel, seq_len=seq_len, tile=tile),
        grid=(n // tile,),
        in_specs=[
            pl.BlockSpec(memory_space=pltpu.SMEM),
            pl.BlockSpec((tile, ATTN_WIDTH), lambda i: (i, 0)),
            pl.BlockSpec((tile, 2 * KV_WIDTH), lambda i: (i, kv_col)),
            pl.BlockSpec((WINDOW, 2 * KV_WIDTH),
                         lambda i: (jnp.maximum(i * blocks_per_tile - 1, 0), kv_col)),
        ],
        out_specs=pl.BlockSpec((tile, ATTN_WIDTH), lambda i: (i, 0)),
        out_shape=jax.ShapeDtypeStruct((n, ATTN_WIDTH), F32),
        scratch_shapes=[pltpu.VMEM((ATTN_KV_HEADS, 2 * WINDOW, ATTN_GROUP * WINDOW), F32)],
        compiler_params=_params(),
        name="swa_attention",
    )(sinks, u, u, u)


HGRN_FACTORED_DECAY_LIMIT = 80.0


def _hgrn_exponent_matrix():
    c = HGRN_CHUNK
    r = jnp.arange(c)[:, None]
    j = jnp.arange(c)[None, :]
    blocks = []
    for lvl in range(HGRN_LEVELS):
        m = 1 << lvl
        start = (r // (2 * m)) * (2 * m)
        right = (r // m) % 2 == 1
        in_right = right & (j >= start + m) & (j <= r)
        in_left = (~right) & (j > r) & (j < start + m)
        blocks.append(in_right | in_left)
    blocks.append(j <= r)
    blocks.append(j > r)
    return jnp.concatenate(blocks, axis=0).astype(BF16)


def _split3(x):
    hi = x.astype(BF16)
    r1 = x - hi.astype(F32)
    mid = r1.astype(BF16)
    lo = (r1 - mid.astype(F32)).astype(BF16)
    return hi, mid, lo


def _hgrn_kernel(q_ref, f_ref, i_ref, g_ref, lbl_ref, gain_ref, em_ref, hm_ref, o_ref,
                 state_ref, state0_ref, qs_ref, ks_ref, lf_ref, bc_ref, *,
                 seq_len, tile, layer):
    c = HGRN_CHUNK
    first = (pl.program_id(0) * tile) % seq_len == 0

    @pl.when(first)
    def _():
        state_ref[...] = jnp.zeros_like(state_ref)

    logits = lbl_ref[...]
    e = jnp.exp(logits - jnp.max(logits, axis=0, keepdims=True))
    sm = e / jnp.sum(e, axis=0, keepdims=True)
    lb = jnp.zeros((1, HGRN_WIDTH), F32)
    for l in range(1, layer + 1):
        lb = lb + sm[l:l + 1, :]

    r_idx = lax.broadcasted_iota(jnp.int32, (c, 1), 0)
    t_idx = lax.broadcasted_iota(jnp.int32, (c, c), 0)
    s_idx = lax.broadcasted_iota(jnp.int32, (c, c), 1)
    lane_head = lax.broadcasted_iota(jnp.int32, (c, HGRN_WIDTH), 1) // HGRN_DIM
    n_chunks = tile // c

    def chunk_rows(ci):
        return pl.ds(pl.multiple_of(ci * c, c), c)

    def prepare(ci):
        rows = pl.ds(ci * c, c)
        z = f_ref[rows, :]
        q = q_ref[rows, :].astype(F32)
        qs_ref[rows, :] = q * jax.nn.sigmoid(q)
        e = jnp.exp(-jnp.abs(z))
        r = 1.0 + e
        inv = 1.0 / r
        pos = z >= 0.0
        sig = jnp.where(pos, 1.0, e) * inv
        ks_ref[rows, :] = (1.0 - lb) * (jnp.where(pos, e, 1.0) * inv)
        lf = jnp.where(lb > 0.0, jnp.log(lb + (1.0 - lb) * sig),
                       jnp.minimum(z, 0.0) - jnp.log(r))
        lf_ref[rows, :] = lf
        tri = em_ref[HGRN_LEVELS * c:(HGRN_LEVELS + 1) * c, :]
        bc_ref[rows, :] = sum(jnp.dot(tri, p, preferred_element_type=F32)
                              for p in _split3(lf))

    def stack_heads(x):
        return jnp.concatenate(
            [jnp.where(lane_head == h, x, 0.0) for h in range(HGRN_HEADS)],
            axis=0).astype(BF16)

    def chunk_scores(ci):
        rows = pl.ds(ci * c, c)
        bc = bc_ref[rows, :]
        qd = (qs_ref[rows, :] * jnp.exp(bc)).astype(BF16)
        kb = stack_heads(ks_ref[rows, :] * jnp.exp(-bc))
        att = lax.dot_general(qd, kb, (((1,), (1,)), ((), ())),
                              preferred_element_type=F32)
        col_s = lax.broadcasted_iota(jnp.int32, (c, HGRN_WIDTH), 1) % c
        row_t = lax.broadcasted_iota(jnp.int32, (c, HGRN_WIDTH), 0)
        return qd, jnp.where(col_s <= row_t, att, 0.0).astype(BF16)

    def chunk_output(ci, qd, att, state_t):
        rows = pl.ds(ci * c, c)
        k, v, bc = ks_ref[rows, :], i_ref[rows, :].astype(F32), bc_ref[rows, :]
        b_last = bc[c - 1:c, :]
        o = jnp.dot(att, stack_heads(v), preferred_element_type=F32)
        o = o + lax.dot_general(qd, state_t.astype(BF16), (((1,), (1,)), ((), ())),
                                preferred_element_type=F32)
        upd = lax.dot_general(v.astype(BF16), (k * jnp.exp(b_last - bc)).astype(BF16),
                              (((0,), (0,)), ((), ())), preferred_element_type=F32)
        rh = lax.broadcasted_iota(jnp.int32, (HGRN_WIDTH, HGRN_WIDTH), 0) // HGRN_DIM
        ch = lax.broadcasted_iota(jnp.int32, (HGRN_WIDTH, HGRN_WIDTH), 1) // HGRN_DIM
        o_ref[rows, :] = o
        return state_t * jnp.exp(b_last) + jnp.where(rh == ch, upd, 0.0)

    def general_chunk(ci, carry):
        rows = chunk_rows(ci)
        q, k, v = qs_ref[rows, :], ks_ref[rows, :], i_ref[rows, :].astype(F32)
        expo = sum(jnp.dot(em_ref[...], p, preferred_element_type=F32)
                   for p in _split3(lf_ref[rows, :]))
        decay = jnp.exp(expo)
        outs = []
        for h in range(HGRN_HEADS):
            cols = slice(h * HGRN_DIM, (h + 1) * HGRN_DIM)
            qh, kh, vh = q[:, cols], k[:, cols], v[:, cols]
            att = jnp.where(t_idx == s_idx,
                            jnp.sum(qh * kh, axis=-1, keepdims=True), 0.0)
            for lvl in range(HGRN_LEVELS):
                m = 1 << lvl
                d = decay[lvl * c:(lvl + 1) * c, cols]
                right = (r_idx // m) % 2 == 1
                ql = jnp.where(right, qh * d, 0.0)
                kl = jnp.where(right, 0.0, kh * d)
                a = lax.dot_general(ql.astype(BF16), kl.astype(BF16),
                                    (((1,), (1,)), ((), ())),
                                    preferred_element_type=F32)
                same = (t_idx // (2 * m)) == (s_idx // (2 * m))
                att = att + jnp.where(same, a, 0.0)
            d_in = decay[HGRN_LEVELS * c:(HGRN_LEVELS + 1) * c, cols]
            d_out = decay[(HGRN_LEVELS + 1) * c:(HGRN_LEVELS + 2) * c, cols]
            state_t = state_ref[cols, cols]
            o = _bdot(att, vh) + lax.dot_general(
                (qh * d_in).astype(BF16), state_t.astype(BF16),
                (((1,), (1,)), ((), ())), preferred_element_type=F32)
            upd_t = lax.dot_general(vh.astype(BF16), (kh * d_out).astype(BF16),
                                    (((0,), (0,)), ((), ())),
                                    preferred_element_type=F32)
            state_ref[cols, cols] = state_t * d_in[c - 1:c, :] + upd_t
            outs.append(o)
        o_ref[rows, :] = jnp.concatenate(outs, axis=-1)
        return carry

    def finish_tile():
        o = o_ref[...]
        sq = sum(jnp.dot(p, hm_ref[...], preferred_element_type=F32)
                 for p in _split3(o * o))
        inv = lax.rsqrt(sq * (1.0 / HGRN_DIM) + EPS)
        gate = g_ref[...].astype(F32)
        o_ref[...] = o * inv * gain_ref[...] * (gate * jax.nn.sigmoid(gate))

    state0_ref[...] = state_ref[...]
    prepare(0)
    if n_chunks > 1:
        prepare(1)
    state_t = state_ref[...]
    scores = chunk_scores(0)
    for ci in range(n_chunks):
        if ci + 2 < n_chunks:
            prepare(ci + 2)
        scores_next = chunk_scores(ci + 1) if ci + 1 < n_chunks else None
        state_t = chunk_output(ci, *scores, state_t)
        scores = scores_next
    state_ref[...] = state_t
    finish_tile()

    @pl.when(jnp.min(bc_ref[...]) < -HGRN_FACTORED_DECAY_LIMIT)
    def _():
        state_ref[...] = state0_ref[...]
        lax.fori_loop(0, n_chunks, general_chunk, 0)
        finish_tile()


def _hgrn(u_bf16, u_f32, lb_logits, gain, layer, seq_len):
    n = u_bf16.shape[0]
    tile = min(TOKEN_TILE, seq_len)
    depth = lb_logits.shape[0]
    col = lambda j: pl.BlockSpec((tile, HGRN_WIDTH), lambda i: (i, j))
    q_col = (ATTN_WIDTH + 2 * KV_WIDTH) // HGRN_WIDTH
    em = _hgrn_exponent_matrix()
    channel_head = jnp.arange(HGRN_WIDTH) // HGRN_DIM
    head_mask = (channel_head[:, None] == channel_head[None, :]).astype(BF16)
    tile_f32 = pltpu.VMEM((tile, HGRN_WIDTH), F32)
    return pl.pallas_call(
        functools.partial(_hgrn_kernel, seq_len=seq_len, tile=tile, layer=layer),
        grid=(n // tile,),
        in_specs=[col(q_col), col(0), col(q_col + 1), col(q_col + 2),
                  _full((depth, HGRN_WIDTH)), _full((1, HGRN_WIDTH)),
                  _full(em.shape), _full(head_mask.shape)],
        out_specs=pl.BlockSpec((tile, HGRN_WIDTH), lambda i: (i, 0)),
        out_shape=jax.ShapeDtypeStruct((n, HGRN_WIDTH), F32),
        scratch_shapes=[pltpu.VMEM((HGRN_WIDTH, HGRN_WIDTH), F32),
                        pltpu.VMEM((HGRN_WIDTH, HGRN_WIDTH), F32),
                        tile_f32, tile_f32, tile_f32, tile_f32],
        compiler_params=_params(),
        name="hgrn2",
    )(u_bf16, u_f32, u_bf16, u_bf16, lb_logits, gain, em, head_mask)


def _s5_prep_kernel(ar_ref, ai_ref, ldt_ref, br_ref, bi_ref,
                    abr_ref, abi_ref, bbr_ref, bbi_ref):
    ar, ai = ar_ref[...], ai_ref[...]
    dt = jnp.exp(ldt_ref[...])
    mag = jnp.exp(ar * dt)
    abar_re = mag * jnp.cos(ai * dt)
    abar_im = mag * jnp.sin(ai * dt)
    nr, ni = abar_re - 1.0, abar_im
    den = ar * ar + ai * ai
    z_re = (nr * ar + ni * ai) / den
    z_im = (ni * ar - nr * ai) / den
    abr_ref[...] = abar_re
    abi_ref[...] = abar_im
    br, bi = br_ref[...], bi_ref[...]
    bbr_ref[...] = z_re[:, None, :] * br - z_im[:, None, :] * bi
    bbi_ref[...] = z_re[:, None, :] * bi + z_im[:, None, :] * br


def _s5_prep(a_re, a_im, log_dt, b_re, b_im):
    gp = jax.ShapeDtypeStruct((S5_GROUPS, S5_STATE), F32)
    gcp = jax.ShapeDtypeStruct((S5_GROUPS, S5_GROUP_CH, S5_STATE), F32)
    return pl.pallas_call(
        _s5_prep_kernel,
        out_shape=(gp, gp, gcp, gcp),
        name="s5_discretize",
    )(a_re, a_im, log_dt[:, None], jnp.swapaxes(b_re, 1, 2), jnp.swapaxes(b_im, 1, 2))


def _block_diag(blocks):
    g, r, c = blocks.shape
    eye = jnp.eye(g, dtype=blocks.dtype)
    return (blocks[:, :, None, :] * eye[:, None, :, None]).reshape(g * r, g * c)


def _s5_kernel(u_ref, a_ref, b_ref, c_ref, d_ref, gw_ref, gb_ref, o_ref,
               xa_ref, xb_ref, ua_ref, ub_ref, state_ref, *, batch, steps):
    step = pl.program_id(0)

    @pl.when(step == 0)
    def _():
        state_ref[...] = jnp.zeros_like(state_ref)
        xb_ref[...] = jnp.zeros_like(xb_ref)
        ub_ref[...] = jnp.zeros_like(ub_ref)

    a_re = jnp.broadcast_to(a_ref[0:1, :], (batch, S5_STATES))
    a_im = jnp.broadcast_to(a_ref[1:2, :], (batch, S5_STATES))

    def pipeline_step(x_new, u_new, x_cur, u_cur):
        u = jnp.swapaxes(u_ref[...], 0, 1).reshape(steps * batch, S5_WIDTH)
        u_new[...] = u
        x_new[...] = _bdot(u, b_ref[...])

        x_re, x_im = state_ref[:, 0:S5_STATES], state_ref[:, S5_STATES:2 * S5_STATES]
        for t in range(steps):
            rows = slice(t * batch, (t + 1) * batch)
            bu_re = x_cur[rows, 0:S5_STATES]
            bu_im = x_cur[rows, S5_STATES:2 * S5_STATES]
            x_re, x_im = (a_re * x_re - a_im * x_im + bu_re,
                          a_re * x_im + a_im * x_re + bu_im)
            x_cur[rows, 0:S5_STATES] = x_re
            x_cur[rows, S5_STATES:2 * S5_STATES] = x_im
        state_ref[:, 0:S5_STATES] = x_re
        state_ref[:, S5_STATES:2 * S5_STATES] = x_im

        y = _bdot(x_cur[...], c_ref[...]) + d_ref[...] * u_cur[...]
        z = jax.nn.gelu(y)
        out = z * jax.nn.sigmoid(_bdot(z, gw_ref[...]) + gb_ref[...])
        o_ref[...] = jnp.swapaxes(out.reshape(steps, batch, S5_WIDTH), 0, 1)

    @pl.when(step % 2 == 0)
    def _():
        pipeline_step(xa_ref, ua_ref, xb_ref, ub_ref)

    @pl.when(step % 2 == 1)
    def _():
        pipeline_step(xb_ref, ub_ref, xa_ref, ua_ref)


def _s5(u, abar, b_blk, c_blk, d_skip, glu_w, glu_b, batch, seq_len):
    steps = min(S5_TIME_TILE, seq_len)
    rows = steps * batch
    n_tiles = seq_len // steps
    u_col = MIX_F32_WIDTH // S5_WIDTH - 1
    x_tile = pltpu.VMEM((rows, 2 * S5_STATES), F32)
    u_tile = pltpu.VMEM((rows, S5_WIDTH), F32)
    return pl.pallas_call(
        functools.partial(_s5_kernel, batch=batch, steps=steps),
        grid=(n_tiles + 1,),
        in_specs=[
            pl.BlockSpec((batch, steps, S5_WIDTH),
                         lambda i: (0, jnp.minimum(i, n_tiles - 1), u_col)),
            _full((2, S5_STATES)),
            _full((S5_WIDTH, 2 * S5_STATES)),
            _full((2 * S5_STATES, S5_WIDTH)),
            _full((1, S5_WIDTH)),
            _full((S5_WIDTH, S5_WIDTH)),
            _full((1, S5_WIDTH)),
        ],
        out_specs=pl.BlockSpec((batch, steps, S5_WIDTH),
                               lambda i: (0, jnp.maximum(i - 1, 0), 0)),
        out_shape=jax.ShapeDtypeStruct((batch, seq_len, S5_WIDTH), F32),
        scratch_shapes=[x_tile, x_tile, u_tile, u_tile,
                        pltpu.VMEM((batch, 2 * S5_STATES), F32)],
        compiler_params=_params(),
        name="s5_ssm",
    )(u, abar, b_blk, c_blk, d_skip, glu_w, glu_b)


def _s5_mixer(u, batch, seq_len, a_re, a_im, log_dt, b_re, b_im, c_re, c_im, d_skip,
              glu_w, glu_b):
    abar_re, abar_im, bbar_re, bbar_im = _s5_prep(a_re, a_im, log_dt, b_re, b_im)
    abar = jnp.stack([abar_re.reshape(-1), abar_im.reshape(-1)])
    b_blk = jnp.concatenate([_block_diag(bbar_re), _block_diag(bbar_im)], axis=1)
    c_blk = jnp.concatenate([_block_diag(jnp.swapaxes(c_re, 1, 2)),
                             -_block_diag(jnp.swapaxes(c_im, 1, 2))], axis=0)
    y = _s5(u.reshape(batch, seq_len, MIX_F32_WIDTH), abar, b_blk.astype(BF16),
            c_blk.astype(BF16), d_skip.reshape(1, S5_WIDTH), glu_w.astype(BF16),
            glu_b.reshape(1, S5_WIDTH), batch, seq_len)
    return y.reshape(batch * seq_len, S5_WIDTH)


def kernel(x, norm_ffn1, ffn1_w_gate, ffn1_w_up, ffn1_w_down, norm_mix, w_in, attn_sinks,
           hgrn_lb_logits, hgrn_norm, s5_a_re, s5_a_im, s5_log_dt, s5_b_re, s5_b_im,
           s5_c_re, s5_c_im, s5_d, s5_glu_w, s5_glu_b, w_out, norm_ffn2, ffn2_w_gate,
           ffn2_w_up, ffn2_w_down, norm_final):
    batch, seq_len, _ = x.shape
    depth = w_in.shape[0]
    h = x.reshape(batch * seq_len, D_MODEL)
    row = lambda v: v.reshape(1, -1)
    final_gain = row(norm_final)
    for layer in range(depth):
        h, u_bf16, u_f32 = _ffn(
            h, None, (row(norm_mix[layer]), w_in), row(norm_ffn1[layer]), ffn1_w_gate,
            ffn1_w_up, ffn1_w_down, final_gain, False, layer, seq_len)
        y_a = _attention(u_bf16, attn_sinks[layer], seq_len)
        y_b = _hgrn(u_bf16, u_f32, hgrn_lb_logits, row(hgrn_norm[layer]), layer, seq_len)
        y_c = _s5_mixer(u_f32, batch, seq_len, s5_a_re[layer], s5_a_im[layer],
                        s5_log_dt[layer], s5_b_re[layer], s5_b_im[layer], s5_c_re[layer],
                        s5_c_im[layer], s5_d[layer], s5_glu_w[layer], s5_glu_b[layer])
        (h,) = _ffn(h, (y_a, y_b, y_c, w_out), None, row(norm_ffn2[layer]), ffn2_w_gate,
                    ffn2_w_up, ffn2_w_down, final_gain, layer == depth - 1, layer, seq_len)
    return h.reshape(batch, seq_len, D_MODEL)
```

```python
import functools
import math

import jax
import jax.numpy as jnp
from jax import lax
from jax.experimental import pallas as pl
from jax.experimental.pallas import tpu as pltpu

D_MODEL = 1024
D_FF = 2816
EPS = 1e-6

ATTN_HEADS = 8
ATTN_KV_HEADS = 2
ATTN_GROUP = ATTN_HEADS // ATTN_KV_HEADS
HEAD_DIM = 64
WINDOW = 128
ATTN_WIDTH = ATTN_HEADS * HEAD_DIM
KV_WIDTH = ATTN_KV_HEADS * HEAD_DIM

HGRN_HEADS = 4
HGRN_DIM = 64
HGRN_WIDTH = HGRN_HEADS * HGRN_DIM
HGRN_CHUNK = 64
HGRN_LEVELS = 6

S5_GROUPS = 16
S5_GROUP_CH = 16
S5_STATE = 64
S5_WIDTH = S5_GROUPS * S5_GROUP_CH
S5_STATES = S5_GROUPS * S5_STATE

IN_PROJ_WIDTH = ATTN_WIDTH + 2 * KV_WIDTH + 4 * HGRN_WIDTH + S5_WIDTH
_F_B_START = ATTN_WIDTH + 2 * KV_WIDTH + HGRN_WIDTH
_U_C_START = IN_PROJ_WIDTH - S5_WIDTH
MIX_BF16_COLUMNS = ((0, _F_B_START), (_F_B_START + HGRN_WIDTH, _U_C_START))
MIX_F32_COLUMNS = ((_F_B_START, _F_B_START + HGRN_WIDTH), (_U_C_START, IN_PROJ_WIDTH))
MIX_BF16_WIDTH = sum(b - a for a, b in MIX_BF16_COLUMNS)
MIX_F32_WIDTH = sum(b - a for a, b in MIX_F32_COLUMNS)

VMEM_LIMIT_BYTES = 56 * 1024 * 1024

TOKEN_TILE = 512
MIXER_TILE = 2048
S5_TIME_TILE = 128
MASK_VALUE = -1e30

F32 = jnp.float32
BF16 = jnp.bfloat16


def _rms(x, gain):
    return x * lax.rsqrt(jnp.mean(x * x, axis=-1, keepdims=True) + EPS) * gain


def _bdot(a, b):
    return jnp.dot(a.astype(BF16), b.astype(BF16), preferred_element_type=F32)


def _params(n_axes=1):
    return pltpu.CompilerParams(
        dimension_semantics=("arbitrary",) * n_axes,
        vmem_limit_bytes=VMEM_LIMIT_BYTES,
    )


def _full(shape):
    return pl.BlockSpec(shape, lambda i: (0,) * len(shape))


WEIGHT_LOAD_STEPS = 8


def _token_tile(step):
    return jnp.maximum(step - WEIGHT_LOAD_STEPS, 0)


def _weight_chunk_spec(w, layer):
    _, rows, cols = w.shape
    return pl.BlockSpec(
        (None, rows // WEIGHT_LOAD_STEPS, cols),
        lambda i: (layer, jnp.minimum(i, WEIGHT_LOAD_STEPS - 1), 0))


def _keep_weight_chunk(step, w_ref, w_s):
    chunk = w_ref.shape[0]
    w_s[pl.ds(pl.multiple_of(step * chunk, chunk), chunk), :] = w_ref[...].astype(BF16)


def _ffn_kernel(*refs, mix, project, final_norm):
    refs = list(refs)
    take = lambda n: [refs.pop(0) for _ in range(n)]
    (x_ref,) = take(1)
    if mix:
        ya_ref, yb_ref, yc_ref, wo_ref = take(4)
    g_ref, wg_ref, wu_ref, wd_ref, gf_ref = take(5)
    if project:
        gm_ref, wi_ref = take(2)
    (o_ref,) = take(1)
    if project:
        ub_ref, uf_ref = take(2)
    if mix:
        (wo_s,) = take(1)
    wg_s, wu_s, wd_s = take(3)
    streamed = [(wg_ref, wg_s), (wu_ref, wu_s), (wd_ref, wd_s)]
    if mix:
        streamed.append((wo_ref, wo_s))
    if project:
        (wi_s,) = take(1)
        streamed.append((wi_ref, wi_s))
    step = pl.program_id(0)

    @pl.when(step < WEIGHT_LOAD_STEPS)
    def _():
        for w_ref, w_s in streamed:
            _keep_weight_chunk(step, w_ref, w_s)

    @pl.when(step >= WEIGHT_LOAD_STEPS)
    def _():
        x = x_ref[...]
        if mix:
            y = jnp.concatenate([ya_ref[...], yb_ref[...], yc_ref[...]], axis=-1)
            x = x + _bdot(y, wo_s[...])
        h = _rms(x, g_ref[...]).astype(BF16)
        gate = jnp.dot(h, wg_s[...], preferred_element_type=F32)
        up = jnp.dot(h, wu_s[...], preferred_element_type=F32)
        act = (gate * jax.nn.sigmoid(gate) * up).astype(BF16)
        y = x + 0.5 * jnp.dot(act, wd_s[...], preferred_element_type=F32)
        if final_norm:
            y = _rms(y, gf_ref[...])
        o_ref[...] = y
        if project:
            u = jnp.dot(_rms(y, gm_ref[...]).astype(BF16), wi_s[...],
                        preferred_element_type=F32)
            for dst, columns in ((ub_ref, MIX_BF16_COLUMNS), (uf_ref, MIX_F32_COLUMNS)):
                dst[...] = jnp.concatenate([u[:, a:b] for a, b in columns],
                                           axis=-1).astype(dst.dtype)


def _ffn(x, mixer_out, in_proj, gain, w_gate, w_up, w_down, final_gain, final_norm,
         layer, seq_len):
    n = x.shape[0]
    tm = min(TOKEN_TILE, seq_len)
    row = lambda width: pl.BlockSpec((tm, width), lambda i: (_token_tile(i), 0))
    vec = _full((1, D_MODEL))
    chunk = lambda w: _weight_chunk_spec(w, layer)
    ffn_weights = [w_gate, w_up, w_down]
    in_specs, args = [row(D_MODEL)], [x]
    out_specs = [row(D_MODEL)]
    out_shape = [jax.ShapeDtypeStruct((n, D_MODEL), F32)]
    scratch = list(ffn_weights)
    if mixer_out is not None:
        y_a, y_b, y_c, w_out = mixer_out
        in_specs += [row(ATTN_WIDTH), row(HGRN_WIDTH), row(S5_WIDTH), chunk(w_out)]
        args += [y_a, y_b, y_c, w_out]
        scratch.insert(0, w_out)
    in_specs += [vec] + [chunk(w) for w in ffn_weights] + [vec]
    args += [gain, *ffn_weights, final_gain]
    if in_proj is not None:
        mix_gain, w_in = in_proj
        in_specs += [vec, chunk(w_in)]
        args += [mix_gain, w_in]
        scratch.append(w_in)
        out_specs += [row(MIX_BF16_WIDTH), row(MIX_F32_WIDTH)]
        out_shape += [jax.ShapeDtypeStruct((n, MIX_BF16_WIDTH), BF16),
                      jax.ShapeDtypeStruct((n, MIX_F32_WIDTH), F32)]
    return pl.pallas_call(
        functools.partial(_ffn_kernel, mix=mixer_out is not None,
                          project=in_proj is not None, final_norm=final_norm),
        grid=(WEIGHT_LOAD_STEPS + n // tm,),
        in_specs=in_specs,
        out_specs=out_specs,
        out_shape=out_shape,
        scratch_shapes=[pltpu.VMEM(w.shape[1:], BF16) for w in scratch],
        compiler_params=_params(),
        name="ffn",
    )(*args)


def _alibi_slope(head):
    return 2.0 ** (-8.0 * (head + 1.0) / ATTN_HEADS)


def _attn_kernel(sink_ref, q_ref, kv_ref, kvp_ref, o_ref, bias_ref, *, seq_len, tile):
    first = (pl.program_id(0) * tile) % seq_len == 0
    cols = ATTN_GROUP * WINDOW
    c_idx = lax.broadcasted_iota(jnp.int32, (2 * WINDOW, cols), 0)
    log2e = math.log2(math.e)
    scale = log2e / math.sqrt(HEAD_DIM)

    @pl.when(pl.program_id(0) == 0)
    def _():
        g_idx = lax.broadcasted_iota(jnp.int32, (1, cols), 1) // WINDOW
        t_idx = lax.broadcasted_iota(jnp.int32, (2 * WINDOW, cols), 1) % WINDOW
        rel = t_idx + WINDOW - c_idx
        in_win = (rel >= 0) & (rel < WINDOW)
        relf = rel.astype(F32)
        for hk in range(ATTN_KV_HEADS):
            slope = jnp.zeros((1, cols), F32)
            sink = jnp.zeros((1, cols), F32)
            for g in range(ATTN_GROUP):
                head = hk * ATTN_GROUP + g
                slope = jnp.where(g_idx == g, _alibi_slope(head) * log2e, slope)
                sink = jnp.where(g_idx == g, sink_ref[head] * log2e, sink)
            bias = jnp.where(in_win, -(slope * relf), MASK_VALUE)
            bias_ref[hk] = jnp.where(c_idx == 0, sink, bias)

    keys = jnp.concatenate([kvp_ref[...], kv_ref[...]], axis=0).astype(F32)
    k_all = keys[:, 0:KV_WIDTH]
    v_t = jnp.transpose(keys[:, KV_WIDTH:2 * KV_WIDTH])
    krow = lax.broadcasted_iota(jnp.int32, (2 * WINDOW, HEAD_DIM), 0)
    vcol = lax.broadcasted_iota(jnp.int32, (HEAD_DIM, 2 * WINDOW), 1)
    ones_rows = jnp.ones((8, 2 * WINDOW), F32)
    no_prev = (c_idx >= 1) & (c_idx < jnp.where(first, WINDOW, 0))

    n_blocks = tile // WINDOW
    units = [(blk, hk) for blk in range(n_blocks) for hk in range(ATTN_KV_HEADS)]
    q_t = [jnp.transpose(q_ref[blk * WINDOW:(blk + 1) * WINDOW, :].astype(F32) * scale)
           .astype(BF16) for blk in range(n_blocks)]
    out_t = [[None] * ATTN_HEADS for _ in range(n_blocks)]

    def scores(blk, hk):
        slots = slice(blk * WINDOW, (blk + 2) * WINDOW)
        dims = slice(hk * HEAD_DIM, (hk + 1) * HEAD_DIM)
        kh = jnp.where(krow == 0, 0.0, k_all[slots, dims]).astype(BF16)
        qs = jnp.concatenate(
            [q_t[blk][h * HEAD_DIM:(h + 1) * HEAD_DIM, :]
             for h in range(hk * ATTN_GROUP, (hk + 1) * ATTN_GROUP)], axis=1)
        s = jnp.dot(kh, qs, preferred_element_type=F32) + bias_ref[hk]
        if blk == 0:
            s = jnp.where(no_prev, MASK_VALUE, s)
        return s

    def weighted_values(blk, hk, s):
        slots = slice(blk * WINDOW, (blk + 2) * WINDOW)
        dims = slice(hk * HEAD_DIM, (hk + 1) * HEAD_DIM)
        vh = jnp.where(vcol == 0, 0.0, v_t[dims, slots])
        vh = jnp.concatenate([vh, ones_rows], axis=0).astype(BF16)
        p = jnp.exp2(s - jnp.max(s, axis=0, keepdims=True)).astype(BF16)
        o = jnp.dot(vh, p, preferred_element_type=F32)
        o = o[:HEAD_DIM] * (1.0 / o[HEAD_DIM:HEAD_DIM + 1])
        for g in range(ATTN_GROUP):
            out_t[blk][hk * ATTN_GROUP + g] = o[:, g * WINDOW:(g + 1) * WINDOW]
        if hk == ATTN_KV_HEADS - 1:
            o_ref[blk * WINDOW:(blk + 1) * WINDOW, :] = jnp.transpose(
                jnp.concatenate(out_t[blk], axis=0))

    s = scores(*units[0])
    for i, unit in enumerate(units):
        s_next = scores(*units[i + 1]) if i + 1 < len(units) else None
        weighted_values(*unit, s)
        s = s_next


def _attention(u, sinks, seq_len):
    n = u.shape[0]
    tile = min(MIXER_TILE, seq_len)
    blocks_per_tile = tile // WINDOW
    kv_col = ATTN_WIDTH // (2 * KV_WIDTH)
    return pl.pallas_call(
        functools.partial(_attn_kernel, seq_len=seq_len, tile=tile),
        grid=(n // tile,),
        in_specs=[
            pl.BlockSpec(memory_space=pltpu.SMEM),
            pl.BlockSpec((tile, ATTN_WIDTH), lambda i: (i, 0)),
            pl.BlockSpec((tile, 2 * KV_WIDTH), lambda i: (i, kv_col)),
            pl.BlockSpec((WINDOW, 2 * KV_WIDTH),
                         lambda i: (jnp.maximum(i * blocks_per_tile - 1, 0), kv_col)),
        ],
        out_specs=pl.BlockSpec((tile, ATTN_WIDTH), lambda i: (i, 0)),
        out_shape=jax.ShapeDtypeStruct((n, ATTN_WIDTH), F32),
        scratch_shapes=[pltpu.VMEM((ATTN_KV_HEADS, 2 * WINDOW, ATTN_GROUP * WINDOW), F32)],
        compiler_params=_params(),
        name="swa_attention",
    )(sinks, u, u, u)


HGRN_FACTORED_DECAY_LIMIT = 80.0


def _hgrn_exponent_matrix():
    c = HGRN_CHUNK
    r = jnp.arange(c)[:, None]
    j = jnp.arange(c)[None, :]
    blocks = []
    for lvl in range(HGRN_LEVELS):
        m = 1 << lvl
        start = (r // (2 * m)) * (2 * m)
        right = (r // m) % 2 == 1
        in_right = right & (j >= start + m) & (j <= r)
        in_left = (~right) & (j > r) & (j < start + m)
        blocks.append(in_right | in_left)
    blocks.append(j <= r)
    blocks.append(j > r)
    return jnp.concatenate(blocks, axis=0).astype(BF16)


def _split3(x):
    hi = x.astype(BF16)
    r1 = x - hi.astype(F32)
    mid = r1.astype(BF16)
    lo = (r1 - mid.astype(F32)).astype(BF16)
    return hi, mid, lo


def _hgrn_kernel(q_ref, f_ref, i_ref, g_ref, lbl_ref, gain_ref, em_ref, hm_ref, o_ref,
                 state_ref, state0_ref, qs_ref, ks_ref, lf_ref, bc_ref, *,
                 seq_len, tile, layer):
    c = HGRN_CHUNK
    first = (pl.program_id(0) * tile) % seq_len == 0

    @pl.when(first)
    def _():
        state_ref[...] = jnp.zeros_like(state_ref)

    logits = lbl_ref[...]
    e = jnp.exp(logits - jnp.max(logits, axis=0, keepdims=True))
    sm = e / jnp.sum(e, axis=0, keepdims=True)
    lb = jnp.zeros((1, HGRN_WIDTH), F32)
    for l in range(1, layer + 1):
        lb = lb + sm[l:l + 1, :]

    r_idx = lax.broadcasted_iota(jnp.int32, (c, 1), 0)
    t_idx = lax.broadcasted_iota(jnp.int32, (c, c), 0)
    s_idx = lax.broadcasted_iota(jnp.int32, (c, c), 1)
    lane_head = lax.broadcasted_iota(jnp.int32, (c, HGRN_WIDTH), 1) // HGRN_DIM
    n_chunks = tile // c

    def chunk_rows(ci):
        return pl.ds(pl.multiple_of(ci * c, c), c)

    def prepare(ci):
        rows = pl.ds(ci * c, c)
        z = f_ref[rows, :]
        q = q_ref[rows, :].astype(F32)
        qs_ref[rows, :] = q * jax.nn.sigmoid(q)
        e = jnp.exp(-jnp.abs(z))
        r = 1.0 + e
        inv = 1.0 / r
        pos = z >= 0.0
        sig = jnp.where(pos, 1.0, e) * inv
        ks_ref[rows, :] = (1.0 - lb) * (jnp.where(pos, e, 1.0) * inv)
        lf = jnp.where(lb > 0.0, jnp.log(lb + (1.0 - lb) * sig),
                       jnp.minimum(z, 0.0) - jnp.log(r))
        lf_ref[rows, :] = lf
        tri = em_ref[HGRN_LEVELS * c:(HGRN_LEVELS + 1) * c, :]
        bc_ref[rows, :] = sum(jnp.dot(tri, p, preferred_element_type=F32)
                              for p in _split3(lf))

    def stack_heads(x):
        return jnp.concatenate(
            [jnp.where(lane_head == h, x, 0.0) for h in range(HGRN_HEADS)],
            axis=0).astype(BF16)

    def chunk_scores(ci):
        rows = pl.ds(ci * c, c)
        bc = bc_ref[rows, :]
        qd = (qs_ref[rows, :] * jnp.exp(bc)).astype(BF16)
        kb = stack_heads(ks_ref[rows, :] * jnp.exp(-bc))
        att = lax.dot_general(qd, kb, (((1,), (1,)), ((), ())),
                              preferred_element_type=F32)
        col_s = lax.broadcasted_iota(jnp.int32, (c, HGRN_WIDTH), 1) % c
        row_t = lax.broadcasted_iota(jnp.int32, (c, HGRN_WIDTH), 0)
        return qd, jnp.where(col_s <= row_t, att, 0.0).astype(BF16)

    def chunk_output(ci, qd, att, state_t):
        rows = pl.ds(ci * c, c)
        k, v, bc = ks_ref[rows, :], i_ref[rows, :].astype(F32), bc_ref[rows, :]
        b_last = bc[c - 1:c, :]
        o = jnp.dot(att, stack_heads(v), preferred_element_type=F32)
        o = o + lax.dot_general(qd, state_t.astype(BF16), (((1,), (1,)), ((), ())),
                                preferred_element_type=F32)
        upd = lax.dot_general(v.astype(BF16), (k * jnp.exp(b_last - bc)).astype(BF16),
                              (((0,), (0,)), ((), ())), preferred_element_type=F32)
        rh = lax.broadcasted_iota(jnp.int32, (HGRN_WIDTH, HGRN_WIDTH), 0) // HGRN_DIM
        ch = lax.broadcasted_iota(jnp.int32, (HGRN_WIDTH, HGRN_WIDTH), 1) // HGRN_DIM
        o_ref[rows, :] = o
        return state_t * jnp.exp(b_last) + jnp.where(rh == ch, upd, 0.0)

    def general_chunk(ci, carry):
        rows = chunk_rows(ci)
        q, k, v = qs_ref[rows, :], ks_ref[rows, :], i_ref[rows, :].astype(F32)
        expo = sum(jnp.dot(em_ref[...], p, preferred_element_type=F32)
                   for p in _split3(lf_ref[rows, :]))
        decay = jnp.exp(expo)
        outs = []
        for h in range(HGRN_HEADS):
            cols = slice(h * HGRN_DIM, (h + 1) * HGRN_DIM)
            qh, kh, vh = q[:, cols], k[:, cols], v[:, cols]
            att = jnp.where(t_idx == s_idx,
                            jnp.sum(qh * kh, axis=-1, keepdims=True), 0.0)
            for lvl in range(HGRN_LEVELS):
                m = 1 << lvl
                d = decay[lvl * c:(lvl + 1) * c, cols]
                right = (r_idx // m) % 2 == 1
                ql = jnp.where(right, qh * d, 0.0)
                kl = jnp.where(right, 0.0, kh * d)
                a = lax.dot_general(ql.astype(BF16), kl.astype(BF16),
                                    (((1,), (1,)), ((), ())),
                                    preferred_element_type=F32)
                same = (t_idx // (2 * m)) == (s_idx // (2 * m))
                att = att + jnp.where(same, a, 0.0)
            d_in = decay[HGRN_LEVELS * c:(HGRN_LEVELS + 1) * c, cols]
            d_out = decay[(HGRN_LEVELS + 1) * c:(HGRN_LEVELS + 2) * c, cols]
            state_t = state_ref[cols, cols]
            o = _bdot(att, vh) + lax.dot_general(
                (qh * d_in).astype(BF16), state_t.astype(BF16),
                (((1,), (1,)), ((), ())), preferred_element_type=F32)
            upd_t = lax.dot_general(vh.astype(BF16), (kh * d_out).astype(BF16),
                                    (((0,), (0,)), ((), ())),
                                    preferred_element_type=F32)
            state_ref[cols, cols] = state_t * d_in[c - 1:c, :] + upd_t
            outs.append(o)
        o_ref[rows, :] = jnp.concatenate(outs, axis=-1)
        return carry

    def finish_tile():
        o = o_ref[...]
        sq = sum(jnp.dot(p, hm_ref[...], preferred_element_type=F32)
                 for p in _split3(o * o))
        inv = lax.rsqrt(sq * (1.0 / HGRN_DIM) + EPS)
        gate = g_ref[...].astype(F32)
        o_ref[...] = o * inv * gain_ref[...] * (gate * jax.nn.sigmoid(gate))

    state0_ref[...] = state_ref[...]
    prepare(0)
    if n_chunks > 1:
        prepare(1)
    state_t = state_ref[...]
    scores = chunk_scores(0)
    for ci in range(n_chunks):
        if ci + 2 < n_chunks:
            prepare(ci + 2)
        scores_next = chunk_scores(ci + 1) if ci + 1 < n_chunks else None
        state_t = chunk_output(ci, *scores, state_t)
        scores = scores_next
    state_ref[...] = state_t
    finish_tile()

    @pl.when(jnp.min(bc_ref[...]) < -HGRN_FACTORED_DECAY_LIMIT)
    def _():
        state_ref[...] = state0_ref[...]
        lax.fori_loop(0, n_chunks, general_chunk, 0)
        finish_tile()


def _hgrn(u_bf16, u_f32, lb_logits, gain, layer, seq_len):
    n = u_bf16.shape[0]
    tile = min(MIXER_TILE, seq_len)
    depth = lb_logits.shape[0]
    col = lambda j: pl.BlockSpec((tile, HGRN_WIDTH), lambda i: (i, j))
    q_col = (ATTN_WIDTH + 2 * KV_WIDTH) // HGRN_WIDTH
    em = _hgrn_exponent_matrix()
    channel_head = jnp.arange(HGRN_WIDTH) // HGRN_DIM
    head_mask = (channel_head[:, None] == channel_head[None, :]).astype(BF16)
    tile_f32 = pltpu.VMEM((tile, HGRN_WIDTH), F32)
    return pl.pallas_call(
        functools.partial(_hgrn_kernel, seq_len=seq_len, tile=tile, layer=layer),
        grid=(n // tile,),
        in_specs=[col(q_col), col(0), col(q_col + 1), col(q_col + 2),
                  _full((depth, HGRN_WIDTH)), _full((1, HGRN_WIDTH)),
                  _full(em.shape), _full(head_mask.shape)],
        out_specs=pl.BlockSpec((tile, HGRN_WIDTH), lambda i: (i, 0)),
        out_shape=jax.ShapeDtypeStruct((n, HGRN_WIDTH), F32),
        scratch_shapes=[pltpu.VMEM((HGRN_WIDTH, HGRN_WIDTH), F32),
                        pltpu.VMEM((HGRN_WIDTH, HGRN_WIDTH), F32),
                        tile_f32, tile_f32, tile_f32, tile_f32],
        compiler_params=_params(),
        name="hgrn2",
    )(u_bf16, u_f32, u_bf16, u_bf16, lb_logits, gain, em, head_mask)


def _s5_prep_kernel(ar_ref, ai_ref, ldt_ref, br_ref, bi_ref,
                    abr_ref, abi_ref, bbr_ref, bbi_ref):
    ar, ai = ar_ref[...], ai_ref[...]
    dt = jnp.exp(ldt_ref[...])
    mag = jnp.exp(ar * dt)
    abar_re = mag * jnp.cos(ai * dt)
    abar_im = mag * jnp.sin(ai * dt)
    nr, ni = abar_re - 1.0, abar_im
    den = ar * ar + ai * ai
    z_re = (nr * ar + ni * ai) / den
    z_im = (ni * ar - nr * ai) / den
    abr_ref[...] = abar_re
    abi_ref[...] = abar_im
    br, bi = br_ref[...], bi_ref[...]
    bbr_ref[...] = z_re[:, None, :] * br - z_im[:, None, :] * bi
    bbi_ref[...] = z_re[:, None, :] * bi + z_im[:, None, :] * br


def _s5_prep(a_re, a_im, log_dt, b_re, b_im):
    gp = jax.ShapeDtypeStruct((S5_GROUPS, S5_STATE), F32)
    gcp = jax.ShapeDtypeStruct((S5_GROUPS, S5_GROUP_CH, S5_STATE), F32)
    return pl.pallas_call(
        _s5_prep_kernel,
        out_shape=(gp, gp, gcp, gcp),
        name="s5_discretize",
    )(a_re, a_im, log_dt[:, None], jnp.swapaxes(b_re, 1, 2), jnp.swapaxes(b_im, 1, 2))


def _block_diag(blocks):
    g, r, c = blocks.shape
    eye = jnp.eye(g, dtype=blocks.dtype)
    return (blocks[:, :, None, :] * eye[:, None, :, None]).reshape(g * r, g * c)


def _s5_kernel(u_ref, a_ref, b_ref, c_ref, d_ref, gw_ref, gb_ref, o_ref,
               xa_ref, xb_ref, ua_ref, ub_ref, state_ref, *, batch, steps):
    step = pl.program_id(0)

    @pl.when(step == 0)
    def _():
        state_ref[...] = jnp.zeros_like(state_ref)
        xb_ref[...] = jnp.zeros_like(xb_ref)
        ub_ref[...] = jnp.zeros_like(ub_ref)

    a_re = jnp.broadcast_to(a_ref[0:1, :], (batch, S5_STATES))
    a_im = jnp.broadcast_to(a_ref[1:2, :], (batch, S5_STATES))

    def pipeline_step(x_new, u_new, x_cur, u_cur):
        u = jnp.swapaxes(u_ref[...], 0, 1).reshape(steps * batch, S5_WIDTH)
        u_new[...] = u
        x_new[...] = _bdot(u, b_ref[...])

        x_re, x_im = state_ref[:, 0:S5_STATES], state_ref[:, S5_STATES:2 * S5_STATES]
        for t in range(steps):
            rows = slice(t * batch, (t + 1) * batch)
            bu_re = x_cur[rows, 0:S5_STATES]
            bu_im = x_cur[rows, S5_STATES:2 * S5_STATES]
            x_re, x_im = (a_re * x_re - a_im * x_im + bu_re,
                          a_re * x_im + a_im * x_re + bu_im)
            x_cur[rows, 0:S5_STATES] = x_re
            x_cur[rows, S5_STATES:2 * S5_STATES] = x_im
        state_ref[:, 0:S5_STATES] = x_re
        state_ref[:, S5_STATES:2 * S5_STATES] = x_im

        y = _bdot(x_cur[...], c_ref[...]) + d_ref[...] * u_cur[...]
        z = jax.nn.gelu(y)
        out = z * jax.nn.sigmoid(_bdot(z, gw_ref[...]) + gb_ref[...])
        o_ref[...] = jnp.swapaxes(out.reshape(steps, batch, S5_WIDTH), 0, 1)

    @pl.when(step % 2 == 0)
    def _():
        pipeline_step(xa_ref, ua_ref, xb_ref, ub_ref)

    @pl.when(step % 2 == 1)
    def _():
        pipeline_step(xb_ref, ub_ref, xa_ref, ua_ref)


def _s5(u, abar, b_blk, c_blk, d_skip, glu_w, glu_b, batch, seq_len):
    steps = min(S5_TIME_TILE, seq_len)
    rows = steps * batch
    n_tiles = seq_len // steps
    u_col = MIX_F32_WIDTH // S5_WIDTH - 1
    x_tile = pltpu.VMEM((rows, 2 * S5_STATES), F32)
    u_tile = pltpu.VMEM((rows, S5_WIDTH), F32)
    return pl.pallas_call(
        functools.partial(_s5_kernel, batch=batch, steps=steps),
        grid=(n_tiles + 1,),
        in_specs=[
            pl.BlockSpec((batch, steps, S5_WIDTH),
                         lambda i: (0, jnp.minimum(i, n_tiles - 1), u_col)),
            _full((2, S5_STATES)),
            _full((S5_WIDTH, 2 * S5_STATES)),
            _full((2 * S5_STATES, S5_WIDTH)),
            _full((1, S5_WIDTH)),
            _full((S5_WIDTH, S5_WIDTH)),
            _full((1, S5_WIDTH)),
        ],
        out_specs=pl.BlockSpec((batch, steps, S5_WIDTH),
                               lambda i: (0, jnp.maximum(i - 1, 0), 0)),
        out_shape=jax.ShapeDtypeStruct((batch, seq_len, S5_WIDTH), F32),
        scratch_shapes=[x_tile, x_tile, u_tile, u_tile,
                        pltpu.VMEM((batch, 2 * S5_STATES), F32)],
        compiler_params=_params(),
        name="s5_ssm",
    )(u, abar, b_blk, c_blk, d_skip, glu_w, glu_b)


def _s5_mixer(u, batch, seq_len, a_re, a_im, log_dt, b_re, b_im, c_re, c_im, d_skip,
              glu_w, glu_b):
    abar_re, abar_im, bbar_re, bbar_im = _s5_prep(a_re, a_im, log_dt, b_re, b_im)
    abar = jnp.stack([abar_re.reshape(-1), abar_im.reshape(-1)])
    b_blk = jnp.concatenate([_block_diag(bbar_re), _block_diag(bbar_im)], axis=1)
    c_blk = jnp.concatenate([_block_diag(jnp.swapaxes(c_re, 1, 2)),
                             -_block_diag(jnp.swapaxes(c_im, 1, 2))], axis=0)
    y = _s5(u.reshape(batch, seq_len, MIX_F32_WIDTH), abar, b_blk.astype(BF16),
            c_blk.astype(BF16), d_skip.reshape(1, S5_WIDTH), glu_w.astype(BF16),
            glu_b.reshape(1, S5_WIDTH), batch, seq_len)
    return y.reshape(batch * seq_len, S5_WIDTH)


def kernel(x, norm_ffn1, ffn1_w_gate, ffn1_w_up, ffn1_w_down, norm_mix, w_in, attn_sinks,
           hgrn_lb_logits, hgrn_norm, s5_a_re, s5_a_im, s5_log_dt, s5_b_re, s5_b_im,
           s5_c_re, s5_c_im, s5_d, s5_glu_w, s5_glu_b, w_out, norm_ffn2, ffn2_w_gate,
           ffn2_w_up, ffn2_w_down, norm_final):
    batch, seq_len, _ = x.shape
    depth = w_in.shape[0]
    h = x.reshape(batch * seq_len, D_MODEL)
    row = lambda v: v.reshape(1, -1)
    final_gain = row(norm_final)
    for layer in range(depth):
        h, u_bf16, u_f32 = _ffn(
            h, None, (row(norm_mix[layer]), w_in), row(norm_ffn1[layer]), ffn1_w_gate,
            ffn1_w_up, ffn1_w_down, final_gain, False, layer, seq_len)
        y_a = _attention(u_bf16, attn_sinks[layer], seq_len)
        y_b = _hgrn(u_bf16, u_f32, hgrn_lb_logits, row(hgrn_norm[layer]), layer, seq_len)
        y_c = _s5_mixer(u_f32, batch, seq_len, s5_a_re[layer], s5_a_im[layer],
                        s5_log_dt[layer], s5_b_re[layer], s5_b_im[layer], s5_c_re[layer],
                        s5_c_im[layer], s5_d[layer], s5_glu_w[layer], s5_glu_b[layer])
        (h,) = _ffn(h, (y_a, y_b, y_c, w_out), None, row(norm_ffn2[layer]), ffn2_w_gate,
                    ffn2_w_up, ffn2_w_down, final_gain, layer == depth - 1, layer, seq_len)
    return h.reshape(batch, seq_len, D_MODEL)
```

```python
import functools
import math

import jax
import jax.numpy as jnp
from jax import lax
from jax.experimental import pallas as pl
from jax.experimental.pallas import tpu as pltpu

D_MODEL = 1024
D_FF = 2816
EPS = 1e-6

ATTN_HEADS = 8
ATTN_KV_HEADS = 2
ATTN_GROUP = ATTN_HEADS // ATTN_KV_HEADS
HEAD_DIM = 64
WINDOW = 128
ATTN_WIDTH = ATTN_HEADS * HEAD_DIM
KV_WIDTH = ATTN_KV_HEADS * HEAD_DIM

HGRN_HEADS = 4
HGRN_DIM = 64
HGRN_WIDTH = HGRN_HEADS * HGRN_DIM
HGRN_CHUNK = 64
HGRN_LEVELS = 6

S5_GROUPS = 16
S5_GROUP_CH = 16
S5_STATE = 64
S5_WIDTH = S5_GROUPS * S5_GROUP_CH
S5_STATES = S5_GROUPS * S5_STATE

IN_PROJ_WIDTH = ATTN_WIDTH + 2 * KV_WIDTH + 4 * HGRN_WIDTH + S5_WIDTH
MIX_BF16_WIDTH = ATTN_WIDTH + 2 * KV_WIDTH + 3 * HGRN_WIDTH
MIX_F32_WIDTH = 2 * HGRN_WIDTH + S5_WIDTH
LOG2_E = math.log2(math.e)
ATTN_Q_SCALE = LOG2_E / math.sqrt(HEAD_DIM)

VMEM_LIMIT_BYTES = 56 * 1024 * 1024

TOKEN_TILE = 512
MIXER_TILE = 2048
S5_TIME_TILE = 128
MASK_VALUE = -1e30

F32 = jnp.float32
BF16 = jnp.bfloat16


def _rms(x, gain):
    return x * lax.rsqrt(jnp.mean(x * x, axis=-1, keepdims=True) + EPS) * gain


def _bdot(a, b):
    return jnp.dot(a.astype(BF16), b.astype(BF16), preferred_element_type=F32)


def _params(n_axes=1):
    return pltpu.CompilerParams(
        dimension_semantics=("arbitrary",) * n_axes,
        vmem_limit_bytes=VMEM_LIMIT_BYTES,
    )


def _full(shape):
    return pl.BlockSpec(shape, lambda i: (0,) * len(shape))


WEIGHT_LOAD_STEPS = 8


def _token_tile(step):
    return jnp.maximum(step - WEIGHT_LOAD_STEPS, 0)


def _weight_chunk_spec(w, layer):
    _, rows, cols = w.shape
    return pl.BlockSpec(
        (None, rows // WEIGHT_LOAD_STEPS, cols),
        lambda i: (layer, jnp.minimum(i, WEIGHT_LOAD_STEPS - 1), 0))


def _keep_weight_chunk(step, w_ref, w_s):
    chunk = w_ref.shape[0]
    w_s[pl.ds(pl.multiple_of(step * chunk, chunk), chunk), :] = w_ref[...].astype(BF16)


def _hgrn_lower_bound(logits, layer):
    e = jnp.exp(logits - jnp.max(logits, axis=0, keepdims=True))
    sm = e / jnp.sum(e, axis=0, keepdims=True)
    lb = jnp.zeros((1, HGRN_WIDTH), F32)
    for l in range(1, layer + 1):
        lb = lb + sm[l:l + 1, :]
    return lb


def _hgrn_gates(z, lb):
    e = jnp.exp(-jnp.abs(z))
    r = 1.0 + e
    inv = 1.0 / r
    pos = z >= 0.0
    sig = jnp.where(pos, 1.0, e) * inv
    k = (1.0 - lb) * (jnp.where(pos, e, 1.0) * inv)
    log_f = jnp.where(lb > 0.0, jnp.log(lb + (1.0 - lb) * sig),
                      jnp.minimum(z, 0.0) - jnp.log(r))
    return log_f, k


def _ffn_kernel(*refs, mix, project, final_norm, layer):
    refs = list(refs)
    take = lambda n: [refs.pop(0) for _ in range(n)]
    (x_ref,) = take(1)
    if mix:
        ya_ref, yb_ref, yc_ref, wo_ref = take(4)
    g_ref, wg_ref, wu_ref, wd_ref, gf_ref = take(5)
    if project:
        gm_ref, wi_ref, lbl_ref = take(3)
    (o_ref,) = take(1)
    if project:
        ub_ref, uf_ref = take(2)
    if mix:
        (wo_s,) = take(1)
    wg_s, wu_s, wd_s = take(3)
    streamed = [(wg_ref, wg_s), (wu_ref, wu_s), (wd_ref, wd_s)]
    if mix:
        streamed.append((wo_ref, wo_s))
    if project:
        (wi_s,) = take(1)
        streamed.append((wi_ref, wi_s))
    step = pl.program_id(0)

    @pl.when(step < WEIGHT_LOAD_STEPS)
    def _():
        for w_ref, w_s in streamed:
            _keep_weight_chunk(step, w_ref, w_s)

    @pl.when(step >= WEIGHT_LOAD_STEPS)
    def _():
        x = x_ref[...]
        if mix:
            y = jnp.concatenate([ya_ref[...], yb_ref[...], yc_ref[...]], axis=-1)
            x = x + _bdot(y, wo_s[...])
        h = _rms(x, g_ref[...]).astype(BF16)
        gate = jnp.dot(h, wg_s[...], preferred_element_type=F32)
        up = jnp.dot(h, wu_s[...], preferred_element_type=F32)
        act = (gate * jax.nn.sigmoid(gate) * up).astype(BF16)
        y = x + 0.5 * jnp.dot(act, wd_s[...], preferred_element_type=F32)
        if final_norm:
            y = _rms(y, gf_ref[...])
        o_ref[...] = y
        if project:
            u = jnp.dot(_rms(y, gm_ref[...]).astype(BF16), wi_s[...],
                        preferred_element_type=F32)
            cuts = [0, ATTN_WIDTH, ATTN_WIDTH + 2 * KV_WIDTH]
            cuts += [cuts[-1] + j * HGRN_WIDTH for j in range(1, 5)] + [IN_PROJ_WIDTH]
            q_a, kv_a, q_b, z, i_b, g_b, u_c = (
                u[:, a:b] for a, b in zip(cuts[:-1], cuts[1:]))
            log_f, k_b = _hgrn_gates(z, _hgrn_lower_bound(lbl_ref[...], layer))
            silu = lambda t: t * jax.nn.sigmoid(t)
            ub_ref[...] = jnp.concatenate(
                [q_a * ATTN_Q_SCALE, kv_a, silu(q_b), i_b, silu(g_b)], axis=-1).astype(BF16)
            uf_ref[...] = jnp.concatenate([log_f, k_b, u_c], axis=-1)


def _ffn(x, mixer_out, in_proj, gain, w_gate, w_up, w_down, final_gain, final_norm,
         layer, seq_len):
    n = x.shape[0]
    tm = min(TOKEN_TILE, seq_len)
    row = lambda width: pl.BlockSpec((tm, width), lambda i: (_token_tile(i), 0))
    vec = _full((1, D_MODEL))
    chunk = lambda w: _weight_chunk_spec(w, layer)
    ffn_weights = [w_gate, w_up, w_down]
    in_specs, args = [row(D_MODEL)], [x]
    out_specs = [row(D_MODEL)]
    out_shape = [jax.ShapeDtypeStruct((n, D_MODEL), F32)]
    scratch = list(ffn_weights)
    if mixer_out is not None:
        y_a, y_b, y_c, w_out = mixer_out
        in_specs += [row(ATTN_WIDTH), row(HGRN_WIDTH), row(S5_WIDTH), chunk(w_out)]
        args += [y_a, y_b, y_c, w_out]
        scratch.insert(0, w_out)
    in_specs += [vec] + [chunk(w) for w in ffn_weights] + [vec]
    args += [gain, *ffn_weights, final_gain]
    if in_proj is not None:
        mix_gain, w_in, lb_logits = in_proj
        in_specs += [vec, chunk(w_in), _full(lb_logits.shape)]
        args += [mix_gain, w_in, lb_logits]
        scratch.append(w_in)
        out_specs += [row(MIX_BF16_WIDTH), row(MIX_F32_WIDTH)]
        out_shape += [jax.ShapeDtypeStruct((n, MIX_BF16_WIDTH), BF16),
                      jax.ShapeDtypeStruct((n, MIX_F32_WIDTH), F32)]
    return pl.pallas_call(
        functools.partial(_ffn_kernel, mix=mixer_out is not None,
                          project=in_proj is not None, final_norm=final_norm, layer=layer),
        grid=(WEIGHT_LOAD_STEPS + n // tm,),
        in_specs=in_specs,
        out_specs=out_specs,
        out_shape=out_shape,
        scratch_shapes=[pltpu.VMEM(w.shape[1:], BF16) for w in scratch],
        compiler_params=_params(),
        name="ffn",
    )(*args)


def _alibi_slope(head):
    return 2.0 ** (-8.0 * (head + 1.0) / ATTN_HEADS)


def _attn_kernel(sink_ref, q_ref, kv_ref, kvp_ref, o_ref, bias_ref, *, seq_len, tile):
    first = (pl.program_id(0) * tile) % seq_len == 0
    cols = ATTN_GROUP * WINDOW
    c_idx = lax.broadcasted_iota(jnp.int32, (2 * WINDOW, cols), 0)
    log2e = LOG2_E

    @pl.when(pl.program_id(0) == 0)
    def _():
        g_idx = lax.broadcasted_iota(jnp.int32, (1, cols), 1) // WINDOW
        t_idx = lax.broadcasted_iota(jnp.int32, (2 * WINDOW, cols), 1) % WINDOW
        rel = t_idx + WINDOW - c_idx
        in_win = (rel >= 0) & (rel < WINDOW)
        relf = rel.astype(F32)
        for hk in range(ATTN_KV_HEADS):
            slope = jnp.zeros((1, cols), F32)
            sink = jnp.zeros((1, cols), F32)
            for g in range(ATTN_GROUP):
                head = hk * ATTN_GROUP + g
                slope = jnp.where(g_idx == g, _alibi_slope(head) * log2e, slope)
                sink = jnp.where(g_idx == g, sink_ref[head] * log2e, sink)
            bias = jnp.where(in_win, -(slope * relf), MASK_VALUE)
            bias_ref[hk] = jnp.where(c_idx == 0, sink, bias)

    keys = jnp.concatenate([kvp_ref[...], kv_ref[...]], axis=0).astype(F32)
    k_all = keys[:, 0:KV_WIDTH]
    v_t = jnp.transpose(keys[:, KV_WIDTH:2 * KV_WIDTH])
    krow = lax.broadcasted_iota(jnp.int32, (2 * WINDOW, HEAD_DIM), 0)
    vcol = lax.broadcasted_iota(jnp.int32, (HEAD_DIM, 2 * WINDOW), 1)
    ones_rows = jnp.ones((8, 2 * WINDOW), F32)
    no_prev = (c_idx >= 1) & (c_idx < jnp.where(first, WINDOW, 0))

    n_blocks = tile // WINDOW
    units = [(blk, hk) for blk in range(n_blocks) for hk in range(ATTN_KV_HEADS)]
    q_t = [jnp.transpose(q_ref[blk * WINDOW:(blk + 1) * WINDOW, :].astype(F32))
           .astype(BF16) for blk in range(n_blocks)]
    out_t = [[None] * ATTN_HEADS for _ in range(n_blocks)]

    def scores(blk, hk):
        slots = slice(blk * WINDOW, (blk + 2) * WINDOW)
        dims = slice(hk * HEAD_DIM, (hk + 1) * HEAD_DIM)
        kh = jnp.where(krow == 0, 0.0, k_all[slots, dims]).astype(BF16)
        qs = jnp.concatenate(
            [q_t[blk][h * HEAD_DIM:(h + 1) * HEAD_DIM, :]
             for h in range(hk * ATTN_GROUP, (hk + 1) * ATTN_GROUP)], axis=1)
        s = jnp.dot(kh, qs, preferred_element_type=F32) + bias_ref[hk]
        if blk == 0:
            s = jnp.where(no_prev, MASK_VALUE, s)
        return s

    def weighted_values(blk, hk, s):
        slots = slice(blk * WINDOW, (blk + 2) * WINDOW)
        dims = slice(hk * HEAD_DIM, (hk + 1) * HEAD_DIM)
        vh = jnp.where(vcol == 0, 0.0, v_t[dims, slots])
        vh = jnp.concatenate([vh, ones_rows], axis=0).astype(BF16)
        p = jnp.exp2(s - jnp.max(s, axis=0, keepdims=True)).astype(BF16)
        o = jnp.dot(vh, p, preferred_element_type=F32)
        o = o[:HEAD_DIM] * (1.0 / o[HEAD_DIM:HEAD_DIM + 1])
        for g in range(ATTN_GROUP):
            out_t[blk][hk * ATTN_GROUP + g] = o[:, g * WINDOW:(g + 1) * WINDOW]
        if hk == ATTN_KV_HEADS - 1:
            o_ref[blk * WINDOW:(blk + 1) * WINDOW, :] = jnp.transpose(
                jnp.concatenate(out_t[blk], axis=0))

    s = scores(*units[0])
    for i, unit in enumerate(units):
        s_next = scores(*units[i + 1]) if i + 1 < len(units) else None
        weighted_values(*unit, s)
        s = s_next


def _attention(u, sinks, seq_len):
    n = u.shape[0]
    tile = min(MIXER_TILE, seq_len)
    blocks_per_tile = tile // WINDOW
    kv_col = ATTN_WIDTH // (2 * KV_WIDTH)
    return pl.pallas_call(
        functools.partial(_attn_kernel, seq_len=seq_len, tile=tile),
        grid=(n // tile,),
        in_specs=[
            pl.BlockSpec(memory_space=pltpu.SMEM),
            pl.BlockSpec((tile, ATTN_WIDTH), lambda i: (i, 0)),
            pl.BlockSpec((tile, 2 * KV_WIDTH), lambda i: (i, kv_col)),
            pl.BlockSpec((WINDOW, 2 * KV_WIDTH),
                         lambda i: (jnp.maximum(i * blocks_per_tile - 1, 0), kv_col)),
        ],
        out_specs=pl.BlockSpec((tile, ATTN_WIDTH), lambda i: (i, 0)),
        out_shape=jax.ShapeDtypeStruct((n, ATTN_WIDTH), F32),
        scratch_shapes=[pltpu.VMEM((ATTN_KV_HEADS, 2 * WINDOW, ATTN_GROUP * WINDOW), F32)],
        compiler_params=_params(),
        name="swa_attention",
    )(sinks, u, u, u)


HGRN_FACTORED_DECAY_LIMIT = 80.0


def _hgrn_exponent_matrix():
    c = HGRN_CHUNK
    r = jnp.arange(c)[:, None]
    j = jnp.arange(c)[None, :]
    blocks = []
    for lvl in range(HGRN_LEVELS):
        m = 1 << lvl
        start = (r // (2 * m)) * (2 * m)
        right = (r // m) % 2 == 1
        in_right = right & (j >= start + m) & (j <= r)
        in_left = (~right) & (j > r) & (j < start + m)
        blocks.append(in_right | in_left)
    blocks.append(j <= r)
    blocks.append(j > r)
    return jnp.concatenate(blocks, axis=0).astype(BF16)


def _split3(x):
    hi = x.astype(BF16)
    r1 = x - hi.astype(F32)
    mid = r1.astype(BF16)
    lo = (r1 - mid.astype(F32)).astype(BF16)
    return hi, mid, lo


def _hgrn_kernel(q_ref, lf_ref, k_ref, i_ref, g_ref, gain_ref, em_ref, hm_ref, o_ref,
                 state_ref, state0_ref, bc_ref, *, seq_len, tile):
    c = HGRN_CHUNK
    first = (pl.program_id(0) * tile) % seq_len == 0

    @pl.when(first)
    def _():
        state_ref[...] = jnp.zeros_like(state_ref)

    r_idx = lax.broadcasted_iota(jnp.int32, (c, 1), 0)
    t_idx = lax.broadcasted_iota(jnp.int32, (c, c), 0)
    s_idx = lax.broadcasted_iota(jnp.int32, (c, c), 1)
    lane_head = lax.broadcasted_iota(jnp.int32, (c, HGRN_WIDTH), 1) // HGRN_DIM
    n_chunks = tile // c

    def chunk_rows(ci):
        return pl.ds(pl.multiple_of(ci * c, c), c)

    def prepare(ci):
        rows = pl.ds(ci * c, c)
        tri = em_ref[HGRN_LEVELS * c:(HGRN_LEVELS + 1) * c, :]
        bc_ref[rows, :] = sum(jnp.dot(tri, p, preferred_element_type=F32)
                              for p in _split3(lf_ref[rows, :]))

    def stack_heads(x):
        return jnp.concatenate(
            [jnp.where(lane_head == h, x, 0.0) for h in range(HGRN_HEADS)],
            axis=0).astype(BF16)

    def chunk_scores(ci):
        rows = pl.ds(ci * c, c)
        bc = bc_ref[rows, :]
        qd = (q_ref[rows, :].astype(F32) * jnp.exp(bc)).astype(BF16)
        kb = stack_heads(k_ref[rows, :] * jnp.exp(-bc))
        att = lax.dot_general(qd, kb, (((1,), (1,)), ((), ())),
                              preferred_element_type=F32)
        col_s = lax.broadcasted_iota(jnp.int32, (c, HGRN_WIDTH), 1) % c
        row_t = lax.broadcasted_iota(jnp.int32, (c, HGRN_WIDTH), 0)
        return qd, jnp.where(col_s <= row_t, att, 0.0).astype(BF16)

    def chunk_output(ci, qd, att, state_t):
        rows = pl.ds(ci * c, c)
        k, v, bc = k_ref[rows, :], i_ref[rows, :].astype(F32), bc_ref[rows, :]
        b_last = bc[c - 1:c, :]
        o = jnp.dot(att, stack_heads(v), preferred_element_type=F32)
        o = o + lax.dot_general(qd, state_t.astype(BF16), (((1,), (1,)), ((), ())),
                                preferred_element_type=F32)
        upd = lax.dot_general(v.astype(BF16), (k * jnp.exp(b_last - bc)).astype(BF16),
                              (((0,), (0,)), ((), ())), preferred_element_type=F32)
        rh = lax.broadcasted_iota(jnp.int32, (HGRN_WIDTH, HGRN_WIDTH), 0) // HGRN_DIM
        ch = lax.broadcasted_iota(jnp.int32, (HGRN_WIDTH, HGRN_WIDTH), 1) // HGRN_DIM
        o_ref[rows, :] = o
        return state_t * jnp.exp(b_last) + jnp.where(rh == ch, upd, 0.0)

    def general_chunk(ci, carry):
        rows = chunk_rows(ci)
        q, k, v = q_ref[rows, :].astype(F32), k_ref[rows, :], i_ref[rows, :].astype(F32)
        expo = sum(jnp.dot(em_ref[...], p, preferred_element_type=F32)
                   for p in _split3(lf_ref[rows, :]))
        decay = jnp.exp(expo)
        outs = []
        for h in range(HGRN_HEADS):
            cols = slice(h * HGRN_DIM, (h + 1) * HGRN_DIM)
            qh, kh, vh = q[:, cols], k[:, cols], v[:, cols]
            att = jnp.where(t_idx == s_idx,
                            jnp.sum(qh * kh, axis=-1, keepdims=True), 0.0)
            for lvl in range(HGRN_LEVELS):
                m = 1 << lvl
                d = decay[lvl * c:(lvl + 1) * c, cols]
                right = (r_idx // m) % 2 == 1
                ql = jnp.where(right, qh * d, 0.0)
                kl = jnp.where(right, 0.0, kh * d)
                a = lax.dot_general(ql.astype(BF16), kl.astype(BF16),
                                    (((1,), (1,)), ((), ())),
                                    preferred_element_type=F32)
                same = (t_idx // (2 * m)) == (s_idx // (2 * m))
                att = att + jnp.where(same, a, 0.0)
            d_in = decay[HGRN_LEVELS * c:(HGRN_LEVELS + 1) * c, cols]
            d_out = decay[(HGRN_LEVELS + 1) * c:(HGRN_LEVELS + 2) * c, cols]
            state_t = state_ref[cols, cols]
            o = _bdot(att, vh) + lax.dot_general(
                (qh * d_in).astype(BF16), state_t.astype(BF16),
                (((1,), (1,)), ((), ())), preferred_element_type=F32)
            upd_t = lax.dot_general(vh.astype(BF16), (kh * d_out).astype(BF16),
                                    (((0,), (0,)), ((), ())),
                                    preferred_element_type=F32)
            state_ref[cols, cols] = state_t * d_in[c - 1:c, :] + upd_t
            outs.append(o)
        o_ref[rows, :] = jnp.concatenate(outs, axis=-1)
        return carry

    def finish_tile():
        o = o_ref[...]
        sq = sum(jnp.dot(p, hm_ref[...], preferred_element_type=F32)
                 for p in _split3(o * o))
        inv = lax.rsqrt(sq * (1.0 / HGRN_DIM) + EPS)
        o_ref[...] = o * inv * gain_ref[...] * g_ref[...].astype(F32)

    state0_ref[...] = state_ref[...]
    prepare(0)
    if n_chunks > 1:
        prepare(1)
    state_t = state_ref[...]
    scores = chunk_scores(0)
    for ci in range(n_chunks):
        if ci + 2 < n_chunks:
            prepare(ci + 2)
        scores_next = chunk_scores(ci + 1) if ci + 1 < n_chunks else None
        state_t = chunk_output(ci, *scores, state_t)
        scores = scores_next
    state_ref[...] = state_t
    finish_tile()

    @pl.when(jnp.min(bc_ref[...]) < -HGRN_FACTORED_DECAY_LIMIT)
    def _():
        state_ref[...] = state0_ref[...]
        lax.fori_loop(0, n_chunks, general_chunk, 0)
        finish_tile()


def _hgrn(u_bf16, u_f32, gain, seq_len):
    n = u_bf16.shape[0]
    tile = min(MIXER_TILE, seq_len)
    col = lambda j: pl.BlockSpec((tile, HGRN_WIDTH), lambda i: (i, j))
    q_col = (ATTN_WIDTH + 2 * KV_WIDTH) // HGRN_WIDTH
    em = _hgrn_exponent_matrix()
    channel_head = jnp.arange(HGRN_WIDTH) // HGRN_DIM
    head_mask = (channel_head[:, None] == channel_head[None, :]).astype(BF16)
    return pl.pallas_call(
        functools.partial(_hgrn_kernel, seq_len=seq_len, tile=tile),
        grid=(n // tile,),
        in_specs=[col(q_col), col(0), col(1), col(q_col + 1), col(q_col + 2),
                  _full((1, HGRN_WIDTH)), _full(em.shape), _full(head_mask.shape)],
        out_specs=pl.BlockSpec((tile, HGRN_WIDTH), lambda i: (i, 0)),
        out_shape=jax.ShapeDtypeStruct((n, HGRN_WIDTH), F32),
        scratch_shapes=[pltpu.VMEM((HGRN_WIDTH, HGRN_WIDTH), F32),
                        pltpu.VMEM((HGRN_WIDTH, HGRN_WIDTH), F32),
                        pltpu.VMEM((tile, HGRN_WIDTH), F32)],
        compiler_params=_params(),
        name="hgrn2",
    )(u_bf16, u_f32, u_f32, u_bf16, u_bf16, gain, em, head_mask)


def _s5_prep_kernel(ar_ref, ai_ref, ldt_ref, br_ref, bi_ref,
                    abr_ref, abi_ref, bbr_ref, bbi_ref):
    ar, ai = ar_ref[...], ai_ref[...]
    dt = jnp.exp(ldt_ref[...])
    mag = jnp.exp(ar * dt)
    abar_re = mag * jnp.cos(ai * dt)
    abar_im = mag * jnp.sin(ai * dt)
    nr, ni = abar_re - 1.0, abar_im
    den = ar * ar + ai * ai
    z_re = (nr * ar + ni * ai) / den
    z_im = (ni * ar - nr * ai) / den
    abr_ref[...] = abar_re
    abi_ref[...] = abar_im
    br, bi = br_ref[...], bi_ref[...]
    bbr_ref[...] = z_re[:, None, :] * br - z_im[:, None, :] * bi
    bbi_ref[...] = z_re[:, None, :] * bi + z_im[:, None, :] * br


def _s5_prep(a_re, a_im, log_dt, b_re, b_im):
    gp = jax.ShapeDtypeStruct((S5_GROUPS, S5_STATE), F32)
    gcp = jax.ShapeDtypeStruct((S5_GROUPS, S5_GROUP_CH, S5_STATE), F32)
    return pl.pallas_call(
        _s5_prep_kernel,
        out_shape=(gp, gp, gcp, gcp),
        name="s5_discretize",
    )(a_re, a_im, log_dt[:, None], jnp.swapaxes(b_re, 1, 2), jnp.swapaxes(b_im, 1, 2))


def _block_diag(blocks):
    g, r, c = blocks.shape
    eye = jnp.eye(g, dtype=blocks.dtype)
    return (blocks[:, :, None, :] * eye[:, None, :, None]).reshape(g * r, g * c)


def _s5_kernel(u_ref, a_ref, b_ref, c_ref, d_ref, gw_ref, gb_ref, o_ref,
               xa_ref, xb_ref, ua_ref, ub_ref, state_ref, *, batch, steps):
    step = pl.program_id(0)

    @pl.when(step == 0)
    def _():
        state_ref[...] = jnp.zeros_like(state_ref)
        xb_ref[...] = jnp.zeros_like(xb_ref)
        ub_ref[...] = jnp.zeros_like(ub_ref)

    a_re = jnp.broadcast_to(a_ref[0:1, :], (batch, S5_STATES))
    a_im = jnp.broadcast_to(a_ref[1:2, :], (batch, S5_STATES))

    def pipeline_step(x_new, u_new, x_cur, u_cur):
        u = jnp.swapaxes(u_ref[...], 0, 1).reshape(steps * batch, S5_WIDTH)
        u_new[...] = u
        x_new[...] = _bdot(u, b_ref[...])

        x_re, x_im = state_ref[:, 0:S5_STATES], state_ref[:, S5_STATES:2 * S5_STATES]
        for t in range(steps):
            rows = slice(t * batch, (t + 1) * batch)
            bu_re = x_cur[rows, 0:S5_STATES]
            bu_im = x_cur[rows, S5_STATES:2 * S5_STATES]
            x_re, x_im = (a_re * x_re - a_im * x_im + bu_re,
                          a_re * x_im + a_im * x_re + bu_im)
            x_cur[rows, 0:S5_STATES] = x_re
            x_cur[rows, S5_STATES:2 * S5_STATES] = x_im
        state_ref[:, 0:S5_STATES] = x_re
        state_ref[:, S5_STATES:2 * S5_STATES] = x_im

        y = _bdot(x_cur[...], c_ref[...]) + d_ref[...] * u_cur[...]
        z = jax.nn.gelu(y)
        out = z * jax.nn.sigmoid(_bdot(z, gw_ref[...]) + gb_ref[...])
        o_ref[...] = jnp.swapaxes(out.reshape(steps, batch, S5_WIDTH), 0, 1)

    @pl.when(step % 2 == 0)
    def _():
        pipeline_step(xa_ref, ua_ref, xb_ref, ub_ref)

    @pl.when(step % 2 == 1)
    def _():
        pipeline_step(xb_ref, ub_ref, xa_ref, ua_ref)


def _s5(u, abar, b_blk, c_blk, d_skip, glu_w, glu_b, batch, seq_len):
    steps = min(S5_TIME_TILE, seq_len)
    rows = steps * batch
    n_tiles = seq_len // steps
    u_col = MIX_F32_WIDTH // S5_WIDTH - 1
    x_tile = pltpu.VMEM((rows, 2 * S5_STATES), F32)
    u_tile = pltpu.VMEM((rows, S5_WIDTH), F32)
    return pl.pallas_call(
        functools.partial(_s5_kernel, batch=batch, steps=steps),
        grid=(n_tiles + 1,),
        in_specs=[
            pl.BlockSpec((batch, steps, S5_WIDTH),
                         lambda i: (0, jnp.minimum(i, n_tiles - 1), u_col)),
            _full((2, S5_STATES)),
            _full((S5_WIDTH, 2 * S5_STATES)),
            _full((2 * S5_STATES, S5_WIDTH)),
            _full((1, S5_WIDTH)),
            _full((S5_WIDTH, S5_WIDTH)),
            _full((1, S5_WIDTH)),
        ],
        out_specs=pl.BlockSpec((batch, steps, S5_WIDTH),
                               lambda i: (0, jnp.maximum(i - 1, 0), 0)),
        out_shape=jax.ShapeDtypeStruct((batch, seq_len, S5_WIDTH), F32),
        scratch_shapes=[x_tile, x_tile, u_tile, u_tile,
                        pltpu.VMEM((batch, 2 * S5_STATES), F32)],
        compiler_params=_params(),
        name="s5_ssm",
    )(u, abar, b_blk, c_blk, d_skip, glu_w, glu_b)


def _s5_mixer(u, batch, seq_len, a_re, a_im, log_dt, b_re, b_im, c_re, c_im, d_skip,
              glu_w, glu_b):
    abar_re, abar_im, bbar_re, bbar_im = _s5_prep(a_re, a_im, log_dt, b_re, b_im)
    abar = jnp.stack([abar_re.reshape(-1), abar_im.reshape(-1)])
    b_blk = jnp.concatenate([_block_diag(bbar_re), _block_diag(bbar_im)], axis=1)
    c_blk = jnp.concatenate([_block_diag(jnp.swapaxes(c_re, 1, 2)),
                             -_block_diag(jnp.swapaxes(c_im, 1, 2))], axis=0)
    y = _s5(u.reshape(batch, seq_len, MIX_F32_WIDTH), abar, b_blk.astype(BF16),
            c_blk.astype(BF16), d_skip.reshape(1, S5_WIDTH), glu_w.astype(BF16),
            glu_b.reshape(1, S5_WIDTH), batch, seq_len)
    return y.reshape(batch * seq_len, S5_WIDTH)


def kernel(x, norm_ffn1, ffn1_w_gate, ffn1_w_up, ffn1_w_down, norm_mix, w_in, attn_sinks,
           hgrn_lb_logits, hgrn_norm, s5_a_re, s5_a_im, s5_log_dt, s5_b_re, s5_b_im,
           s5_c_re, s5_c_im, s5_d, s5_glu_w, s5_glu_b, w_out, norm_ffn2, ffn2_w_gate,
           ffn2_w_up, ffn2_w_down, norm_final):
    batch, seq_len, _ = x.shape
    depth = w_in.shape[0]
    h = x.reshape(batch * seq_len, D_MODEL)
    row = lambda v: v.reshape(1, -1)
    final_gain = row(norm_final)
    for layer in range(depth):
        h, u_bf16, u_f32 = _ffn(
            h, None, (row(norm_mix[layer]), w_in, hgrn_lb_logits), row(norm_ffn1[layer]),
            ffn1_w_gate, ffn1_w_up, ffn1_w_down, final_gain, False, layer, seq_len)
        y_a = _attention(u_bf16, attn_sinks[layer], seq_len)
        y_b = _hgrn(u_bf16, u_f32, row(hgrn_norm[layer]), seq_len)
        y_c = _s5_mixer(u_f32, batch, seq_len, s5_a_re[layer], s5_a_im[layer],
                        s5_log_dt[layer], s5_b_re[layer], s5_b_im[layer], s5_c_re[layer],
                        s5_c_im[layer], s5_d[layer], s5_glu_w[layer], s5_glu_b[layer])
        (h,) = _ffn(h, (y_a, y_b, y_c, w_out), None, row(norm_ffn2[layer]), ffn2_w_gate,
                    ffn2_w_up, ffn2_w_down, final_gain, layer == depth - 1, layer, seq_len)
    return h.reshape(batch, seq_len, D_MODEL)
```

```python
import functools
import math

import jax
import jax.numpy as jnp
from jax import lax
from jax.experimental import pallas as pl
from jax.experimental.pallas import tpu as pltpu

D_MODEL = 1024
D_FF = 2816
EPS = 1e-6

ATTN_HEADS = 8
ATTN_KV_HEADS = 2
ATTN_GROUP = ATTN_HEADS // ATTN_KV_HEADS
HEAD_DIM = 64
WINDOW = 128
ATTN_WIDTH = ATTN_HEADS * HEAD_DIM
KV_WIDTH = ATTN_KV_HEADS * HEAD_DIM

HGRN_HEADS = 4
HGRN_DIM = 64
HGRN_WIDTH = HGRN_HEADS * HGRN_DIM
HGRN_CHUNK = 64
HGRN_LEVELS = 6

S5_GROUPS = 16
S5_GROUP_CH = 16
S5_STATE = 64
S5_WIDTH = S5_GROUPS * S5_GROUP_CH
S5_STATES = S5_GROUPS * S5_STATE

IN_PROJ_WIDTH = ATTN_WIDTH + 2 * KV_WIDTH + 4 * HGRN_WIDTH + S5_WIDTH
_F_B_START = ATTN_WIDTH + 2 * KV_WIDTH + HGRN_WIDTH
_U_C_START = IN_PROJ_WIDTH - S5_WIDTH
MIX_BF16_COLUMNS = ((0, _F_B_START), (_F_B_START + HGRN_WIDTH, _U_C_START))
MIX_F32_COLUMNS = ((_F_B_START, _F_B_START + HGRN_WIDTH), (_U_C_START, IN_PROJ_WIDTH))
MIX_BF16_WIDTH = sum(b - a for a, b in MIX_BF16_COLUMNS)
MIX_F32_WIDTH = sum(b - a for a, b in MIX_F32_COLUMNS)

VMEM_LIMIT_BYTES = 56 * 1024 * 1024

TOKEN_TILE = 512
MIXER_TILE = 2048
S5_TIME_TILE = 128
MASK_VALUE = -1e30

F32 = jnp.float32
BF16 = jnp.bfloat16


def _rms(x, gain):
    return x * lax.rsqrt(jnp.mean(x * x, axis=-1, keepdims=True) + EPS) * gain


def _bdot(a, b):
    return jnp.dot(a.astype(BF16), b.astype(BF16), preferred_element_type=F32)


def _params(n_axes=1):
    return pltpu.CompilerParams(
        dimension_semantics=("arbitrary",) * n_axes,
        vmem_limit_bytes=VMEM_LIMIT_BYTES,
    )


def _full(shape):
    return pl.BlockSpec(shape, lambda i: (0,) * len(shape))


WEIGHT_LOAD_STEPS = 8


def _token_tile(step):
    return jnp.maximum(step - WEIGHT_LOAD_STEPS, 0)


def _weight_chunk_spec(w, layer):
    _, rows, cols = w.shape
    return pl.BlockSpec(
        (None, rows // WEIGHT_LOAD_STEPS, cols),
        lambda i: (layer, jnp.minimum(i, WEIGHT_LOAD_STEPS - 1), 0))


def _keep_weight_chunk(step, w_ref, w_s):
    chunk = w_ref.shape[0]
    w_s[pl.ds(pl.multiple_of(step * chunk, chunk), chunk), :] = w_ref[...].astype(BF16)


def _ffn_kernel(*refs, mix, project, final_norm):
    refs = list(refs)
    take = lambda n: [refs.pop(0) for _ in range(n)]
    (x_ref,) = take(1)
    if mix:
        ya_ref, yb_ref, yc_ref, wo_ref = take(4)
    g_ref, wg_ref, wu_ref, wd_ref, gf_ref = take(5)
    if project:
        gm_ref, wi_ref = take(2)
    (o_ref,) = take(1)
    if project:
        ub_ref, uf_ref = take(2)
    if mix:
        (wo_s,) = take(1)
    wg_s, wu_s, wd_s = take(3)
    streamed = [(wg_ref, wg_s), (wu_ref, wu_s), (wd_ref, wd_s)]
    if mix:
        streamed.append((wo_ref, wo_s))
    if project:
        (wi_s,) = take(1)
        streamed.append((wi_ref, wi_s))
    step = pl.program_id(0)

    @pl.when(step < WEIGHT_LOAD_STEPS)
    def _():
        for w_ref, w_s in streamed:
            _keep_weight_chunk(step, w_ref, w_s)

    @pl.when(step >= WEIGHT_LOAD_STEPS)
    def _():
        x = x_ref[...]
        if mix:
            y = jnp.concatenate([ya_ref[...], yb_ref[...], yc_ref[...]], axis=-1)
            x = x + _bdot(y, wo_s[...])
        h = _rms(x, g_ref[...]).astype(BF16)
        gate = jnp.dot(h, wg_s[...], preferred_element_type=F32)
        up = jnp.dot(h, wu_s[...], preferred_element_type=F32)
        act = (gate * jax.nn.sigmoid(gate) * up).astype(BF16)
        y = x + 0.5 * jnp.dot(act, wd_s[...], preferred_element_type=F32)
        if final_norm:
            y = _rms(y, gf_ref[...])
        o_ref[...] = y
        if project:
            u = jnp.dot(_rms(y, gm_ref[...]).astype(BF16), wi_s[...],
                        preferred_element_type=F32)
            for dst, columns in ((ub_ref, MIX_BF16_COLUMNS), (uf_ref, MIX_F32_COLUMNS)):
                dst[...] = jnp.concatenate([u[:, a:b] for a, b in columns],
                                           axis=-1).astype(dst.dtype)


def _ffn(x, mixer_out, in_proj, gain, w_gate, w_up, w_down, final_gain, final_norm,
         layer, seq_len):
    n = x.shape[0]
    tm = min(TOKEN_TILE, seq_len)
    row = lambda width: pl.BlockSpec((tm, width), lambda i: (_token_tile(i), 0))
    vec = _full((1, D_MODEL))
    chunk = lambda w: _weight_chunk_spec(w, layer)
    ffn_weights = [w_gate, w_up, w_down]
    in_specs, args = [row(D_MODEL)], [x]
    out_specs = [row(D_MODEL)]
    out_shape = [jax.ShapeDtypeStruct((n, D_MODEL), F32)]
    scratch = list(ffn_weights)
    if mixer_out is not None:
        y_a, y_b, y_c, w_out = mixer_out
        in_specs += [row(ATTN_WIDTH), row(HGRN_WIDTH), row(S5_WIDTH), chunk(w_out)]
        args += [y_a, y_b, y_c, w_out]
        scratch.insert(0, w_out)
    in_specs += [vec] + [chunk(w) for w in ffn_weights] + [vec]
    args += [gain, *ffn_weights, final_gain]
    if in_proj is not None:
        mix_gain, w_in = in_proj
        in_specs += [vec, chunk(w_in)]
        args += [mix_gain, w_in]
        scratch.append(w_in)
        out_specs += [row(MIX_BF16_WIDTH), row(MIX_F32_WIDTH)]
        out_shape += [jax.ShapeDtypeStruct((n, MIX_BF16_WIDTH), BF16),
                      jax.ShapeDtypeStruct((n, MIX_F32_WIDTH), F32)]
    return pl.pallas_call(
        functools.partial(_ffn_kernel, mix=mixer_out is not None,
                          project=in_proj is not None, final_norm=final_norm),
        grid=(WEIGHT_LOAD_STEPS + n // tm,),
        in_specs=in_specs,
        out_specs=out_specs,
        out_shape=out_shape,
        scratch_shapes=[pltpu.VMEM(w.shape[1:], BF16) for w in scratch],
        compiler_params=_params(),
        name="ffn",
    )(*args)


def _alibi_slope(head):
    return 2.0 ** (-8.0 * (head + 1.0) / ATTN_HEADS)


def _attn_kernel(sink_ref, q_ref, kv_ref, kvp_ref, o_ref, bias_ref, *, seq_len, tile):
    first = (pl.program_id(0) * tile) % seq_len == 0
    cols = ATTN_GROUP * WINDOW
    c_idx = lax.broadcasted_iota(jnp.int32, (2 * WINDOW, cols), 0)
    log2e = math.log2(math.e)
    scale = log2e / math.sqrt(HEAD_DIM)

    @pl.when(pl.program_id(0) == 0)
    def _():
        g_idx = lax.broadcasted_iota(jnp.int32, (1, cols), 1) // WINDOW
        t_idx = lax.broadcasted_iota(jnp.int32, (2 * WINDOW, cols), 1) % WINDOW
        rel = t_idx + WINDOW - c_idx
        in_win = (rel >= 0) & (rel < WINDOW)
        relf = rel.astype(F32)
        for hk in range(ATTN_KV_HEADS):
            slope = jnp.zeros((1, cols), F32)
            sink = jnp.zeros((1, cols), F32)
            for g in range(ATTN_GROUP):
                head = hk * ATTN_GROUP + g
                slope = jnp.where(g_idx == g, _alibi_slope(head) * log2e, slope)
                sink = jnp.where(g_idx == g, sink_ref[head] * log2e, sink)
            bias = jnp.where(in_win, -(slope * relf), MASK_VALUE)
            bias_ref[hk] = jnp.where(c_idx == 0, sink, bias)

    keys = jnp.concatenate([kvp_ref[...], kv_ref[...]], axis=0).astype(F32)
    k_all = keys[:, 0:KV_WIDTH]
    v_t = jnp.transpose(keys[:, KV_WIDTH:2 * KV_WIDTH])
    krow = lax.broadcasted_iota(jnp.int32, (2 * WINDOW, HEAD_DIM), 0)
    vcol = lax.broadcasted_iota(jnp.int32, (HEAD_DIM, 2 * WINDOW), 1)
    ones_rows = jnp.ones((8, 2 * WINDOW), F32)
    no_prev = (c_idx >= 1) & (c_idx < jnp.where(first, WINDOW, 0))

    n_blocks = tile // WINDOW
    units = [(blk, hk) for blk in range(n_blocks) for hk in range(ATTN_KV_HEADS)]
    q_t = [jnp.transpose(q_ref[blk * WINDOW:(blk + 1) * WINDOW, :].astype(F32) * scale)
           .astype(BF16) for blk in range(n_blocks)]
    out_t = [[None] * ATTN_HEADS for _ in range(n_blocks)]

    def scores(blk, hk):
        slots = slice(blk * WINDOW, (blk + 2) * WINDOW)
        dims = slice(hk * HEAD_DIM, (hk + 1) * HEAD_DIM)
        kh = jnp.where(krow == 0, 0.0, k_all[slots, dims]).astype(BF16)
        qs = jnp.concatenate(
            [q_t[blk][h * HEAD_DIM:(h + 1) * HEAD_DIM, :]
             for h in range(hk * ATTN_GROUP, (hk + 1) * ATTN_GROUP)], axis=1)
        s = jnp.dot(kh, qs, preferred_element_type=F32) + bias_ref[hk]
        if blk == 0:
            s = jnp.where(no_prev, MASK_VALUE, s)
        return s

    def weighted_values(blk, hk, s):
        slots = slice(blk * WINDOW, (blk + 2) * WINDOW)
        dims = slice(hk * HEAD_DIM, (hk + 1) * HEAD_DIM)
        vh = jnp.where(vcol == 0, 0.0, v_t[dims, slots])
        vh = jnp.concatenate([vh, ones_rows], axis=0).astype(BF16)
        p = jnp.exp2(s - jnp.max(s, axis=0, keepdims=True)).astype(BF16)
        o = jnp.dot(vh, p, preferred_element_type=F32)
        o = o[:HEAD_DIM] * (1.0 / o[HEAD_DIM:HEAD_DIM + 1])
        for g in range(ATTN_GROUP):
            out_t[blk][hk * ATTN_GROUP + g] = o[:, g * WINDOW:(g + 1) * WINDOW]
        if hk == ATTN_KV_HEADS - 1:
            o_ref[blk * WINDOW:(blk + 1) * WINDOW, :] = jnp.transpose(
                jnp.concatenate(out_t[blk], axis=0)).astype(BF16)

    s = scores(*units[0])
    for i, unit in enumerate(units):
        s_next = scores(*units[i + 1]) if i + 1 < len(units) else None
        weighted_values(*unit, s)
        s = s_next


def _attention(u, sinks, seq_len):
    n = u.shape[0]
    tile = min(MIXER_TILE, seq_len)
    blocks_per_tile = tile // WINDOW
    kv_col = ATTN_WIDTH // (2 * KV_WIDTH)
    return pl.pallas_call(
        functools.partial(_attn_kernel, seq_len=seq_len, tile=tile),
        grid=(n // tile,),
        in_specs=[
            pl.BlockSpec(memory_space=pltpu.SMEM),
            pl.BlockSpec((tile, ATTN_WIDTH), lambda i: (i, 0)),
            pl.BlockSpec((tile, 2 * KV_WIDTH), lambda i: (i, kv_col)),
            pl.BlockSpec((WINDOW, 2 * KV_WIDTH),
                         lambda i: (jnp.maximum(i * blocks_per_tile - 1, 0), kv_col)),
        ],
        out_specs=pl.BlockSpec((tile, ATTN_WIDTH), lambda i: (i, 0)),
        out_shape=jax.ShapeDtypeStruct((n, ATTN_WIDTH), BF16),
        scratch_shapes=[pltpu.VMEM((ATTN_KV_HEADS, 2 * WINDOW, ATTN_GROUP * WINDOW), F32)],
        compiler_params=_params(),
        name="swa_attention",
    )(sinks, u, u, u)


HGRN_FACTORED_DECAY_LIMIT = 80.0


def _hgrn_exponent_matrix():
    c = HGRN_CHUNK
    r = jnp.arange(c)[:, None]
    j = jnp.arange(c)[None, :]
    blocks = []
    for lvl in range(HGRN_LEVELS):
        m = 1 << lvl
        start = (r // (2 * m)) * (2 * m)
        right = (r // m) % 2 == 1
        in_right = right & (j >= start + m) & (j <= r)
        in_left = (~right) & (j > r) & (j < start + m)
        blocks.append(in_right | in_left)
    blocks.append(j <= r)
    blocks.append(j > r)
    return jnp.concatenate(blocks, axis=0).astype(BF16)


def _split3(x):
    hi = x.astype(BF16)
    r1 = x - hi.astype(F32)
    mid = r1.astype(BF16)
    lo = (r1 - mid.astype(F32)).astype(BF16)
    return hi, mid, lo


def _hgrn_kernel(q_ref, f_ref, i_ref, g_ref, lbl_ref, gain_ref, em_ref, hm_ref, o_ref,
                 state_ref, state0_ref, raw_ref, qs_ref, ks_ref, lf_ref, bc_ref, *,
                 seq_len, tile, layer):
    c = HGRN_CHUNK
    first = (pl.program_id(0) * tile) % seq_len == 0

    @pl.when(first)
    def _():
        state_ref[...] = jnp.zeros_like(state_ref)

    logits = lbl_ref[...]
    e = jnp.exp(logits - jnp.max(logits, axis=0, keepdims=True))
    sm = e / jnp.sum(e, axis=0, keepdims=True)
    lb = jnp.zeros((1, HGRN_WIDTH), F32)
    for l in range(1, layer + 1):
        lb = lb + sm[l:l + 1, :]

    r_idx = lax.broadcasted_iota(jnp.int32, (c, 1), 0)
    t_idx = lax.broadcasted_iota(jnp.int32, (c, c), 0)
    s_idx = lax.broadcasted_iota(jnp.int32, (c, c), 1)
    lane_head = lax.broadcasted_iota(jnp.int32, (c, HGRN_WIDTH), 1) // HGRN_DIM
    n_chunks = tile // c

    def chunk_rows(ci):
        return pl.ds(pl.multiple_of(ci * c, c), c)

    def prepare(ci):
        rows = pl.ds(ci * c, c)
        z = f_ref[rows, :]
        q = q_ref[rows, :].astype(F32)
        qs_ref[rows, :] = q * jax.nn.sigmoid(q)
        e = jnp.exp(-jnp.abs(z))
        r = 1.0 + e
        inv = 1.0 / r
        pos = z >= 0.0
        sig = jnp.where(pos, 1.0, e) * inv
        ks_ref[rows, :] = (1.0 - lb) * (jnp.where(pos, e, 1.0) * inv)
        lf = jnp.where(lb > 0.0, jnp.log(lb + (1.0 - lb) * sig),
                       jnp.minimum(z, 0.0) - jnp.log(r))
        lf_ref[rows, :] = lf
        tri = em_ref[HGRN_LEVELS * c:(HGRN_LEVELS + 1) * c, :]
        bc_ref[rows, :] = sum(jnp.dot(tri, p, preferred_element_type=F32)
                              for p in _split3(lf))

    def stack_heads(x):
        return jnp.concatenate(
            [jnp.where(lane_head == h, x, 0.0) for h in range(HGRN_HEADS)],
            axis=0).astype(BF16)

    def chunk_scores(ci):
        rows = pl.ds(ci * c, c)
        bc = bc_ref[rows, :]
        qd = (qs_ref[rows, :] * jnp.exp(bc)).astype(BF16)
        kb = stack_heads(ks_ref[rows, :] * jnp.exp(-bc))
        att = lax.dot_general(qd, kb, (((1,), (1,)), ((), ())),
                              preferred_element_type=F32)
        col_s = lax.broadcasted_iota(jnp.int32, (c, HGRN_WIDTH), 1) % c
        row_t = lax.broadcasted_iota(jnp.int32, (c, HGRN_WIDTH), 0)
        return qd, jnp.where(col_s <= row_t, att, 0.0).astype(BF16)

    def chunk_output(ci, qd, att, state_t):
        rows = pl.ds(ci * c, c)
        k, v, bc = ks_ref[rows, :], i_ref[rows, :].astype(F32), bc_ref[rows, :]
        b_last = bc[c - 1:c, :]
        o = jnp.dot(att, stack_heads(v), preferred_element_type=F32)
        o = o + lax.dot_general(qd, state_t.astype(BF16), (((1,), (1,)), ((), ())),
                                preferred_element_type=F32)
        upd = lax.dot_general(v.astype(BF16), (k * jnp.exp(b_last - bc)).astype(BF16),
                              (((0,), (0,)), ((), ())), preferred_element_type=F32)
        rh = lax.broadcasted_iota(jnp.int32, (HGRN_WIDTH, HGRN_WIDTH), 0) // HGRN_DIM
        ch = lax.broadcasted_iota(jnp.int32, (HGRN_WIDTH, HGRN_WIDTH), 1) // HGRN_DIM
        raw_ref[rows, :] = o
        return state_t * jnp.exp(b_last) + jnp.where(rh == ch, upd, 0.0)

    def general_chunk(ci, carry):
        rows = chunk_rows(ci)
        q, k, v = qs_ref[rows, :], ks_ref[rows, :], i_ref[rows, :].astype(F32)
        expo = sum(jnp.dot(em_ref[...], p, preferred_element_type=F32)
                   for p in _split3(lf_ref[rows, :]))
        decay = jnp.exp(expo)
        outs = []
        for h in range(HGRN_HEADS):
            cols = slice(h * HGRN_DIM, (h + 1) * HGRN_DIM)
            qh, kh, vh = q[:, cols], k[:, cols], v[:, cols]
            att = jnp.where(t_idx == s_idx,
                            jnp.sum(qh * kh, axis=-1, keepdims=True), 0.0)
            for lvl in range(HGRN_LEVELS):
                m = 1 << lvl
                d = decay[lvl * c:(lvl + 1) * c, cols]
                right = (r_idx // m) % 2 == 1
                ql = jnp.where(right, qh * d, 0.0)
                kl = jnp.where(right, 0.0, kh * d)
                a = lax.dot_general(ql.astype(BF16), kl.astype(BF16),
                                    (((1,), (1,)), ((), ())),
                                    preferred_element_type=F32)
                same = (t_idx // (2 * m)) == (s_idx // (2 * m))
                att = att + jnp.where(same, a, 0.0)
            d_in = decay[HGRN_LEVELS * c:(HGRN_LEVELS + 1) * c, cols]
            d_out = decay[(HGRN_LEVELS + 1) * c:(HGRN_LEVELS + 2) * c, cols]
            state_t = state_ref[cols, cols]
            o = _bdot(att, vh) + lax.dot_general(
                (qh * d_in).astype(BF16), state_t.astype(BF16),
                (((1,), (1,)), ((), ())), preferred_element_type=F32)
            upd_t = lax.dot_general(vh.astype(BF16), (kh * d_out).astype(BF16),
                                    (((0,), (0,)), ((), ())),
                                    preferred_element_type=F32)
            state_ref[cols, cols] = state_t * d_in[c - 1:c, :] + upd_t
            outs.append(o)
        raw_ref[rows, :] = jnp.concatenate(outs, axis=-1)
        return carry

    def finish_tile():
        o = raw_ref[...]
        sq = sum(jnp.dot(p, hm_ref[...], preferred_element_type=F32)
                 for p in _split3(o * o))
        inv = lax.rsqrt(sq * (1.0 / HGRN_DIM) + EPS)
        gate = g_ref[...].astype(F32)
        o_ref[...] = (o * inv * gain_ref[...] * (gate * jax.nn.sigmoid(gate))).astype(BF16)

    state0_ref[...] = state_ref[...]
    prepare(0)
    if n_chunks > 1:
        prepare(1)
    state_t = state_ref[...]
    scores = chunk_scores(0)
    for ci in range(n_chunks):
        if ci + 2 < n_chunks:
            prepare(ci + 2)
        scores_next = chunk_scores(ci + 1) if ci + 1 < n_chunks else None
        state_t = chunk_output(ci, *scores, state_t)
        scores = scores_next
    state_ref[...] = state_t
    finish_tile()

    @pl.when(jnp.min(bc_ref[...]) < -HGRN_FACTORED_DECAY_LIMIT)
    def _():
        state_ref[...] = state0_ref[...]
        lax.fori_loop(0, n_chunks, general_chunk, 0)
        finish_tile()


def _hgrn(u_bf16, u_f32, lb_logits, gain, layer, seq_len):
    n = u_bf16.shape[0]
    tile = min(MIXER_TILE, seq_len)
    depth = lb_logits.shape[0]
    col = lambda j: pl.BlockSpec((tile, HGRN_WIDTH), lambda i: (i, j))
    q_col = (ATTN_WIDTH + 2 * KV_WIDTH) // HGRN_WIDTH
    em = _hgrn_exponent_matrix()
    channel_head = jnp.arange(HGRN_WIDTH) // HGRN_DIM
    head_mask = (channel_head[:, None] == channel_head[None, :]).astype(BF16)
    tile_f32 = pltpu.VMEM((tile, HGRN_WIDTH), F32)
    return pl.pallas_call(
        functools.partial(_hgrn_kernel, seq_len=seq_len, tile=tile, layer=layer),
        grid=(n // tile,),
        in_specs=[col(q_col), col(0), col(q_col + 1), col(q_col + 2),
                  _full((depth, HGRN_WIDTH)), _full((1, HGRN_WIDTH)),
                  _full(em.shape), _full(head_mask.shape)],
        out_specs=pl.BlockSpec((tile, HGRN_WIDTH), lambda i: (i, 0)),
        out_shape=jax.ShapeDtypeStruct((n, HGRN_WIDTH), BF16),
        scratch_shapes=[pltpu.VMEM((HGRN_WIDTH, HGRN_WIDTH), F32),
                        pltpu.VMEM((HGRN_WIDTH, HGRN_WIDTH), F32),
                        tile_f32, tile_f32, tile_f32, tile_f32, tile_f32],
        compiler_params=_params(),
        name="hgrn2",
    )(u_bf16, u_f32, u_bf16, u_bf16, lb_logits, gain, em, head_mask)


def _s5_prep_kernel(ar_ref, ai_ref, ldt_ref, br_ref, bi_ref,
                    abr_ref, abi_ref, bbr_ref, bbi_ref):
    ar, ai = ar_ref[...], ai_ref[...]
    dt = jnp.exp(ldt_ref[...])
    mag = jnp.exp(ar * dt)
    abar_re = mag * jnp.cos(ai * dt)
    abar_im = mag * jnp.sin(ai * dt)
    nr, ni = abar_re - 1.0, abar_im
    den = ar * ar + ai * ai
    z_re = (nr * ar + ni * ai) / den
    z_im = (ni * ar - nr * ai) / den
    abr_ref[...] = abar_re
    abi_ref[...] = abar_im
    br, bi = br_ref[...], bi_ref[...]
    bbr_ref[...] = z_re[:, None, :] * br - z_im[:, None, :] * bi
    bbi_ref[...] = z_re[:, None, :] * bi + z_im[:, None, :] * br


def _s5_prep(a_re, a_im, log_dt, b_re, b_im):
    gp = jax.ShapeDtypeStruct((S5_GROUPS, S5_STATE), F32)
    gcp = jax.ShapeDtypeStruct((S5_GROUPS, S5_GROUP_CH, S5_STATE), F32)
    return pl.pallas_call(
        _s5_prep_kernel,
        out_shape=(gp, gp, gcp, gcp),
        name="s5_discretize",
    )(a_re, a_im, log_dt[:, None], jnp.swapaxes(b_re, 1, 2), jnp.swapaxes(b_im, 1, 2))


def _block_diag(blocks):
    g, r, c = blocks.shape
    eye = jnp.eye(g, dtype=blocks.dtype)
    return (blocks[:, :, None, :] * eye[:, None, :, None]).reshape(g * r, g * c)


def _s5_kernel(u_ref, a_ref, b_ref, c_ref, d_ref, gw_ref, gb_ref, o_ref,
               xa_ref, xb_ref, ua_ref, ub_ref, state_ref, *, batch, steps):
    step = pl.program_id(0)

    @pl.when(step == 0)
    def _():
        state_ref[...] = jnp.zeros_like(state_ref)
        xb_ref[...] = jnp.zeros_like(xb_ref)
        ub_ref[...] = jnp.zeros_like(ub_ref)

    a_re = jnp.broadcast_to(a_ref[0:1, :], (batch, S5_STATES))
    a_im = jnp.broadcast_to(a_ref[1:2, :], (batch, S5_STATES))

    def pipeline_step(x_new, u_new, x_cur, u_cur):
        u = jnp.swapaxes(u_ref[...], 0, 1).reshape(steps * batch, S5_WIDTH)
        u_new[...] = u
        x_new[...] = _bdot(u, b_ref[...])

        x_re, x_im = state_ref[:, 0:S5_STATES], state_ref[:, S5_STATES:2 * S5_STATES]
        for t in range(steps):
            rows = slice(t * batch, (t + 1) * batch)
            bu_re = x_cur[rows, 0:S5_STATES]
            bu_im = x_cur[rows, S5_STATES:2 * S5_STATES]
            x_re, x_im = (a_re * x_re - a_im * x_im + bu_re,
                          a_re * x_im + a_im * x_re + bu_im)
            x_cur[rows, 0:S5_STATES] = x_re
            x_cur[rows, S5_STATES:2 * S5_STATES] = x_im
        state_ref[:, 0:S5_STATES] = x_re
        state_ref[:, S5_STATES:2 * S5_STATES] = x_im

        y = _bdot(x_cur[...], c_ref[...]) + d_ref[...] * u_cur[...]
        z = jax.nn.gelu(y)
        out = z * jax.nn.sigmoid(_bdot(z, gw_ref[...]) + gb_ref[...])
        o_ref[...] = jnp.swapaxes(out.reshape(steps, batch, S5_WIDTH), 0, 1).astype(BF16)

    @pl.when(step % 2 == 0)
    def _():
        pipeline_step(xa_ref, ua_ref, xb_ref, ub_ref)

    @pl.when(step % 2 == 1)
    def _():
        pipeline_step(xb_ref, ub_ref, xa_ref, ua_ref)


def _s5(u, abar, b_blk, c_blk, d_skip, glu_w, glu_b, batch, seq_len):
    steps = min(S5_TIME_TILE, seq_len)
    rows = steps * batch
    n_tiles = seq_len // steps
    u_col = MIX_F32_WIDTH // S5_WIDTH - 1
    x_tile = pltpu.VMEM((rows, 2 * S5_STATES), F32)
    u_tile = pltpu.VMEM((rows, S5_WIDTH), F32)
    return pl.pallas_call(
        functools.partial(_s5_kernel, batch=batch, steps=steps),
        grid=(n_tiles + 1,),
        in_specs=[
            pl.BlockSpec((batch, steps, S5_WIDTH),
                         lambda i: (0, jnp.minimum(i, n_tiles - 1), u_col)),
            _full((2, S5_STATES)),
            _full((S5_WIDTH, 2 * S5_STATES)),
            _full((2 * S5_STATES, S5_WIDTH)),
            _full((1, S5_WIDTH)),
            _full((S5_WIDTH, S5_WIDTH)),
            _full((1, S5_WIDTH)),
        ],
        out_specs=pl.BlockSpec((batch, steps, S5_WIDTH),
                               lambda i: (0, jnp.maximum(i - 1, 0), 0)),
        out_shape=jax.ShapeDtypeStruct((batch, seq_len, S5_WIDTH), BF16),
        scratch_shapes=[x_tile, x_tile, u_tile, u_tile,
                        pltpu.VMEM((batch, 2 * S5_STATES), F32)],
        compiler_params=_params(),
        name="s5_ssm",
    )(u, abar, b_blk, c_blk, d_skip, glu_w, glu_b)


def _s5_mixer(u, batch, seq_len, a_re, a_im, log_dt, b_re, b_im, c_re, c_im, d_skip,
              glu_w, glu_b):
    abar_re, abar_im, bbar_re, bbar_im = _s5_prep(a_re, a_im, log_dt, b_re, b_im)
    abar = jnp.stack([abar_re.reshape(-1), abar_im.reshape(-1)])
    b_blk = jnp.concatenate([_block_diag(bbar_re), _block_diag(bbar_im)], axis=1)
    c_blk = jnp.concatenate([_block_diag(jnp.swapaxes(c_re, 1, 2)),
                             -_block_diag(jnp.swapaxes(c_im, 1, 2))], axis=0)
    y = _s5(u.reshape(batch, seq_len, MIX_F32_WIDTH), abar, b_blk.astype(BF16),
            c_blk.astype(BF16), d_skip.reshape(1, S5_WIDTH), glu_w.astype(BF16),
            glu_b.reshape(1, S5_WIDTH), batch, seq_len)
    return y.reshape(batch * seq_len, S5_WIDTH)


def kernel(x, norm_ffn1, ffn1_w_gate, ffn1_w_up, ffn1_w_down, norm_mix, w_in, attn_sinks,
           hgrn_lb_logits, hgrn_norm, s5_a_re, s5_a_im, s5_log_dt, s5_b_re, s5_b_im,
           s5_c_re, s5_c_im, s5_d, s5_glu_w, s5_glu_b, w_out, norm_ffn2, ffn2_w_gate,
           ffn2_w_up, ffn2_w_down, norm_final):
    batch, seq_len, _ = x.shape
    depth = w_in.shape[0]
    h = x.reshape(batch * seq_len, D_MODEL)
    row = lambda v: v.reshape(1, -1)
    final_gain = row(norm_final)
    for layer in range(depth):
        h, u_bf16, u_f32 = _ffn(
            h, None, (row(norm_mix[layer]), w_in), row(norm_ffn1[layer]), ffn1_w_gate,
            ffn1_w_up, ffn1_w_down, final_gain, False, layer, seq_len)
        y_a = _attention(u_bf16, attn_sinks[layer], seq_len)
        y_b = _hgrn(u_bf16, u_f32, hgrn_lb_logits, row(hgrn_norm[layer]), layer, seq_len)
        y_c = _s5_mixer(u_f32, batch, seq_len, s5_a_re[layer], s5_a_im[layer],
                        s5_log_dt[layer], s5_b_re[layer], s5_b_im[layer], s5_c_re[layer],
                        s5_c_im[layer], s5_d[layer], s5_glu_w[layer], s5_glu_b[layer])
        (h,) = _ffn(h, (y_a, y_b, y_c, w_out), None, row(norm_ffn2[layer]), ffn2_w_gate,
                    ffn2_w_up, ffn2_w_down, final_gain, layer == depth - 1, layer, seq_len)
    return h.reshape(batch, seq_len, D_MODEL)
```

```python
import functools
import math

import jax
import jax.numpy as jnp
from jax import lax
from jax.experimental import pallas as pl
from jax.experimental.pallas import tpu as pltpu

D_MODEL = 1024
D_FF = 2816
EPS = 1e-6

ATTN_HEADS = 8
ATTN_KV_HEADS = 2
ATTN_GROUP = ATTN_HEADS // ATTN_KV_HEADS
HEAD_DIM = 64
WINDOW = 128
ATTN_WIDTH = ATTN_HEADS * HEAD_DIM
KV_WIDTH = ATTN_KV_HEADS * HEAD_DIM

HGRN_HEADS = 4
HGRN_DIM = 64
HGRN_WIDTH = HGRN_HEADS * HGRN_DIM
HGRN_CHUNK = 64
HGRN_LEVELS = 6

S5_GROUPS = 16
S5_GROUP_CH = 16
S5_STATE = 64
S5_WIDTH = S5_GROUPS * S5_GROUP_CH
S5_STATES = S5_GROUPS * S5_STATE

IN_PROJ_WIDTH = ATTN_WIDTH + 2 * KV_WIDTH + 4 * HGRN_WIDTH + S5_WIDTH
_F_B_START = ATTN_WIDTH + 2 * KV_WIDTH + HGRN_WIDTH
_U_C_START = IN_PROJ_WIDTH - S5_WIDTH
MIX_BF16_COLUMNS = ((0, _F_B_START), (_F_B_START + HGRN_WIDTH, _U_C_START))
MIX_F32_COLUMNS = ((_F_B_START, _F_B_START + HGRN_WIDTH), (_U_C_START, IN_PROJ_WIDTH))
MIX_BF16_WIDTH = sum(b - a for a, b in MIX_BF16_COLUMNS)
MIX_F32_WIDTH = sum(b - a for a, b in MIX_F32_COLUMNS)

VMEM_LIMIT_BYTES = 56 * 1024 * 1024
SUBLANES = 8

TOKEN_TILE = 512
MIXER_TILE = 2048
S5_TIME_TILE = 128
MASK_VALUE = -1e30

F32 = jnp.float32
BF16 = jnp.bfloat16


def _rms(x, gain):
    return x * lax.rsqrt(jnp.mean(x * x, axis=-1, keepdims=True) + EPS) * gain


def _bdot(a, b):
    return jnp.dot(a.astype(BF16), b.astype(BF16), preferred_element_type=F32)


def _params(n_axes=1):
    return pltpu.CompilerParams(
        dimension_semantics=("arbitrary",) * n_axes,
        vmem_limit_bytes=VMEM_LIMIT_BYTES,
    )


def _full(shape):
    return pl.BlockSpec(shape, lambda i: (0,) * len(shape))


WEIGHT_LOAD_STEPS = 8


def _token_tile(step):
    return jnp.maximum(step - WEIGHT_LOAD_STEPS, 0)


def _weight_chunk_spec(w, layer):
    _, rows, cols = w.shape
    return pl.BlockSpec(
        (None, rows // WEIGHT_LOAD_STEPS, cols),
        lambda i: (layer, jnp.minimum(i, WEIGHT_LOAD_STEPS - 1), 0))


def _keep_weight_chunk(step, w_ref, w_s):
    chunk = w_ref.shape[0]
    w_s[pl.ds(pl.multiple_of(step * chunk, chunk), chunk), :] = w_ref[...].astype(BF16)


def _ffn_kernel(*refs, mix, project, final_norm):
    refs = list(refs)
    take = lambda n: [refs.pop(0) for _ in range(n)]
    (x_ref,) = take(1)
    if mix:
        ya_ref, yb_ref, yc_ref, wo_ref = take(4)
    g_ref, wg_ref, wu_ref, wd_ref, gf_ref = take(5)
    if project:
        gm_ref, wi_ref = take(2)
    (o_ref,) = take(1)
    if project:
        ub_ref, uf_ref = take(2)
    if mix:
        (wo_s,) = take(1)
    wg_s, wu_s, wd_s = take(3)
    streamed = [(wg_ref, wg_s), (wu_ref, wu_s), (wd_ref, wd_s)]
    if mix:
        streamed.append((wo_ref, wo_s))
    if project:
        (wi_s,) = take(1)
        streamed.append((wi_ref, wi_s))
    step = pl.program_id(0)

    @pl.when(step < WEIGHT_LOAD_STEPS)
    def _():
        for w_ref, w_s in streamed:
            _keep_weight_chunk(step, w_ref, w_s)

    @pl.when(step >= WEIGHT_LOAD_STEPS)
    def _():
        x = x_ref[...]
        if mix:
            y = jnp.concatenate([ya_ref[...], yb_ref[...], yc_ref[...]], axis=-1)
            x = x + _bdot(y, wo_s[...])
        h = _rms(x, g_ref[...]).astype(BF16)
        gate = jnp.dot(h, wg_s[...], preferred_element_type=F32)
        up = jnp.dot(h, wu_s[...], preferred_element_type=F32)
        act = (gate * jax.nn.sigmoid(gate) * up).astype(BF16)
        y = x + 0.5 * jnp.dot(act, wd_s[...], preferred_element_type=F32)
        if final_norm:
            y = _rms(y, gf_ref[...])
        o_ref[...] = y
        if project:
            u = jnp.dot(_rms(y, gm_ref[...]).astype(BF16), wi_s[...],
                        preferred_element_type=F32)
            for dst, columns in ((ub_ref, MIX_BF16_COLUMNS), (uf_ref, MIX_F32_COLUMNS)):
                dst[...] = jnp.concatenate([u[:, a:b] for a, b in columns],
                                           axis=-1).astype(dst.dtype)


def _ffn(x, mixer_out, in_proj, gain, w_gate, w_up, w_down, final_gain, final_norm,
         layer, seq_len):
    n = x.shape[0]
    tm = min(TOKEN_TILE, seq_len)
    row = lambda width: pl.BlockSpec((tm, width), lambda i: (_token_tile(i), 0))
    vec = _full((1, D_MODEL))
    chunk = lambda w: _weight_chunk_spec(w, layer)
    ffn_weights = [w_gate, w_up, w_down]
    in_specs, args = [row(D_MODEL)], [x]
    out_specs = [row(D_MODEL)]
    out_shape = [jax.ShapeDtypeStruct((n, D_MODEL), F32)]
    scratch = list(ffn_weights)
    if mixer_out is not None:
        y_a, y_b, y_c, w_out = mixer_out
        in_specs += [row(ATTN_WIDTH), row(HGRN_WIDTH), row(S5_WIDTH), chunk(w_out)]
        args += [y_a, y_b, y_c, w_out]
        scratch.insert(0, w_out)
    in_specs += [vec] + [chunk(w) for w in ffn_weights] + [vec]
    args += [gain, *ffn_weights, final_gain]
    if in_proj is not None:
        mix_gain, w_in = in_proj
        in_specs += [vec, chunk(w_in)]
        args += [mix_gain, w_in]
        scratch.append(w_in)
        out_specs += [row(MIX_BF16_WIDTH), row(MIX_F32_WIDTH)]
        out_shape += [jax.ShapeDtypeStruct((n, MIX_BF16_WIDTH), BF16),
                      jax.ShapeDtypeStruct((n, MIX_F32_WIDTH), F32)]
    return pl.pallas_call(
        functools.partial(_ffn_kernel, mix=mixer_out is not None,
                          project=in_proj is not None, final_norm=final_norm),
        grid=(WEIGHT_LOAD_STEPS + n // tm,),
        in_specs=in_specs,
        out_specs=out_specs,
        out_shape=out_shape,
        scratch_shapes=[pltpu.VMEM(w.shape[1:], BF16) for w in scratch],
        compiler_params=_params(),
        name="ffn",
    )(*args)


def _alibi_slope(head):
    return 2.0 ** (-8.0 * (head + 1.0) / ATTN_HEADS)


def _attn_kernel(sink_ref, q_ref, kv_ref, kvp_ref, o_ref, bias_ref, *, seq_len, tile):
    first = (pl.program_id(0) * tile) % seq_len == 0
    cols = ATTN_GROUP * WINDOW
    c_idx = lax.broadcasted_iota(jnp.int32, (2 * WINDOW, cols), 0)
    log2e = math.log2(math.e)
    scale = log2e / math.sqrt(HEAD_DIM)

    @pl.when(pl.program_id(0) == 0)
    def _():
        g_idx = lax.broadcasted_iota(jnp.int32, (1, cols), 1) // WINDOW
        t_idx = lax.broadcasted_iota(jnp.int32, (2 * WINDOW, cols), 1) % WINDOW
        rel = t_idx + WINDOW - c_idx
        in_win = (rel >= 0) & (rel < WINDOW)
        relf = rel.astype(F32)
        for hk in range(ATTN_KV_HEADS):
            slope = jnp.zeros((1, cols), F32)
            sink = jnp.zeros((1, cols), F32)
            for g in range(ATTN_GROUP):
                head = hk * ATTN_GROUP + g
                slope = jnp.where(g_idx == g, _alibi_slope(head) * log2e, slope)
                sink = jnp.where(g_idx == g, sink_ref[head] * log2e, sink)
            bias = jnp.where(in_win, -(slope * relf), MASK_VALUE)
            bias_ref[hk] = jnp.where(c_idx == 0, sink, bias)

    keys = jnp.concatenate([kvp_ref[...], kv_ref[...]], axis=0).astype(F32)
    k_all = keys[:, 0:KV_WIDTH]
    v_t = jnp.transpose(keys[:, KV_WIDTH:2 * KV_WIDTH])
    krow = lax.broadcasted_iota(jnp.int32, (2 * WINDOW, HEAD_DIM), 0)
    vcol = lax.broadcasted_iota(jnp.int32, (HEAD_DIM, 2 * WINDOW), 1)
    ones_rows = jnp.ones((SUBLANES, 2 * WINDOW), F32)
    no_prev = (c_idx >= 1) & (c_idx < jnp.where(first, WINDOW, 0))

    n_blocks = tile // WINDOW
    units = [(blk, hk) for blk in range(n_blocks) for hk in range(ATTN_KV_HEADS)]
    q_t = [jnp.transpose(q_ref[blk * WINDOW:(blk + 1) * WINDOW, :].astype(F32) * scale)
           .astype(BF16) for blk in range(n_blocks)]
    out_t = [[None] * ATTN_HEADS for _ in range(n_blocks)]

    def scores(blk, hk):
        slots = slice(blk * WINDOW, (blk + 2) * WINDOW)
        dims = slice(hk * HEAD_DIM, (hk + 1) * HEAD_DIM)
        kh = jnp.where(krow == 0, 0.0, k_all[slots, dims]).astype(BF16)
        qs = jnp.concatenate(
            [q_t[blk][h * HEAD_DIM:(h + 1) * HEAD_DIM, :]
             for h in range(hk * ATTN_GROUP, (hk + 1) * ATTN_GROUP)], axis=1)
        s = jnp.dot(kh, qs, preferred_element_type=F32) + bias_ref[hk]
        if blk == 0:
            s = jnp.where(no_prev, MASK_VALUE, s)
        return s

    def weighted_values(blk, hk, s):
        slots = slice(blk * WINDOW, (blk + 2) * WINDOW)
        dims = slice(hk * HEAD_DIM, (hk + 1) * HEAD_DIM)
        vh = jnp.where(vcol == 0, 0.0, v_t[dims, slots])
        vh = jnp.concatenate([vh, ones_rows], axis=0).astype(BF16)
        p = jnp.exp2(s - jnp.max(s, axis=0, keepdims=True)).astype(BF16)
        o = jnp.dot(vh, p, preferred_element_type=F32)
        o = o[:HEAD_DIM] * (1.0 / o[HEAD_DIM:HEAD_DIM + 1])
        for g in range(ATTN_GROUP):
            out_t[blk][hk * ATTN_GROUP + g] = o[:, g * WINDOW:(g + 1) * WINDOW]
        if hk == ATTN_KV_HEADS - 1:
            o_ref[blk * WINDOW:(blk + 1) * WINDOW, :] = jnp.transpose(
                jnp.concatenate(out_t[blk], axis=0))

    s = scores(*units[0])
    for i, unit in enumerate(units):
        s_next = scores(*units[i + 1]) if i + 1 < len(units) else None
        weighted_values(*unit, s)
        s = s_next


def _attention(u, sinks, seq_len):
    n = u.shape[0]
    tile = min(MIXER_TILE, seq_len)
    blocks_per_tile = tile // WINDOW
    kv_col = ATTN_WIDTH // (2 * KV_WIDTH)
    return pl.pallas_call(
        functools.partial(_attn_kernel, seq_len=seq_len, tile=tile),
        grid=(n // tile,),
        in_specs=[
            pl.BlockSpec(memory_space=pltpu.SMEM),
            pl.BlockSpec((tile, ATTN_WIDTH), lambda i: (i, 0)),
            pl.BlockSpec((tile, 2 * KV_WIDTH), lambda i: (i, kv_col)),
            pl.BlockSpec((WINDOW, 2 * KV_WIDTH),
                         lambda i: (jnp.maximum(i * blocks_per_tile - 1, 0), kv_col)),
        ],
        out_specs=pl.BlockSpec((tile, ATTN_WIDTH), lambda i: (i, 0)),
        out_shape=jax.ShapeDtypeStruct((n, ATTN_WIDTH), F32),
        scratch_shapes=[pltpu.VMEM((ATTN_KV_HEADS, 2 * WINDOW, ATTN_GROUP * WINDOW), F32)],
        compiler_params=_params(),
        name="swa_attention",
    )(sinks, u, u, u)


HGRN_FACTORED_DECAY_LIMIT = 80.0


def _hgrn_exponent_matrix():
    c = HGRN_CHUNK
    r = jnp.arange(c)[:, None]
    j = jnp.arange(c)[None, :]
    blocks = []
    for lvl in range(HGRN_LEVELS):
        m = 1 << lvl
        start = (r // (2 * m)) * (2 * m)
        right = (r // m) % 2 == 1
        in_right = right & (j >= start + m) & (j <= r)
        in_left = (~right) & (j > r) & (j < start + m)
        blocks.append(in_right | in_left)
    blocks.append(j <= r)
    blocks.append(j > r)
    return jnp.concatenate(blocks, axis=0).astype(BF16)


def _split3(x):
    hi = x.astype(BF16)
    r1 = x - hi.astype(F32)
    mid = r1.astype(BF16)
    lo = (r1 - mid.astype(F32)).astype(BF16)
    return hi, mid, lo


def _hgrn_kernel(q_ref, f_ref, i_ref, g_ref, lbl_ref, gain_ref, em_ref, hm_ref, o_ref,
                 state_ref, state0_ref, qs_ref, ks_ref, lf_ref, bc_ref, *,
                 seq_len, tile, layer):
    c = HGRN_CHUNK
    first = (pl.program_id(0) * tile) % seq_len == 0

    @pl.when(first)
    def _():
        state_ref[...] = jnp.zeros_like(state_ref)

    logits = lbl_ref[...]
    e = jnp.exp(logits - jnp.max(logits, axis=0, keepdims=True))
    sm = e / jnp.sum(e, axis=0, keepdims=True)
    lb = jnp.zeros((1, HGRN_WIDTH), F32)
    for l in range(1, layer + 1):
        lb = lb + sm[l:l + 1, :]

    r_idx = lax.broadcasted_iota(jnp.int32, (c, 1), 0)
    t_idx = lax.broadcasted_iota(jnp.int32, (c, c), 0)
    s_idx = lax.broadcasted_iota(jnp.int32, (c, c), 1)
    lane_head = lax.broadcasted_iota(jnp.int32, (c, HGRN_WIDTH), 1) // HGRN_DIM
    n_chunks = tile // c

    def chunk_rows(ci):
        return pl.ds(pl.multiple_of(ci * c, c), c)

    def prepare(ci):
        rows = pl.ds(ci * c, c)
        z = f_ref[rows, :]
        q = q_ref[rows, :].astype(F32)
        qs_ref[rows, :] = q * jax.nn.sigmoid(q)
        e = jnp.exp(-jnp.abs(z))
        r = 1.0 + e
        inv = 1.0 / r
        pos = z >= 0.0
        sig = jnp.where(pos, 1.0, e) * inv
        ks_ref[rows, :] = (1.0 - lb) * (jnp.where(pos, e, 1.0) * inv)
        lf = jnp.where(lb > 0.0, jnp.log(lb + (1.0 - lb) * sig),
                       jnp.minimum(z, 0.0) - jnp.log(r))
        lf_ref[rows, :] = lf
        tri = em_ref[HGRN_LEVELS * c:(HGRN_LEVELS + 1) * c, :]
        bc_ref[rows, :] = sum(jnp.dot(tri, p, preferred_element_type=F32)
                              for p in _split3(lf))

    def stack_heads(x):
        return jnp.concatenate(
            [jnp.where(lane_head == h, x, 0.0) for h in range(HGRN_HEADS)],
            axis=0).astype(BF16)

    def chunk_scores(ci):
        rows = pl.ds(ci * c, c)
        bc = bc_ref[rows, :]
        qd = (qs_ref[rows, :] * jnp.exp(bc)).astype(BF16)
        kb = stack_heads(ks_ref[rows, :] * jnp.exp(-bc))
        att = lax.dot_general(qd, kb, (((1,), (1,)), ((), ())),
                              preferred_element_type=F32)
        col_s = lax.broadcasted_iota(jnp.int32, (c, HGRN_WIDTH), 1) % c
        row_t = lax.broadcasted_iota(jnp.int32, (c, HGRN_WIDTH), 0)
        return qd, jnp.where(col_s <= row_t, att, 0.0).astype(BF16)

    def chunk_output(ci, qd, att, state_t):
        rows = pl.ds(ci * c, c)
        k, v, bc = ks_ref[rows, :], i_ref[rows, :].astype(F32), bc_ref[rows, :]
        b_last = bc[c - 1:c, :]
        o = jnp.dot(att, stack_heads(v), preferred_element_type=F32)
        o = o + lax.dot_general(qd, state_t.astype(BF16), (((1,), (1,)), ((), ())),
                                preferred_element_type=F32)
        upd = lax.dot_general(v.astype(BF16), (k * jnp.exp(b_last - bc)).astype(BF16),
                              (((0,), (0,)), ((), ())), preferred_element_type=F32)
        rh = lax.broadcasted_iota(jnp.int32, (HGRN_WIDTH, HGRN_WIDTH), 0) // HGRN_DIM
        ch = lax.broadcasted_iota(jnp.int32, (HGRN_WIDTH, HGRN_WIDTH), 1) // HGRN_DIM
        o_ref[rows, :] = o
        return state_t * jnp.exp(b_last) + jnp.where(rh == ch, upd, 0.0)

    def general_chunk(ci, carry):
        rows = chunk_rows(ci)
        q, k, v = qs_ref[rows, :], ks_ref[rows, :], i_ref[rows, :].astype(F32)
        expo = sum(jnp.dot(em_ref[...], p, preferred_element_type=F32)
                   for p in _split3(lf_ref[rows, :]))
        decay = jnp.exp(expo)
        outs = []
        for h in range(HGRN_HEADS):
            cols = slice(h * HGRN_DIM, (h + 1) * HGRN_DIM)
            qh, kh, vh = q[:, cols], k[:, cols], v[:, cols]
            att = jnp.where(t_idx == s_idx,
                            jnp.sum(qh * kh, axis=-1, keepdims=True), 0.0)
            for lvl in range(HGRN_LEVELS):
                m = 1 << lvl
                d = decay[lvl * c:(lvl + 1) * c, cols]
                right = (r_idx // m) % 2 == 1
                ql = jnp.where(right, qh * d, 0.0)
                kl = jnp.where(right, 0.0, kh * d)
                a = lax.dot_general(ql.astype(BF16), kl.astype(BF16),
                                    (((1,), (1,)), ((), ())),
                                    preferred_element_type=F32)
                same = (t_idx // (2 * m)) == (s_idx // (2 * m))
                att = att + jnp.where(same, a, 0.0)
            d_in = decay[HGRN_LEVELS * c:(HGRN_LEVELS + 1) * c, cols]
            d_out = decay[(HGRN_LEVELS + 1) * c:(HGRN_LEVELS + 2) * c, cols]
            state_t = state_ref[cols, cols]
            o = _bdot(att, vh) + lax.dot_general(
                (qh * d_in).astype(BF16), state_t.astype(BF16),
                (((1,), (1,)), ((), ())), preferred_element_type=F32)
            upd_t = lax.dot_general(vh.astype(BF16), (kh * d_out).astype(BF16),
                                    (((0,), (0,)), ((), ())),
                                    preferred_element_type=F32)
            state_ref[cols, cols] = state_t * d_in[c - 1:c, :] + upd_t
            outs.append(o)
        o_ref[rows, :] = jnp.concatenate(outs, axis=-1)
        return carry

    def finish_tile():
        o = o_ref[...]
        sq = sum(jnp.dot(p, hm_ref[...], preferred_element_type=F32)
                 for p in _split3(o * o))
        inv = lax.rsqrt(sq * (1.0 / HGRN_DIM) + EPS)
        gate = g_ref[...].astype(F32)
        o_ref[...] = o * inv * gain_ref[...] * (gate * jax.nn.sigmoid(gate))

    state0_ref[...] = state_ref[...]
    prepare(0)
    if n_chunks > 1:
        prepare(1)
    state_t = state_ref[...]
    scores = chunk_scores(0)
    for ci in range(n_chunks):
        if ci + 2 < n_chunks:
            prepare(ci + 2)
        scores_next = chunk_scores(ci + 1) if ci + 1 < n_chunks else None
        state_t = chunk_output(ci, *scores, state_t)
        scores = scores_next
    state_ref[...] = state_t
    finish_tile()

    @pl.when(jnp.min(bc_ref[...]) < -HGRN_FACTORED_DECAY_LIMIT)
    def _():
        state_ref[...] = state0_ref[...]
        lax.fori_loop(0, n_chunks, general_chunk, 0)
        finish_tile()


def _hgrn(u_bf16, u_f32, lb_logits, gain, layer, seq_len):
    n = u_bf16.shape[0]
    tile = min(MIXER_TILE, seq_len)
    depth = lb_logits.shape[0]
    col = lambda j: pl.BlockSpec((tile, HGRN_WIDTH), lambda i: (i, j))
    q_col = (ATTN_WIDTH + 2 * KV_WIDTH) // HGRN_WIDTH
    em = _hgrn_exponent_matrix()
    channel_head = jnp.arange(HGRN_WIDTH) // HGRN_DIM
    head_mask = (channel_head[:, None] == channel_head[None, :]).astype(BF16)
    tile_f32 = pltpu.VMEM((tile, HGRN_WIDTH), F32)
    return pl.pallas_call(
        functools.partial(_hgrn_kernel, seq_len=seq_len, tile=tile, layer=layer),
        grid=(n // tile,),
        in_specs=[col(q_col), col(0), col(q_col + 1), col(q_col + 2),
                  _full((depth, HGRN_WIDTH)), _full((1, HGRN_WIDTH)),
                  _full(em.shape), _full(head_mask.shape)],
        out_specs=pl.BlockSpec((tile, HGRN_WIDTH), lambda i: (i, 0)),
        out_shape=jax.ShapeDtypeStruct((n, HGRN_WIDTH), F32),
        scratch_shapes=[pltpu.VMEM((HGRN_WIDTH, HGRN_WIDTH), F32),
                        pltpu.VMEM((HGRN_WIDTH, HGRN_WIDTH), F32),
                        tile_f32, tile_f32, tile_f32, tile_f32],
        compiler_params=_params(),
        name="hgrn2",
    )(u_bf16, u_f32, u_bf16, u_bf16, lb_logits, gain, em, head_mask)


def _s5_prep_kernel(ar_ref, ai_ref, ldt_ref, br_ref, bi_ref,
                    abr_ref, abi_ref, bbr_ref, bbi_ref):
    ar, ai = ar_ref[...], ai_ref[...]
    dt = jnp.exp(ldt_ref[...])
    mag = jnp.exp(ar * dt)
    abar_re = mag * jnp.cos(ai * dt)
    abar_im = mag * jnp.sin(ai * dt)
    nr, ni = abar_re - 1.0, abar_im
    den = ar * ar + ai * ai
    z_re = (nr * ar + ni * ai) / den
    z_im = (ni * ar - nr * ai) / den
    abr_ref[...] = abar_re
    abi_ref[...] = abar_im
    br, bi = br_ref[...], bi_ref[...]
    bbr_ref[...] = z_re[:, None, :] * br - z_im[:, None, :] * bi
    bbi_ref[...] = z_re[:, None, :] * bi + z_im[:, None, :] * br


def _s5_prep(a_re, a_im, log_dt, b_re, b_im):
    gp = jax.ShapeDtypeStruct((S5_GROUPS, S5_STATE), F32)
    gcp = jax.ShapeDtypeStruct((S5_GROUPS, S5_GROUP_CH, S5_STATE), F32)
    return pl.pallas_call(
        _s5_prep_kernel,
        out_shape=(gp, gp, gcp, gcp),
        name="s5_discretize",
    )(a_re, a_im, log_dt[:, None], jnp.swapaxes(b_re, 1, 2), jnp.swapaxes(b_im, 1, 2))


def _block_diag(blocks):
    g, r, c = blocks.shape
    eye = jnp.eye(g, dtype=blocks.dtype)
    return (blocks[:, :, None, :] * eye[:, None, :, None]).reshape(g * r, g * c)


def _s5_kernel(u_ref, a_ref, b_ref, c_ref, d_ref, gw_ref, gb_ref, o_ref,
               xa_ref, xb_ref, ua_ref, ub_ref, state_ref, *, batch, steps):
    step = pl.program_id(0)

    @pl.when(step == 0)
    def _():
        state_ref[...] = jnp.zeros_like(state_ref)
        xb_ref[...] = jnp.zeros_like(xb_ref)
        ub_ref[...] = jnp.zeros_like(ub_ref)

    a_re = jnp.broadcast_to(a_ref[0:1, :], (batch, S5_STATES))
    a_im = jnp.broadcast_to(a_ref[1:2, :], (batch, S5_STATES))

    def pipeline_step(x_new, u_new, x_cur, u_cur):
        u = jnp.swapaxes(u_ref[...], 0, 1).reshape(steps * batch, S5_WIDTH)
        u_new[...] = u
        x_new[...] = _bdot(u, b_ref[...])

        x_re, x_im = state_ref[:, 0:S5_STATES], state_ref[:, S5_STATES:2 * S5_STATES]
        for t in range(steps):
            rows = slice(t * batch, (t + 1) * batch)
            bu_re = x_cur[rows, 0:S5_STATES]
            bu_im = x_cur[rows, S5_STATES:2 * S5_STATES]
            x_re, x_im = (a_re * x_re - a_im * x_im + bu_re,
                          a_re * x_im + a_im * x_re + bu_im)
            x_cur[rows, 0:S5_STATES] = x_re
            x_cur[rows, S5_STATES:2 * S5_STATES] = x_im
        state_ref[:, 0:S5_STATES] = x_re
        state_ref[:, S5_STATES:2 * S5_STATES] = x_im

        y = _bdot(x_cur[...], c_ref[...]) + d_ref[...] * u_cur[...]
        z = jax.nn.gelu(y)
        out = z * jax.nn.sigmoid(_bdot(z, gw_ref[...]) + gb_ref[...])
        o_ref[...] = jnp.swapaxes(out.reshape(steps, batch, S5_WIDTH), 0, 1)

    @pl.when(step % 2 == 0)
    def _():
        pipeline_step(xa_ref, ua_ref, xb_ref, ub_ref)

    @pl.when(step % 2 == 1)
    def _():
        pipeline_step(xb_ref, ub_ref, xa_ref, ua_ref)


def _s5(u, abar, b_blk, c_blk, d_skip, glu_w, glu_b, batch, seq_len):
    steps = min(S5_TIME_TILE, seq_len)
    rows = steps * batch
    n_tiles = seq_len // steps
    u_col = MIX_F32_WIDTH // S5_WIDTH - 1
    x_tile = pltpu.VMEM((rows, 2 * S5_STATES), F32)
    u_tile = pltpu.VMEM((rows, S5_WIDTH), F32)
    return pl.pallas_call(
        functools.partial(_s5_kernel, batch=batch, steps=steps),
        grid=(n_tiles + 1,),
        in_specs=[
            pl.BlockSpec((batch, steps, S5_WIDTH),
                         lambda i: (0, jnp.minimum(i, n_tiles - 1), u_col)),
            _full((2, S5_STATES)),
            _full((S5_WIDTH, 2 * S5_STATES)),
            _full((2 * S5_STATES, S5_WIDTH)),
            _full((1, S5_WIDTH)),
            _full((S5_WIDTH, S5_WIDTH)),
            _full((1, S5_WIDTH)),
        ],
        out_specs=pl.BlockSpec((batch, steps, S5_WIDTH),
                               lambda i: (0, jnp.maximum(i - 1, 0), 0)),
        out_shape=jax.ShapeDtypeStruct((batch, seq_len, S5_WIDTH), F32),
        scratch_shapes=[x_tile, x_tile, u_tile, u_tile,
                        pltpu.VMEM((batch, 2 * S5_STATES), F32)],
        compiler_params=_params(),
        name="s5_ssm",
    )(u, abar, b_blk, c_blk, d_skip, glu_w, glu_b)


def _s5_mixer(u, batch, seq_len, a_re, a_im, log_dt, b_re, b_im, c_re, c_im, d_skip,
              glu_w, glu_b):
    abar_re, abar_im, bbar_re, bbar_im = _s5_prep(a_re, a_im, log_dt, b_re, b_im)
    abar = jnp.stack([abar_re.reshape(-1), abar_im.reshape(-1)])
    b_blk = jnp.concatenate([_block_diag(bbar_re), _block_diag(bbar_im)], axis=1)
    c_blk = jnp.concatenate([_block_diag(jnp.swapaxes(c_re, 1, 2)),
                             -_block_diag(jnp.swapaxes(c_im, 1, 2))], axis=0)
    y = _s5(u.reshape(batch, seq_len, MIX_F32_WIDTH), abar, b_blk.astype(BF16),
            c_blk.astype(BF16), d_skip.reshape(1, S5_WIDTH), glu_w.astype(BF16),
            glu_b.reshape(1, S5_WIDTH), batch, seq_len)
    return y.reshape(batch * seq_len, S5_WIDTH)


def kernel(x, norm_ffn1, ffn1_w_gate, ffn1_w_up, ffn1_w_down, norm_mix, w_in, attn_sinks,
           hgrn_lb_logits, hgrn_norm, s5_a_re, s5_a_im, s5_log_dt, s5_b_re, s5_b_im,
           s5_c_re, s5_c_im, s5_d, s5_glu_w, s5_glu_b, w_out, norm_ffn2, ffn2_w_gate,
           ffn2_w_up, ffn2_w_down, norm_final):
    batch, seq_len, _ = x.shape
    depth = w_in.shape[0]
    h = x.reshape(batch * seq_len, D_MODEL)
    row = lambda v: v.reshape(1, -1)
    final_gain = row(norm_final)
    for layer in range(depth):
        h, u_bf16, u_f32 = _ffn(
            h, None, (row(norm_mix[layer]), w_in), row(norm_ffn1[layer]), ffn1_w_gate,
            ffn1_w_up, ffn1_w_down, final_gain, False, layer, seq_len)
        y_a = _attention(u_bf16, attn_sinks[layer], seq_len)
        y_b = _hgrn(u_bf16, u_f32, hgrn_lb_logits, row(hgrn_norm[layer]), layer, seq_len)
        y_c = _s5_mixer(u_f32, batch, seq_len, s5_a_re[layer], s5_a_im[layer],
                        s5_log_dt[layer], s5_b_re[layer], s5_b_im[layer], s5_c_re[layer],
                        s5_c_im[layer], s5_d[layer], s5_glu_w[layer], s5_glu_b[layer])
        (h,) = _ffn(h, (y_a, y_b, y_c, w_out), None, row(norm_ffn2[layer]), ffn2_w_gate,
                    ffn2_w_up, ffn2_w_down, final_gain, layer == depth - 1, layer, seq_len)
    return h.reshape(batch, seq_len, D_MODEL)
```

```python
import functools
import math

import jax
import jax.numpy as jnp
from jax import lax
from jax.experimental import pallas as pl
from jax.experimental.pallas import tpu as pltpu

D_MODEL = 1024
D_FF = 2816
EPS = 1e-6

ATTN_HEADS = 8
ATTN_KV_HEADS = 2
ATTN_GROUP = ATTN_HEADS // ATTN_KV_HEADS
HEAD_DIM = 64
WINDOW = 128
ATTN_WIDTH = ATTN_HEADS * HEAD_DIM
KV_WIDTH = ATTN_KV_HEADS * HEAD_DIM

HGRN_HEADS = 4
HGRN_DIM = 64
HGRN_WIDTH = HGRN_HEADS * HGRN_DIM
HGRN_CHUNK = 64
HGRN_LEVELS = 6

S5_GROUPS = 16
S5_GROUP_CH = 16
S5_STATE = 64
S5_WIDTH = S5_GROUPS * S5_GROUP_CH
S5_STATES = S5_GROUPS * S5_STATE

IN_PROJ_WIDTH = ATTN_WIDTH + 2 * KV_WIDTH + 4 * HGRN_WIDTH + S5_WIDTH
_F_B_START = ATTN_WIDTH + 2 * KV_WIDTH + HGRN_WIDTH
_U_C_START = IN_PROJ_WIDTH - S5_WIDTH
MIX_BF16_COLUMNS = ((0, _F_B_START), (_F_B_START + HGRN_WIDTH, _U_C_START))
MIX_F32_COLUMNS = ((_F_B_START, _F_B_START + HGRN_WIDTH), (_U_C_START, IN_PROJ_WIDTH))
MIX_BF16_WIDTH = sum(b - a for a, b in MIX_BF16_COLUMNS)
MIX_F32_WIDTH = sum(b - a for a, b in MIX_F32_COLUMNS)

VMEM_LIMIT_BYTES = 56 * 1024 * 1024
SUBLANES = 8

TOKEN_TILE = 512
FF_CHUNK = 256
MIXER_TILE = 4096
S5_TIME_TILE = 128
MASK_VALUE = -1e30

F32 = jnp.float32
BF16 = jnp.bfloat16


def _rms(x, gain):
    return x * lax.rsqrt(jnp.mean(x * x, axis=-1, keepdims=True) + EPS) * gain


def _bdot(a, b):
    return jnp.dot(a.astype(BF16), b.astype(BF16), preferred_element_type=F32)


def _params(n_axes=1):
    return pltpu.CompilerParams(
        dimension_semantics=("arbitrary",) * n_axes,
        vmem_limit_bytes=VMEM_LIMIT_BYTES,
    )


def _full(shape):
    return pl.BlockSpec(shape, lambda i: (0,) * len(shape))


WEIGHT_LOAD_STEPS = 8


def _token_tile(step):
    return jnp.maximum(step - WEIGHT_LOAD_STEPS, 0)


def _weight_chunk_spec(w, layer):
    _, rows, cols = w.shape
    return pl.BlockSpec(
        (None, rows // WEIGHT_LOAD_STEPS, cols),
        lambda i: (layer, jnp.minimum(i, WEIGHT_LOAD_STEPS - 1), 0))


def _keep_weight_chunk(step, w_ref, w_s):
    chunk = w_ref.shape[0]
    w_s[pl.ds(pl.multiple_of(step * chunk, chunk), chunk), :] = w_ref[...].astype(BF16)


def _ffn_kernel(*refs, mix, project, final_norm):
    refs = list(refs)
    take = lambda n: [refs.pop(0) for _ in range(n)]
    (x_ref,) = take(1)
    if mix:
        ya_ref, yb_ref, yc_ref, wo_ref = take(4)
    g_ref, wg_ref, wu_ref, wd_ref, gf_ref = take(5)
    if project:
        gm_ref, wi_ref = take(2)
    (o_ref,) = take(1)
    if project:
        ub_ref, uf_ref = take(2)
    if mix:
        (wo_s,) = take(1)
    wg_s, wu_s, wd_s = take(3)
    streamed = [(wg_ref, wg_s), (wu_ref, wu_s), (wd_ref, wd_s)]
    if mix:
        streamed.append((wo_ref, wo_s))
    if project:
        (wi_s,) = take(1)
        streamed.append((wi_ref, wi_s))
    step = pl.program_id(0)

    @pl.when(step < WEIGHT_LOAD_STEPS)
    def _():
        for w_ref, w_s in streamed:
            _keep_weight_chunk(step, w_ref, w_s)

    @pl.when(step >= WEIGHT_LOAD_STEPS)
    def _():
        x = x_ref[...]
        if mix:
            y = jnp.concatenate([ya_ref[...], yb_ref[...], yc_ref[...]], axis=-1)
            x = x + _bdot(y, wo_s[...])
        h = _rms(x, g_ref[...]).astype(BF16)
        ffn = None
        for j in range(D_FF // FF_CHUNK):
            cols = slice(j * FF_CHUNK, (j + 1) * FF_CHUNK)
            gate = jnp.dot(h, wg_s[:, cols], preferred_element_type=F32)
            up = jnp.dot(h, wu_s[:, cols], preferred_element_type=F32)
            act = (gate * jax.nn.sigmoid(gate) * up).astype(BF16)
            part = jnp.dot(act, wd_s[cols, :], preferred_element_type=F32)
            ffn = part if ffn is None else ffn + part
        y = x + 0.5 * ffn
        if final_norm:
            y = _rms(y, gf_ref[...])
        o_ref[...] = y
        if project:
            u = jnp.dot(_rms(y, gm_ref[...]).astype(BF16), wi_s[...],
                        preferred_element_type=F32)
            for dst, columns in ((ub_ref, MIX_BF16_COLUMNS), (uf_ref, MIX_F32_COLUMNS)):
                dst[...] = jnp.concatenate([u[:, a:b] for a, b in columns],
                                           axis=-1).astype(dst.dtype)


def _ffn(x, mixer_out, in_proj, gain, w_gate, w_up, w_down, final_gain, final_norm,
         layer, seq_len):
    n = x.shape[0]
    tm = min(TOKEN_TILE, seq_len)
    row = lambda width: pl.BlockSpec((tm, width), lambda i: (_token_tile(i), 0))
    vec = _full((1, D_MODEL))
    chunk = lambda w: _weight_chunk_spec(w, layer)
    ffn_weights = [w_gate, w_up, w_down]
    in_specs, args = [row(D_MODEL)], [x]
    out_specs = [row(D_MODEL)]
    out_shape = [jax.ShapeDtypeStruct((n, D_MODEL), F32)]
    scratch = list(ffn_weights)
    if mixer_out is not None:
        y_a, y_b, y_c, w_out = mixer_out
        in_specs += [row(ATTN_WIDTH), row(HGRN_WIDTH), row(S5_WIDTH), chunk(w_out)]
        args += [y_a, y_b, y_c, w_out]
        scratch.insert(0, w_out)
    in_specs += [vec] + [chunk(w) for w in ffn_weights] + [vec]
    args += [gain, *ffn_weights, final_gain]
    if in_proj is not None:
        mix_gain, w_in = in_proj
        in_specs += [vec, chunk(w_in)]
        args += [mix_gain, w_in]
        scratch.append(w_in)
        out_specs += [row(MIX_BF16_WIDTH), row(MIX_F32_WIDTH)]
        out_shape += [jax.ShapeDtypeStruct((n, MIX_BF16_WIDTH), BF16),
                      jax.ShapeDtypeStruct((n, MIX_F32_WIDTH), F32)]
    return pl.pallas_call(
        functools.partial(_ffn_kernel, mix=mixer_out is not None,
                          project=in_proj is not None, final_norm=final_norm),
        grid=(WEIGHT_LOAD_STEPS + n // tm,),
        in_specs=in_specs,
        out_specs=out_specs,
        out_shape=out_shape,
        scratch_shapes=[pltpu.VMEM(w.shape[1:], BF16) for w in scratch],
        compiler_params=_params(),
        name="ffn",
    )(*args)


def _alibi_slope(head):
    return 2.0 ** (-8.0 * (head + 1.0) / ATTN_HEADS)


def _attn_kernel(sink_ref, q_ref, kv_ref, kvp_ref, o_ref, bias_ref, *, seq_len, tile):
    first = (pl.program_id(0) * tile) % seq_len == 0
    cols = ATTN_GROUP * WINDOW
    c_idx = lax.broadcasted_iota(jnp.int32, (2 * WINDOW, cols), 0)
    log2e = math.log2(math.e)
    scale = log2e / math.sqrt(HEAD_DIM)

    @pl.when(pl.program_id(0) == 0)
    def _():
        g_idx = lax.broadcasted_iota(jnp.int32, (1, cols), 1) // WINDOW
        t_idx = lax.broadcasted_iota(jnp.int32, (2 * WINDOW, cols), 1) % WINDOW
        rel = t_idx + WINDOW - c_idx
        in_win = (rel >= 0) & (rel < WINDOW)
        relf = rel.astype(F32)
        for hk in range(ATTN_KV_HEADS):
            slope = jnp.zeros((1, cols), F32)
            sink = jnp.zeros((1, cols), F32)
            for g in range(ATTN_GROUP):
                head = hk * ATTN_GROUP + g
                slope = jnp.where(g_idx == g, _alibi_slope(head) * log2e, slope)
                sink = jnp.where(g_idx == g, sink_ref[head] * log2e, sink)
            bias = jnp.where(in_win, -(slope * relf), MASK_VALUE)
            bias_ref[hk] = jnp.where(c_idx == 0, sink, bias)

    keys = jnp.concatenate([kvp_ref[...], kv_ref[...]], axis=0).astype(F32)
    k_all = keys[:, 0:KV_WIDTH]
    v_t = jnp.transpose(keys[:, KV_WIDTH:2 * KV_WIDTH])
    krow = lax.broadcasted_iota(jnp.int32, (2 * WINDOW, HEAD_DIM), 0)
    vcol = lax.broadcasted_iota(jnp.int32, (HEAD_DIM, 2 * WINDOW), 1)
    ones_rows = jnp.ones((SUBLANES, 2 * WINDOW), F32)
    no_prev = (c_idx >= 1) & (c_idx < jnp.where(first, WINDOW, 0))

    n_blocks = tile // WINDOW
    units = [(blk, hk) for blk in range(n_blocks) for hk in range(ATTN_KV_HEADS)]
    q_t = [jnp.transpose(q_ref[blk * WINDOW:(blk + 1) * WINDOW, :].astype(F32) * scale)
           .astype(BF16) for blk in range(n_blocks)]
    out_t = [[None] * ATTN_HEADS for _ in range(n_blocks)]

    def scores(blk, hk):
        slots = slice(blk * WINDOW, (blk + 2) * WINDOW)
        dims = slice(hk * HEAD_DIM, (hk + 1) * HEAD_DIM)
        kh = jnp.where(krow == 0, 0.0, k_all[slots, dims]).astype(BF16)
        qs = jnp.concatenate(
            [q_t[blk][h * HEAD_DIM:(h + 1) * HEAD_DIM, :]
             for h in range(hk * ATTN_GROUP, (hk + 1) * ATTN_GROUP)], axis=1)
        s = jnp.dot(kh, qs, preferred_element_type=F32) + bias_ref[hk]
        if blk == 0:
            s = jnp.where(no_prev, MASK_VALUE, s)
        return s

    def weighted_values(blk, hk, s):
        slots = slice(blk * WINDOW, (blk + 2) * WINDOW)
        dims = slice(hk * HEAD_DIM, (hk + 1) * HEAD_DIM)
        vh = jnp.where(vcol == 0, 0.0, v_t[dims, slots])
        vh = jnp.concatenate([vh, ones_rows], axis=0).astype(BF16)
        p = jnp.exp2(s - jnp.max(s, axis=0, keepdims=True)).astype(BF16)
        o = jnp.dot(vh, p, preferred_element_type=F32)
        o = o[:HEAD_DIM] * (1.0 / o[HEAD_DIM:HEAD_DIM + 1])
        for g in range(ATTN_GROUP):
            out_t[blk][hk * ATTN_GROUP + g] = o[:, g * WINDOW:(g + 1) * WINDOW]
        if hk == ATTN_KV_HEADS - 1:
            o_ref[blk * WINDOW:(blk + 1) * WINDOW, :] = jnp.transpose(
                jnp.concatenate(out_t[blk], axis=0))

    s = scores(*units[0])
    for i, unit in enumerate(units):
        s_next = scores(*units[i + 1]) if i + 1 < len(units) else None
        weighted_values(*unit, s)
        s = s_next


def _attention(u, sinks, seq_len):
    n = u.shape[0]
    tile = min(MIXER_TILE, seq_len)
    blocks_per_tile = tile // WINDOW
    kv_col = ATTN_WIDTH // (2 * KV_WIDTH)
    return pl.pallas_call(
        functools.partial(_attn_kernel, seq_len=seq_len, tile=tile),
        grid=(n // tile,),
        in_specs=[
            pl.BlockSpec(memory_space=pltpu.SMEM),
            pl.BlockSpec((tile, ATTN_WIDTH), lambda i: (i, 0)),
            pl.BlockSpec((tile, 2 * KV_WIDTH), lambda i: (i, kv_col)),
            pl.BlockSpec((WINDOW, 2 * KV_WIDTH),
                         lambda i: (jnp.maximum(i * blocks_per_tile - 1, 0), kv_col)),
        ],
        out_specs=pl.BlockSpec((tile, ATTN_WIDTH), lambda i: (i, 0)),
        out_shape=jax.ShapeDtypeStruct((n, ATTN_WIDTH), F32),
        scratch_shapes=[pltpu.VMEM((ATTN_KV_HEADS, 2 * WINDOW, ATTN_GROUP * WINDOW), F32)],
        compiler_params=_params(),
        name="swa_attention",
    )(sinks, u, u, u)


HGRN_FACTORED_DECAY_LIMIT = 80.0


def _hgrn_exponent_matrix():
    c = HGRN_CHUNK
    r = jnp.arange(c)[:, None]
    j = jnp.arange(c)[None, :]
    blocks = []
    for lvl in range(HGRN_LEVELS):
        m = 1 << lvl
        start = (r // (2 * m)) * (2 * m)
        right = (r // m) % 2 == 1
        in_right = right & (j >= start + m) & (j <= r)
        in_left = (~right) & (j > r) & (j < start + m)
        blocks.append(in_right | in_left)
    blocks.append(j <= r)
    blocks.append(j > r)
    return jnp.concatenate(blocks, axis=0).astype(BF16)


def _split3(x):
    hi = x.astype(BF16)
    r1 = x - hi.astype(F32)
    mid = r1.astype(BF16)
    lo = (r1 - mid.astype(F32)).astype(BF16)
    return hi, mid, lo


def _hgrn_kernel(q_ref, f_ref, i_ref, g_ref, lbl_ref, gain_ref, em_ref, hm_ref, o_ref,
                 state_ref, state0_ref, qs_ref, ks_ref, lf_ref, bc_ref, *,
                 seq_len, tile, layer):
    c = HGRN_CHUNK
    first = (pl.program_id(0) * tile) % seq_len == 0

    @pl.when(first)
    def _():
        state_ref[...] = jnp.zeros_like(state_ref)

    logits = lbl_ref[...]
    e = jnp.exp(logits - jnp.max(logits, axis=0, keepdims=True))
    sm = e / jnp.sum(e, axis=0, keepdims=True)
    lb = jnp.zeros((1, HGRN_WIDTH), F32)
    for l in range(1, layer + 1):
        lb = lb + sm[l:l + 1, :]

    r_idx = lax.broadcasted_iota(jnp.int32, (c, 1), 0)
    t_idx = lax.broadcasted_iota(jnp.int32, (c, c), 0)
    s_idx = lax.broadcasted_iota(jnp.int32, (c, c), 1)
    lane_head = lax.broadcasted_iota(jnp.int32, (c, HGRN_WIDTH), 1) // HGRN_DIM
    n_chunks = tile // c

    def chunk_rows(ci):
        return pl.ds(pl.multiple_of(ci * c, c), c)

    def prepare(ci):
        rows = pl.ds(ci * c, c)
        z = f_ref[rows, :]
        q = q_ref[rows, :].astype(F32)
        qs_ref[rows, :] = q * jax.nn.sigmoid(q)
        e = jnp.exp(-jnp.abs(z))
        r = 1.0 + e
        inv = 1.0 / r
        pos = z >= 0.0
        sig = jnp.where(pos, 1.0, e) * inv
        ks_ref[rows, :] = (1.0 - lb) * (jnp.where(pos, e, 1.0) * inv)
        lf = jnp.where(lb > 0.0, jnp.log(lb + (1.0 - lb) * sig),
                       jnp.minimum(z, 0.0) - jnp.log(r))
        lf_ref[rows, :] = lf
        tri = em_ref[HGRN_LEVELS * c:(HGRN_LEVELS + 1) * c, :]
        bc_ref[rows, :] = sum(jnp.dot(tri, p, preferred_element_type=F32)
                              for p in _split3(lf))

    def stack_heads(x):
        return jnp.concatenate(
            [jnp.where(lane_head == h, x, 0.0) for h in range(HGRN_HEADS)],
            axis=0).astype(BF16)

    def chunk_scores(ci):
        rows = pl.ds(ci * c, c)
        bc = bc_ref[rows, :]
        qd = (qs_ref[rows, :] * jnp.exp(bc)).astype(BF16)
        kb = stack_heads(ks_ref[rows, :] * jnp.exp(-bc))
        att = lax.dot_general(qd, kb, (((1,), (1,)), ((), ())),
                              preferred_element_type=F32)
        col_s = lax.broadcasted_iota(jnp.int32, (c, HGRN_WIDTH), 1) % c
        row_t = lax.broadcasted_iota(jnp.int32, (c, HGRN_WIDTH), 0)
        return qd, jnp.where(col_s <= row_t, att, 0.0).astype(BF16)

    def chunk_output(ci, qd, att, state_t):
        rows = pl.ds(ci * c, c)
        k, v, bc = ks_ref[rows, :], i_ref[rows, :].astype(F32), bc_ref[rows, :]
        b_last = bc[c - 1:c, :]
        o = jnp.dot(att, stack_heads(v), preferred_element_type=F32)
        o = o + lax.dot_general(qd, state_t.astype(BF16), (((1,), (1,)), ((), ())),
                                preferred_element_type=F32)
        upd = lax.dot_general(v.astype(BF16), (k * jnp.exp(b_last - bc)).astype(BF16),
                              (((0,), (0,)), ((), ())), preferred_element_type=F32)
        rh = lax.broadcasted_iota(jnp.int32, (HGRN_WIDTH, HGRN_WIDTH), 0) // HGRN_DIM
        ch = lax.broadcasted_iota(jnp.int32, (HGRN_WIDTH, HGRN_WIDTH), 1) // HGRN_DIM
        o_ref[rows, :] = o
        return state_t * jnp.exp(b_last) + jnp.where(rh == ch, upd, 0.0)

    def general_chunk(ci, carry):
        rows = chunk_rows(ci)
        q, k, v = qs_ref[rows, :], ks_ref[rows, :], i_ref[rows, :].astype(F32)
        expo = sum(jnp.dot(em_ref[...], p, preferred_element_type=F32)
                   for p in _split3(lf_ref[rows, :]))
        decay = jnp.exp(expo)
        outs = []
        for h in range(HGRN_HEADS):
            cols = slice(h * HGRN_DIM, (h + 1) * HGRN_DIM)
            qh, kh, vh = q[:, cols], k[:, cols], v[:, cols]
            att = jnp.where(t_idx == s_idx,
                            jnp.sum(qh * kh, axis=-1, keepdims=True), 0.0)
            for lvl in range(HGRN_LEVELS):
                m = 1 << lvl
                d = decay[lvl * c:(lvl + 1) * c, cols]
                right = (r_idx // m) % 2 == 1
                ql = jnp.where(right, qh * d, 0.0)
                kl = jnp.where(right, 0.0, kh * d)
                a = lax.dot_general(ql.astype(BF16), kl.astype(BF16),
                                    (((1,), (1,)), ((), ())),
                                    preferred_element_type=F32)
                same = (t_idx // (2 * m)) == (s_idx // (2 * m))
                att = att + jnp.where(same, a, 0.0)
            d_in = decay[HGRN_LEVELS * c:(HGRN_LEVELS + 1) * c, cols]
            d_out = decay[(HGRN_LEVELS + 1) * c:(HGRN_LEVELS + 2) * c, cols]
            state_t = state_ref[cols, cols]
            o = _bdot(att, vh) + lax.dot_general(
                (qh * d_in).astype(BF16), state_t.astype(BF16),
                (((1,), (1,)), ((), ())), preferred_element_type=F32)
            upd_t = lax.dot_general(vh.astype(BF16), (kh * d_out).astype(BF16),
                                    (((0,), (0,)), ((), ())),
                                    preferred_element_type=F32)
            state_ref[cols, cols] = state_t * d_in[c - 1:c, :] + upd_t
            outs.append(o)
        o_ref[rows, :] = jnp.concatenate(outs, axis=-1)
        return carry

    def finish_tile():
        o = o_ref[...]
        sq = sum(jnp.dot(p, hm_ref[...], preferred_element_type=F32)
                 for p in _split3(o * o))
        inv = lax.rsqrt(sq * (1.0 / HGRN_DIM) + EPS)
        gate = g_ref[...].astype(F32)
        o_ref[...] = o * inv * gain_ref[...] * (gate * jax.nn.sigmoid(gate))

    state0_ref[...] = state_ref[...]
    prepare(0)
    if n_chunks > 1:
        prepare(1)
    state_t = state_ref[...]
    scores = chunk_scores(0)
    for ci in range(n_chunks):
        if ci + 2 < n_chunks:
            prepare(ci + 2)
        scores_next = chunk_scores(ci + 1) if ci + 1 < n_chunks else None
        state_t = chunk_output(ci, *scores, state_t)
        scores = scores_next
    state_ref[...] = state_t
    finish_tile()

    @pl.when(jnp.min(bc_ref[...]) < -HGRN_FACTORED_DECAY_LIMIT)
    def _():
        state_ref[...] = state0_ref[...]
        lax.fori_loop(0, n_chunks, general_chunk, 0)
        finish_tile()


def _hgrn(u_bf16, u_f32, lb_logits, gain, layer, seq_len):
    n = u_bf16.shape[0]
    tile = min(MIXER_TILE, seq_len)
    depth = lb_logits.shape[0]
    col = lambda j: pl.BlockSpec((tile, HGRN_WIDTH), lambda i: (i, j))
    q_col = (ATTN_WIDTH + 2 * KV_WIDTH) // HGRN_WIDTH
    em = _hgrn_exponent_matrix()
    channel_head = jnp.arange(HGRN_WIDTH) // HGRN_DIM
    head_mask = (channel_head[:, None] == channel_head[None, :]).astype(BF16)
    tile_f32 = pltpu.VMEM((tile, HGRN_WIDTH), F32)
    return pl.pallas_call(
        functools.partial(_hgrn_kernel, seq_len=seq_len, tile=tile, layer=layer),
        grid=(n // tile,),
        in_specs=[col(q_col), col(0), col(q_col + 1), col(q_col + 2),
                  _full((depth, HGRN_WIDTH)), _full((1, HGRN_WIDTH)),
                  _full(em.shape), _full(head_mask.shape)],
        out_specs=pl.BlockSpec((tile, HGRN_WIDTH), lambda i: (i, 0)),
        out_shape=jax.ShapeDtypeStruct((n, HGRN_WIDTH), F32),
        scratch_shapes=[pltpu.VMEM((HGRN_WIDTH, HGRN_WIDTH), F32),
                        pltpu.VMEM((HGRN_WIDTH, HGRN_WIDTH), F32),
                        tile_f32, tile_f32, tile_f32, tile_f32],
        compiler_params=_params(),
        name="hgrn2",
    )(u_bf16, u_f32, u_bf16, u_bf16, lb_logits, gain, em, head_mask)


def _s5_prep_kernel(ar_ref, ai_ref, ldt_ref, br_ref, bi_ref,
                    abr_ref, abi_ref, bbr_ref, bbi_ref):
    ar, ai = ar_ref[...], ai_ref[...]
    dt = jnp.exp(ldt_ref[...])
    mag = jnp.exp(ar * dt)
    abar_re = mag * jnp.cos(ai * dt)
    abar_im = mag * jnp.sin(ai * dt)
    nr, ni = abar_re - 1.0, abar_im
    den = ar * ar + ai * ai
    z_re = (nr * ar + ni * ai) / den
    z_im = (ni * ar - nr * ai) / den
    abr_ref[...] = abar_re
    abi_ref[...] = abar_im
    br, bi = br_ref[...], bi_ref[...]
    bbr_ref[...] = z_re[:, None, :] * br - z_im[:, None, :] * bi
    bbi_ref[...] = z_re[:, None, :] * bi + z_im[:, None, :] * br


def _s5_prep(a_re, a_im, log_dt, b_re, b_im):
    gp = jax.ShapeDtypeStruct((S5_GROUPS, S5_STATE), F32)
    gcp = jax.ShapeDtypeStruct((S5_GROUPS, S5_GROUP_CH, S5_STATE), F32)
    return pl.pallas_call(
        _s5_prep_kernel,
        out_shape=(gp, gp, gcp, gcp),
        name="s5_discretize",
    )(a_re, a_im, log_dt[:, None], jnp.swapaxes(b_re, 1, 2), jnp.swapaxes(b_im, 1, 2))


def _block_diag(blocks):
    g, r, c = blocks.shape
    eye = jnp.eye(g, dtype=blocks.dtype)
    return (blocks[:, :, None, :] * eye[:, None, :, None]).reshape(g * r, g * c)


def _s5_kernel(u_ref, a_ref, b_ref, c_ref, d_ref, gw_ref, gb_ref, o_ref,
               xa_ref, xb_ref, ua_ref, ub_ref, state_ref, *, batch, steps):
    step = pl.program_id(0)

    @pl.when(step == 0)
    def _():
        state_ref[...] = jnp.zeros_like(state_ref)
        xb_ref[...] = jnp.zeros_like(xb_ref)
        ub_ref[...] = jnp.zeros_like(ub_ref)

    a_re = jnp.broadcast_to(a_ref[0:1, :], (batch, S5_STATES))
    a_im = jnp.broadcast_to(a_ref[1:2, :], (batch, S5_STATES))

    def pipeline_step(x_new, u_new, x_cur, u_cur):
        u = jnp.swapaxes(u_ref[...], 0, 1).reshape(steps * batch, S5_WIDTH)
        u_new[...] = u
        x_new[...] = _bdot(u, b_ref[...])

        x_re, x_im = state_ref[:, 0:S5_STATES], state_ref[:, S5_STATES:2 * S5_STATES]
        for t in range(steps):
            rows = slice(t * batch, (t + 1) * batch)
            bu_re = x_cur[rows, 0:S5_STATES]
            bu_im = x_cur[rows, S5_STATES:2 * S5_STATES]
            x_re, x_im = (a_re * x_re - a_im * x_im + bu_re,
                          a_re * x_im + a_im * x_re + bu_im)
            x_cur[rows, 0:S5_STATES] = x_re
            x_cur[rows, S5_STATES:2 * S5_STATES] = x_im
        state_ref[:, 0:S5_STATES] = x_re
        state_ref[:, S5_STATES:2 * S5_STATES] = x_im

        y = _bdot(x_cur[...], c_ref[...]) + d_ref[...] * u_cur[...]
        z = jax.nn.gelu(y)
        out = z * jax.nn.sigmoid(_bdot(z, gw_ref[...]) + gb_ref[...])
        o_ref[...] = jnp.swapaxes(out.reshape(steps, batch, S5_WIDTH), 0, 1)

    @pl.when(step % 2 == 0)
    def _():
        pipeline_step(xa_ref, ua_ref, xb_ref, ub_ref)

    @pl.when(step % 2 == 1)
    def _():
        pipeline_step(xb_ref, ub_ref, xa_ref, ua_ref)


def _s5(u, abar, b_blk, c_blk, d_skip, glu_w, glu_b, batch, seq_len):
    steps = min(S5_TIME_TILE, seq_len)
    rows = steps * batch
    n_tiles = seq_len // steps
    u_col = MIX_F32_WIDTH // S5_WIDTH - 1
    x_tile = pltpu.VMEM((rows, 2 * S5_STATES), F32)
    u_tile = pltpu.VMEM((rows, S5_WIDTH), F32)
    return pl.pallas_call(
        functools.partial(_s5_kernel, batch=batch, steps=steps),
        grid=(n_tiles + 1,),
        in_specs=[
            pl.BlockSpec((batch, steps, S5_WIDTH),
                         lambda i: (0, jnp.minimum(i, n_tiles - 1), u_col)),
            _full((2, S5_STATES)),
            _full((S5_WIDTH, 2 * S5_STATES)),
            _full((2 * S5_STATES, S5_WIDTH)),
            _full((1, S5_WIDTH)),
            _full((S5_WIDTH, S5_WIDTH)),
            _full((1, S5_WIDTH)),
        ],
        out_specs=pl.BlockSpec((batch, steps, S5_WIDTH),
                               lambda i: (0, jnp.maximum(i - 1, 0), 0)),
        out_shape=jax.ShapeDtypeStruct((batch, seq_len, S5_WIDTH), F32),
        scratch_shapes=[x_tile, x_tile, u_tile, u_tile,
                        pltpu.VMEM((batch, 2 * S5_STATES), F32)],
        compiler_params=_params(),
        name="s5_ssm",
    )(u, abar, b_blk, c_blk, d_skip, glu_w, glu_b)


def _s5_mixer(u, batch, seq_len, a_re, a_im, log_dt, b_re, b_im, c_re, c_im, d_skip,
              glu_w, glu_b):
    abar_re, abar_im, bbar_re, bbar_im = _s5_prep(a_re, a_im, log_dt, b_re, b_im)
    abar = jnp.stack([abar_re.reshape(-1), abar_im.reshape(-1)])
    b_blk = jnp.concatenate([_block_diag(bbar_re), _block_diag(bbar_im)], axis=1)
    c_blk = jnp.concatenate([_block_diag(jnp.swapaxes(c_re, 1, 2)),
                             -_block_diag(jnp.swapaxes(c_im, 1, 2))], axis=0)
    y = _s5(u.reshape(batch, seq_len, MIX_F32_WIDTH), abar, b_blk.astype(BF16),
            c_blk.astype(BF16), d_skip.reshape(1, S5_WIDTH), glu_w.astype(BF16),
            glu_b.reshape(1, S5_WIDTH), batch, seq_len)
    return y.reshape(batch * seq_len, S5_WIDTH)


def kernel(x, norm_ffn1, ffn1_w_gate, ffn1_w_up, ffn1_w_down, norm_mix, w_in, attn_sinks,
           hgrn_lb_logits, hgrn_norm, s5_a_re, s5_a_im, s5_log_dt, s5_b_re, s5_b_im,
           s5_c_re, s5_c_im, s5_d, s5_glu_w, s5_glu_b, w_out, norm_ffn2, ffn2_w_gate,
           ffn2_w_up, ffn2_w_down, norm_final):
    batch, seq_len, _ = x.shape
    depth = w_in.shape[0]
    h = x.reshape(batch * seq_len, D_MODEL)
    row = lambda v: v.reshape(1, -1)
    final_gain = row(norm_final)
    for layer in range(depth):
        h, u_bf16, u_f32 = _ffn(
            h, None, (row(norm_mix[layer]), w_in), row(norm_ffn1[layer]), ffn1_w_gate,
            ffn1_w_up, ffn1_w_down, final_gain, False, layer, seq_len)
        y_a = _attention(u_bf16, attn_sinks[layer], seq_len)
        y_b = _hgrn(u_bf16, u_f32, hgrn_lb_logits, row(hgrn_norm[layer]), layer, seq_len)
        y_c = _s5_mixer(u_f32, batch, seq_len, s5_a_re[layer], s5_a_im[layer],
                        s5_log_dt[layer], s5_b_re[layer], s5_b_im[layer], s5_c_re[layer],
                        s5_c_im[layer], s5_d[layer], s5_glu_w[layer], s5_glu_b[layer])
        (h,) = _ffn(h, (y_a, y_b, y_c, w_out), None, row(norm_ffn2[layer]), ffn2_w_gate,
                    ffn2_w_up, ffn2_w_down, final_gain, layer == depth - 1, layer, seq_len)
    return h.reshape(batch, seq_len, D_MODEL)
```

```python
import functools
import math

import jax
import jax.numpy as jnp
from jax import lax
from jax.experimental import pallas as pl
from jax.experimental.pallas import tpu as pltpu

D_MODEL = 1024
D_FF = 2816
EPS = 1e-6

ATTN_HEADS = 8
ATTN_KV_HEADS = 2
ATTN_GROUP = ATTN_HEADS // ATTN_KV_HEADS
HEAD_DIM = 64
WINDOW = 128
ATTN_WIDTH = ATTN_HEADS * HEAD_DIM
KV_WIDTH = ATTN_KV_HEADS * HEAD_DIM

HGRN_HEADS = 4
HGRN_DIM = 64
HGRN_WIDTH = HGRN_HEADS * HGRN_DIM
HGRN_CHUNK = 64
HGRN_LEVELS = 6

S5_GROUPS = 16
S5_GROUP_CH = 16
S5_STATE = 64
S5_WIDTH = S5_GROUPS * S5_GROUP_CH
S5_STATES = S5_GROUPS * S5_STATE

IN_PROJ_WIDTH = ATTN_WIDTH + 2 * KV_WIDTH + 4 * HGRN_WIDTH + S5_WIDTH
_F_B_START = ATTN_WIDTH + 2 * KV_WIDTH + HGRN_WIDTH
_U_C_START = IN_PROJ_WIDTH - S5_WIDTH
MIX_BF16_COLUMNS = ((0, _F_B_START), (_F_B_START + HGRN_WIDTH, _U_C_START))
MIX_F32_COLUMNS = ((_F_B_START, _F_B_START + HGRN_WIDTH), (_U_C_START, IN_PROJ_WIDTH))
MIX_BF16_WIDTH = sum(b - a for a, b in MIX_BF16_COLUMNS)
MIX_F32_WIDTH = sum(b - a for a, b in MIX_F32_COLUMNS)

VMEM_LIMIT_BYTES = 56 * 1024 * 1024
SUBLANES = 8

TOKEN_TILE = 512
FF_CHUNK = 256
MIXER_TILE = 2048
S5_TIME_TILE = 128
MASK_VALUE = -1e30

F32 = jnp.float32
BF16 = jnp.bfloat16


def _rms(x, gain):
    return x * lax.rsqrt(jnp.mean(x * x, axis=-1, keepdims=True) + EPS) * gain


def _bdot(a, b):
    return jnp.dot(a.astype(BF16), b.astype(BF16), preferred_element_type=F32)


def _params(n_axes=1):
    return pltpu.CompilerParams(
        dimension_semantics=("arbitrary",) * n_axes,
        vmem_limit_bytes=VMEM_LIMIT_BYTES,
    )


def _full(shape):
    return pl.BlockSpec(shape, lambda i: (0,) * len(shape))


WEIGHT_LOAD_STEPS = 8


def _token_tile(step):
    return jnp.maximum(step - WEIGHT_LOAD_STEPS, 0)


def _weight_chunk_spec(w, layer):
    _, rows, cols = w.shape
    return pl.BlockSpec(
        (None, rows // WEIGHT_LOAD_STEPS, cols),
        lambda i: (layer, jnp.minimum(i, WEIGHT_LOAD_STEPS - 1), 0))


def _keep_weight_chunk(step, w_ref, w_s):
    chunk = w_ref.shape[0]
    w_s[pl.ds(pl.multiple_of(step * chunk, chunk), chunk), :] = w_ref[...].astype(BF16)


def _ffn_kernel(*refs, mix, project, final_norm):
    refs = list(refs)
    take = lambda n: [refs.pop(0) for _ in range(n)]
    (x_ref,) = take(1)
    if mix:
        ya_ref, yb_ref, yc_ref, wo_ref = take(4)
    g_ref, wg_ref, wu_ref, wd_ref, gf_ref = take(5)
    if project:
        gm_ref, wi_ref = take(2)
    (o_ref,) = take(1)
    if project:
        ub_ref, uf_ref = take(2)
    if mix:
        (wo_s,) = take(1)
    wg_s, wu_s, wd_s = take(3)
    streamed = [(wg_ref, wg_s), (wu_ref, wu_s), (wd_ref, wd_s)]
    if mix:
        streamed.append((wo_ref, wo_s))
    if project:
        (wi_s,) = take(1)
        streamed.append((wi_ref, wi_s))
    step = pl.program_id(0)

    @pl.when(step < WEIGHT_LOAD_STEPS)
    def _():
        for w_ref, w_s in streamed:
            _keep_weight_chunk(step, w_ref, w_s)

    @pl.when(step >= WEIGHT_LOAD_STEPS)
    def _():
        x = x_ref[...]
        if mix:
            y = jnp.concatenate([ya_ref[...], yb_ref[...], yc_ref[...]], axis=-1)
            x = x + _bdot(y, wo_s[...])
        h = _rms(x, g_ref[...]).astype(BF16)
        ffn = None
        for j in range(D_FF // FF_CHUNK):
            cols = slice(j * FF_CHUNK, (j + 1) * FF_CHUNK)
            gate = jnp.dot(h, wg_s[:, cols], preferred_element_type=F32)
            up = jnp.dot(h, wu_s[:, cols], preferred_element_type=F32)
            act = (gate * jax.nn.sigmoid(gate) * up).astype(BF16)
            part = jnp.dot(act, wd_s[cols, :], preferred_element_type=F32)
            ffn = part if ffn is None else ffn + part
        y = x + 0.5 * ffn
        if final_norm:
            y = _rms(y, gf_ref[...])
        o_ref[...] = y
        if project:
            u = jnp.dot(_rms(y, gm_ref[...]).astype(BF16), wi_s[...],
                        preferred_element_type=F32)
            for dst, columns in ((ub_ref, MIX_BF16_COLUMNS), (uf_ref, MIX_F32_COLUMNS)):
                dst[...] = jnp.concatenate([u[:, a:b] for a, b in columns],
                                           axis=-1).astype(dst.dtype)


def _ffn(x, mixer_out, in_proj, gain, w_gate, w_up, w_down, final_gain, final_norm,
         layer, seq_len):
    n = x.shape[0]
    tm = min(TOKEN_TILE, seq_len)
    row = lambda width: pl.BlockSpec((tm, width), lambda i: (_token_tile(i), 0))
    vec = _full((1, D_MODEL))
    chunk = lambda w: _weight_chunk_spec(w, layer)
    ffn_weights = [w_gate, w_up, w_down]
    in_specs, args = [row(D_MODEL)], [x]
    out_specs = [row(D_MODEL)]
    out_shape = [jax.ShapeDtypeStruct((n, D_MODEL), F32)]
    scratch = list(ffn_weights)
    if mixer_out is not None:
        y_a, y_b, y_c, w_out = mixer_out
        in_specs += [row(ATTN_WIDTH), row(HGRN_WIDTH), row(S5_WIDTH), chunk(w_out)]
        args += [y_a, y_b, y_c, w_out]
        scratch.insert(0, w_out)
    in_specs += [vec] + [chunk(w) for w in ffn_weights] + [vec]
    args += [gain, *ffn_weights, final_gain]
    if in_proj is not None:
        mix_gain, w_in = in_proj
        in_specs += [vec, chunk(w_in)]
        args += [mix_gain, w_in]
        scratch.append(w_in)
        out_specs += [row(MIX_BF16_WIDTH), row(MIX_F32_WIDTH)]
        out_shape += [jax.ShapeDtypeStruct((n, MIX_BF16_WIDTH), BF16),
                      jax.ShapeDtypeStruct((n, MIX_F32_WIDTH), F32)]
    return pl.pallas_call(
        functools.partial(_ffn_kernel, mix=mixer_out is not None,
                          project=in_proj is not None, final_norm=final_norm),
        grid=(WEIGHT_LOAD_STEPS + n // tm,),
        in_specs=in_specs,
        out_specs=out_specs,
        out_shape=out_shape,
        scratch_shapes=[pltpu.VMEM(w.shape[1:], BF16) for w in scratch],
        compiler_params=_params(),
        name="ffn",
    )(*args)


def _alibi_slope(head):
    return 2.0 ** (-8.0 * (head + 1.0) / ATTN_HEADS)


def _attn_kernel(sink_ref, q_ref, kv_ref, kvp_ref, o_ref, bias_ref, *, seq_len, tile):
    first = (pl.program_id(0) * tile) % seq_len == 0
    cols = ATTN_GROUP * WINDOW
    c_idx = lax.broadcasted_iota(jnp.int32, (2 * WINDOW, cols), 0)
    log2e = math.log2(math.e)
    scale = log2e / math.sqrt(HEAD_DIM)

    @pl.when(pl.program_id(0) == 0)
    def _():
        g_idx = lax.broadcasted_iota(jnp.int32, (1, cols), 1) // WINDOW
        t_idx = lax.broadcasted_iota(jnp.int32, (2 * WINDOW, cols), 1) % WINDOW
        rel = t_idx + WINDOW - c_idx
        in_win = (rel >= 0) & (rel < WINDOW)
        relf = rel.astype(F32)
        for hk in range(ATTN_KV_HEADS):
            slope = jnp.zeros((1, cols), F32)
            sink = jnp.zeros((1, cols), F32)
            for g in range(ATTN_GROUP):
                head = hk * ATTN_GROUP + g
                slope = jnp.where(g_idx == g, _alibi_slope(head) * log2e, slope)
                sink = jnp.where(g_idx == g, sink_ref[head] * log2e, sink)
            bias = jnp.where(in_win, -(slope * relf), MASK_VALUE)
            bias_ref[hk] = jnp.where(c_idx == 0, sink, bias)

    keys = jnp.concatenate([kvp_ref[...], kv_ref[...]], axis=0).astype(F32)
    k_all = keys[:, 0:KV_WIDTH]
    v_t = jnp.transpose(keys[:, KV_WIDTH:2 * KV_WIDTH])
    krow = lax.broadcasted_iota(jnp.int32, (2 * WINDOW, HEAD_DIM), 0)
    vcol = lax.broadcasted_iota(jnp.int32, (HEAD_DIM, 2 * WINDOW), 1)
    ones_rows = jnp.ones((SUBLANES, 2 * WINDOW), F32)
    no_prev = (c_idx >= 1) & (c_idx < jnp.where(first, WINDOW, 0))

    n_blocks = tile // WINDOW
    units = [(blk, hk) for blk in range(n_blocks) for hk in range(ATTN_KV_HEADS)]
    q_t = [jnp.transpose(q_ref[blk * WINDOW:(blk + 1) * WINDOW, :].astype(F32) * scale)
           .astype(BF16) for blk in range(n_blocks)]
    out_t = [[None] * ATTN_HEADS for _ in range(n_blocks)]

    def scores(blk, hk):
        slots = slice(blk * WINDOW, (blk + 2) * WINDOW)
        dims = slice(hk * HEAD_DIM, (hk + 1) * HEAD_DIM)
        kh = jnp.where(krow == 0, 0.0, k_all[slots, dims]).astype(BF16)
        qs = jnp.concatenate(
            [q_t[blk][h * HEAD_DIM:(h + 1) * HEAD_DIM, :]
             for h in range(hk * ATTN_GROUP, (hk + 1) * ATTN_GROUP)], axis=1)
        s = jnp.dot(kh, qs, preferred_element_type=F32) + bias_ref[hk]
        if blk == 0:
            s = jnp.where(no_prev, MASK_VALUE, s)
        return s

    def weighted_values(blk, hk, s):
        slots = slice(blk * WINDOW, (blk + 2) * WINDOW)
        dims = slice(hk * HEAD_DIM, (hk + 1) * HEAD_DIM)
        vh = jnp.where(vcol == 0, 0.0, v_t[dims, slots])
        vh = jnp.concatenate([vh, ones_rows], axis=0).astype(BF16)
        p = jnp.exp2(s - jnp.max(s, axis=0, keepdims=True)).astype(BF16)
        o = jnp.dot(vh, p, preferred_element_type=F32)
        o = o[:HEAD_DIM] * (1.0 / o[HEAD_DIM:HEAD_DIM + 1])
        for g in range(ATTN_GROUP):
            out_t[blk][hk * ATTN_GROUP + g] = o[:, g * WINDOW:(g + 1) * WINDOW]
        if hk == ATTN_KV_HEADS - 1:
            o_ref[blk * WINDOW:(blk + 1) * WINDOW, :] = jnp.transpose(
                jnp.concatenate(out_t[blk], axis=0))

    s = scores(*units[0])
    for i, unit in enumerate(units):
        s_next = scores(*units[i + 1]) if i + 1 < len(units) else None
        weighted_values(*unit, s)
        s = s_next


def _attention(u, sinks, seq_len):
    n = u.shape[0]
    tile = min(MIXER_TILE, seq_len)
    blocks_per_tile = tile // WINDOW
    kv_col = ATTN_WIDTH // (2 * KV_WIDTH)
    return pl.pallas_call(
        functools.partial(_attn_kernel, seq_len=seq_len, tile=tile),
        grid=(n // tile,),
        in_specs=[
            pl.BlockSpec(memory_space=pltpu.SMEM),
            pl.BlockSpec((tile, ATTN_WIDTH), lambda i: (i, 0)),
            pl.BlockSpec((tile, 2 * KV_WIDTH), lambda i: (i, kv_col)),
            pl.BlockSpec((WINDOW, 2 * KV_WIDTH),
                         lambda i: (jnp.maximum(i * blocks_per_tile - 1, 0), kv_col)),
        ],
        out_specs=pl.BlockSpec((tile, ATTN_WIDTH), lambda i: (i, 0)),
        out_shape=jax.ShapeDtypeStruct((n, ATTN_WIDTH), F32),
        scratch_shapes=[pltpu.VMEM((ATTN_KV_HEADS, 2 * WINDOW, ATTN_GROUP * WINDOW), F32)],
        compiler_params=_params(),
        name="swa_attention",
    )(sinks, u, u, u)


HGRN_FACTORED_DECAY_LIMIT = 80.0


def _hgrn_exponent_matrix():
    c = HGRN_CHUNK
    r = jnp.arange(c)[:, None]
    j = jnp.arange(c)[None, :]
    blocks = []
    for lvl in range(HGRN_LEVELS):
        m = 1 << lvl
        start = (r // (2 * m)) * (2 * m)
        right = (r // m) % 2 == 1
        in_right = right & (j >= start + m) & (j <= r)
        in_left = (~right) & (j > r) & (j < start + m)
        blocks.append(in_right | in_left)
    blocks.append(j <= r)
    blocks.append(j > r)
    return jnp.concatenate(blocks, axis=0).astype(BF16)


def _split3(x):
    hi = x.astype(BF16)
    r1 = x - hi.astype(F32)
    mid = r1.astype(BF16)
    lo = (r1 - mid.astype(F32)).astype(BF16)
    return hi, mid, lo


def _hgrn_kernel(q_ref, f_ref, i_ref, g_ref, lbl_ref, gain_ref, em_ref, hm_ref, o_ref,
                 state_ref, state0_ref, qs_ref, ks_ref, lf_ref, bc_ref, *,
                 seq_len, tile, layer):
    c = HGRN_CHUNK
    first = (pl.program_id(0) * tile) % seq_len == 0

    @pl.when(first)
    def _():
        state_ref[...] = jnp.zeros_like(state_ref)

    logits = lbl_ref[...]
    e = jnp.exp(logits - jnp.max(logits, axis=0, keepdims=True))
    sm = e / jnp.sum(e, axis=0, keepdims=True)
    lb = jnp.zeros((1, HGRN_WIDTH), F32)
    for l in range(1, layer + 1):
        lb = lb + sm[l:l + 1, :]

    r_idx = lax.broadcasted_iota(jnp.int32, (c, 1), 0)
    t_idx = lax.broadcasted_iota(jnp.int32, (c, c), 0)
    s_idx = lax.broadcasted_iota(jnp.int32, (c, c), 1)
    lane_head = lax.broadcasted_iota(jnp.int32, (c, HGRN_WIDTH), 1) // HGRN_DIM
    n_chunks = tile // c

    def chunk_rows(ci):
        return pl.ds(pl.multiple_of(ci * c, c), c)

    def prepare(ci):
        rows = pl.ds(ci * c, c)
        z = f_ref[rows, :]
        q = q_ref[rows, :].astype(F32)
        qs_ref[rows, :] = q * jax.nn.sigmoid(q)
        e = jnp.exp(-jnp.abs(z))
        r = 1.0 + e
        inv = 1.0 / r
        pos = z >= 0.0
        sig = jnp.where(pos, 1.0, e) * inv
        ks_ref[rows, :] = (1.0 - lb) * (jnp.where(pos, e, 1.0) * inv)
        lf = jnp.where(lb > 0.0, jnp.log(lb + (1.0 - lb) * sig),
                       jnp.minimum(z, 0.0) - jnp.log(r))
        lf_ref[rows, :] = lf
        tri = em_ref[HGRN_LEVELS * c:(HGRN_LEVELS + 1) * c, :]
        bc_ref[rows, :] = sum(jnp.dot(tri, p, preferred_element_type=F32)
                              for p in _split3(lf))

    def stack_heads(x):
        return jnp.concatenate(
            [jnp.where(lane_head == h, x, 0.0) for h in range(HGRN_HEADS)],
            axis=0).astype(BF16)

    def chunk_scores(ci):
        rows = pl.ds(ci * c, c)
        bc = bc_ref[rows, :]
        qd = (qs_ref[rows, :] * jnp.exp(bc)).astype(BF16)
        kb = stack_heads(ks_ref[rows, :] * jnp.exp(-bc))
        att = lax.dot_general(qd, kb, (((1,), (1,)), ((), ())),
                              preferred_element_type=F32)
        col_s = lax.broadcasted_iota(jnp.int32, (c, HGRN_WIDTH), 1) % c
        row_t = lax.broadcasted_iota(jnp.int32, (c, HGRN_WIDTH), 0)
        return qd, jnp.where(col_s <= row_t, att, 0.0).astype(BF16)

    def chunk_output(ci, qd, att, state_t):
        rows = pl.ds(ci * c, c)
        k, v, bc = ks_ref[rows, :], i_ref[rows, :].astype(F32), bc_ref[rows, :]
        b_last = bc[c - 1:c, :]
        o = jnp.dot(att, stack_heads(v), preferred_element_type=F32)
        o = o + lax.dot_general(qd, state_t.astype(BF16), (((1,), (1,)), ((), ())),
                                preferred_element_type=F32)
        upd = lax.dot_general(v.astype(BF16), (k * jnp.exp(b_last - bc)).astype(BF16),
                              (((0,), (0,)), ((), ())), preferred_element_type=F32)
        rh = lax.broadcasted_iota(jnp.int32, (HGRN_WIDTH, HGRN_WIDTH), 0) // HGRN_DIM
        ch = lax.broadcasted_iota(jnp.int32, (HGRN_WIDTH, HGRN_WIDTH), 1) // HGRN_DIM
        o_ref[rows, :] = o
        return state_t * jnp.exp(b_last) + jnp.where(rh == ch, upd, 0.0)

    def general_chunk(ci, carry):
        rows = chunk_rows(ci)
        q, k, v = qs_ref[rows, :], ks_ref[rows, :], i_ref[rows, :].astype(F32)
        expo = sum(jnp.dot(em_ref[...], p, preferred_element_type=F32)
                   for p in _split3(lf_ref[rows, :]))
        decay = jnp.exp(expo)
        outs = []
        for h in range(HGRN_HEADS):
            cols = slice(h * HGRN_DIM, (h + 1) * HGRN_DIM)
            qh, kh, vh = q[:, cols], k[:, cols], v[:, cols]
            att = jnp.where(t_idx == s_idx,
                            jnp.sum(qh * kh, axis=-1, keepdims=True), 0.0)
            for lvl in range(HGRN_LEVELS):
                m = 1 << lvl
                d = decay[lvl * c:(lvl + 1) * c, cols]
                right = (r_idx // m) % 2 == 1
                ql = jnp.where(right, qh * d, 0.0)
                kl = jnp.where(right, 0.0, kh * d)
                a = lax.dot_general(ql.astype(BF16), kl.astype(BF16),
                                    (((1,), (1,)), ((), ())),
                                    preferred_element_type=F32)
                same = (t_idx // (2 * m)) == (s_idx // (2 * m))
                att = att + jnp.where(same, a, 0.0)
            d_in = decay[HGRN_LEVELS * c:(HGRN_LEVELS + 1) * c, cols]
            d_out = decay[(HGRN_LEVELS + 1) * c:(HGRN_LEVELS + 2) * c, cols]
            state_t = state_ref[cols, cols]
            o = _bdot(att, vh) + lax.dot_general(
                (qh * d_in).astype(BF16), state_t.astype(BF16),
                (((1,), (1,)), ((), ())), preferred_element_type=F32)
            upd_t = lax.dot_general(vh.astype(BF16), (kh * d_out).astype(BF16),
                                    (((0,), (0,)), ((), ())),
                                    preferred_element_type=F32)
            state_ref[cols, cols] = state_t * d_in[c - 1:c, :] + upd_t
            outs.append(o)
        o_ref[rows, :] = jnp.concatenate(outs, axis=-1)
        return carry

    def finish_tile():
        o = o_ref[...]
        sq = sum(jnp.dot(p, hm_ref[...], preferred_element_type=F32)
                 for p in _split3(o * o))
        inv = lax.rsqrt(sq * (1.0 / HGRN_DIM) + EPS)
        gate = g_ref[...].astype(F32)
        o_ref[...] = o * inv * gain_ref[...] * (gate * jax.nn.sigmoid(gate))

    state0_ref[...] = state_ref[...]
    prepare(0)
    if n_chunks > 1:
        prepare(1)
    state_t = state_ref[...]
    scores = chunk_scores(0)
    for ci in range(n_chunks):
        if ci + 2 < n_chunks:
            prepare(ci + 2)
        scores_next = chunk_scores(ci + 1) if ci + 1 < n_chunks else None
        state_t = chunk_output(ci, *scores, state_t)
        scores = scores_next
    state_ref[...] = state_t
    finish_tile()

    @pl.when(jnp.min(bc_ref[...]) < -HGRN_FACTORED_DECAY_LIMIT)
    def _():
        state_ref[...] = state0_ref[...]
        lax.fori_loop(0, n_chunks, general_chunk, 0)
        finish_tile()


def _hgrn(u_bf16, u_f32, lb_logits, gain, layer, seq_len):
    n = u_bf16.shape[0]
    tile = min(MIXER_TILE, seq_len)
    depth = lb_logits.shape[0]
    col = lambda j: pl.BlockSpec((tile, HGRN_WIDTH), lambda i: (i, j))
    q_col = (ATTN_WIDTH + 2 * KV_WIDTH) // HGRN_WIDTH
    em = _hgrn_exponent_matrix()
    channel_head = jnp.arange(HGRN_WIDTH) // HGRN_DIM
    head_mask = (channel_head[:, None] == channel_head[None, :]).astype(BF16)
    tile_f32 = pltpu.VMEM((tile, HGRN_WIDTH), F32)
    return pl.pallas_call(
        functools.partial(_hgrn_kernel, seq_len=seq_len, tile=tile, layer=layer),
        grid=(n // tile,),
        in_specs=[col(q_col), col(0), col(q_col + 1), col(q_col + 2),
                  _full((depth, HGRN_WIDTH)), _full((1, HGRN_WIDTH)),
                  _full(em.shape), _full(head_mask.shape)],
        out_specs=pl.BlockSpec((tile, HGRN_WIDTH), lambda i: (i, 0)),
        out_shape=jax.ShapeDtypeStruct((n, HGRN_WIDTH), F32),
        scratch_shapes=[pltpu.VMEM((HGRN_WIDTH, HGRN_WIDTH), F32),
                        pltpu.VMEM((HGRN_WIDTH, HGRN_WIDTH), F32),
                        tile_f32, tile_f32, tile_f32, tile_f32],
        compiler_params=_params(),
        name="hgrn2",
    )(u_bf16, u_f32, u_bf16, u_bf16, lb_logits, gain, em, head_mask)


def _s5_prep_kernel(ar_ref, ai_ref, ldt_ref, br_ref, bi_ref,
                    abr_ref, abi_ref, bbr_ref, bbi_ref):
    ar, ai = ar_ref[...], ai_ref[...]
    dt = jnp.exp(ldt_ref[...])
    mag = jnp.exp(ar * dt)
    abar_re = mag * jnp.cos(ai * dt)
    abar_im = mag * jnp.sin(ai * dt)
    nr, ni = abar_re - 1.0, abar_im
    den = ar * ar + ai * ai
    z_re = (nr * ar + ni * ai) / den
    z_im = (ni * ar - nr * ai) / den
    abr_ref[...] = abar_re
    abi_ref[...] = abar_im
    br, bi = br_ref[...], bi_ref[...]
    bbr_ref[...] = z_re[:, None, :] * br - z_im[:, None, :] * bi
    bbi_ref[...] = z_re[:, None, :] * bi + z_im[:, None, :] * br


def _s5_prep(a_re, a_im, log_dt, b_re, b_im):
    gp = jax.ShapeDtypeStruct((S5_GROUPS, S5_STATE), F32)
    gcp = jax.ShapeDtypeStruct((S5_GROUPS, S5_GROUP_CH, S5_STATE), F32)
    return pl.pallas_call(
        _s5_prep_kernel,
        out_shape=(gp, gp, gcp, gcp),
        name="s5_discretize",
    )(a_re, a_im, log_dt[:, None], jnp.swapaxes(b_re, 1, 2), jnp.swapaxes(b_im, 1, 2))


def _block_diag(blocks):
    g, r, c = blocks.shape
    eye = jnp.eye(g, dtype=blocks.dtype)
    return (blocks[:, :, None, :] * eye[:, None, :, None]).reshape(g * r, g * c)


def _s5_kernel(u_ref, a_ref, b_ref, c_ref, d_ref, gw_ref, gb_ref, o_ref,
               xa_ref, xb_ref, ua_ref, ub_ref, state_ref, *, batch, steps):
    step = pl.program_id(0)

    @pl.when(step == 0)
    def _():
        state_ref[...] = jnp.zeros_like(state_ref)
        xb_ref[...] = jnp.zeros_like(xb_ref)
        ub_ref[...] = jnp.zeros_like(ub_ref)

    a_re = jnp.broadcast_to(a_ref[0:1, :], (batch, S5_STATES))
    a_im = jnp.broadcast_to(a_ref[1:2, :], (batch, S5_STATES))

    def pipeline_step(x_new, u_new, x_cur, u_cur):
        u = jnp.swapaxes(u_ref[...], 0, 1).reshape(steps * batch, S5_WIDTH)
        u_new[...] = u
        x_new[...] = _bdot(u, b_ref[...])

        x_re, x_im = state_ref[:, 0:S5_STATES], state_ref[:, S5_STATES:2 * S5_STATES]
        for t in range(steps):
            rows = slice(t * batch, (t + 1) * batch)
            bu_re = x_cur[rows, 0:S5_STATES]
            bu_im = x_cur[rows, S5_STATES:2 * S5_STATES]
            x_re, x_im = (a_re * x_re - a_im * x_im + bu_re,
                          a_re * x_im + a_im * x_re + bu_im)
            x_cur[rows, 0:S5_STATES] = x_re
            x_cur[rows, S5_STATES:2 * S5_STATES] = x_im
        state_ref[:, 0:S5_STATES] = x_re
        state_ref[:, S5_STATES:2 * S5_STATES] = x_im

        y = _bdot(x_cur[...], c_ref[...]) + d_ref[...] * u_cur[...]
        z = jax.nn.gelu(y)
        out = z * jax.nn.sigmoid(_bdot(z, gw_ref[...]) + gb_ref[...])
        o_ref[...] = jnp.swapaxes(out.reshape(steps, batch, S5_WIDTH), 0, 1)

    @pl.when(step % 2 == 0)
    def _():
        pipeline_step(xa_ref, ua_ref, xb_ref, ub_ref)

    @pl.when(step % 2 == 1)
    def _():
        pipeline_step(xb_ref, ub_ref, xa_ref, ua_ref)


def _s5(u, abar, b_blk, c_blk, d_skip, glu_w, glu_b, batch, seq_len):
    steps = min(S5_TIME_TILE, seq_len)
    rows = steps * batch
    n_tiles = seq_len // steps
    u_col = MIX_F32_WIDTH // S5_WIDTH - 1
    x_tile = pltpu.VMEM((rows, 2 * S5_STATES), F32)
    u_tile = pltpu.VMEM((rows, S5_WIDTH), F32)
    return pl.pallas_call(
        functools.partial(_s5_kernel, batch=batch, steps=steps),
        grid=(n_tiles + 1,),
        in_specs=[
            pl.BlockSpec((batch, steps, S5_WIDTH),
                         lambda i: (0, jnp.minimum(i, n_tiles - 1), u_col)),
            _full((2, S5_STATES)),
            _full((S5_WIDTH, 2 * S5_STATES)),
            _full((2 * S5_STATES, S5_WIDTH)),
            _full((1, S5_WIDTH)),
            _full((S5_WIDTH, S5_WIDTH)),
            _full((1, S5_WIDTH)),
        ],
        out_specs=pl.BlockSpec((batch, steps, S5_WIDTH),
                               lambda i: (0, jnp.maximum(i - 1, 0), 0)),
        out_shape=jax.ShapeDtypeStruct((batch, seq_len, S5_WIDTH), F32),
        scratch_shapes=[x_tile, x_tile, u_tile, u_tile,
                        pltpu.VMEM((batch, 2 * S5_STATES), F32)],
        compiler_params=_params(),
        name="s5_ssm",
    )(u, abar, b_blk, c_blk, d_skip, glu_w, glu_b)


def _s5_mixer(u, batch, seq_len, a_re, a_im, log_dt, b_re, b_im, c_re, c_im, d_skip,
              glu_w, glu_b):
    abar_re, abar_im, bbar_re, bbar_im = _s5_prep(a_re, a_im, log_dt, b_re, b_im)
    abar = jnp.stack([abar_re.reshape(-1), abar_im.reshape(-1)])
    b_blk = jnp.concatenate([_block_diag(bbar_re), _block_diag(bbar_im)], axis=1)
    c_blk = jnp.concatenate([_block_diag(jnp.swapaxes(c_re, 1, 2)),
                             -_block_diag(jnp.swapaxes(c_im, 1, 2))], axis=0)
    y = _s5(u.reshape(batch, seq_len, MIX_F32_WIDTH), abar, b_blk.astype(BF16),
            c_blk.astype(BF16), d_skip.reshape(1, S5_WIDTH), glu_w.astype(BF16),
            glu_b.reshape(1, S5_WIDTH), batch, seq_len)
    return y.reshape(batch * seq_len, S5_WIDTH)


def kernel(x, norm_ffn1, ffn1_w_gate, ffn1_w_up, ffn1_w_down, norm_mix, w_in, attn_sinks,
           hgrn_lb_logits, hgrn_norm, s5_a_re, s5_a_im, s5_log_dt, s5_b_re, s5_b_im,
           s5_c_re, s5_c_im, s5_d, s5_glu_w, s5_glu_b, w_out, norm_ffn2, ffn2_w_gate,
           ffn2_w_up, ffn2_w_down, norm_final):
    batch, seq_len, _ = x.shape
    depth = w_in.shape[0]
    h = x.reshape(batch * seq_len, D_MODEL)
    row = lambda v: v.reshape(1, -1)
    final_gain = row(norm_final)
    for layer in range(depth):
        h, u_bf16, u_f32 = _ffn(
            h, None, (row(norm_mix[layer]), w_in), row(norm_ffn1[layer]), ffn1_w_gate,
            ffn1_w_up, ffn1_w_down, final_gain, False, layer, seq_len)
        y_a = _attention(u_bf16, attn_sinks[layer], seq_len)
        y_b = _hgrn(u_bf16, u_f32, hgrn_lb_logits, row(hgrn_norm[layer]), layer, seq_len)
        y_c = _s5_mixer(u_f32, batch, seq_len, s5_a_re[layer], s5_a_im[layer],
                        s5_log_dt[layer], s5_b_re[layer], s5_b_im[layer], s5_c_re[layer],
                        s5_c_im[layer], s5_d[layer], s5_glu_w[layer], s5_glu_b[layer])
        (h,) = _ffn(h, (y_a, y_b, y_c, w_out), None, row(norm_ffn2[layer]), ffn2_w_gate,
                    ffn2_w_up, ffn2_w_down, final_gain, layer == depth - 1, layer, seq_len)
    return h.reshape(batch, seq_len, D_MODEL)
```
